```python
import math
import jax, jax.numpy as jnp
from jax import lax
import numpy as np

D_MODEL = 1024
BATCH = 8
SEQ = 2048
DEPTH = 1
DEC_BATCH = 128
DEC_SEQ = 8
PAST_LEN = 16384
PAGE_SIZE = 128

POOL_WINDOWS = (2, 4, 8, 16)
N_POOL_GROUPS = 4
D_POOL = D_MODEL
POOL_GROUP = D_POOL // N_POOL_GROUPS
POOL_BUF = 15
N_HEADS = 8
HEAD_K = D_MODEL // N_HEADS
HEAD_V = D_MODEL // N_HEADS
D_QK = N_HEADS * HEAD_K
D_V = N_HEADS * HEAD_V
D_CONV_CH = 2 * D_QK + D_V
CONV_W = 4
CHUNK = 64
N_BRANCH = 2
D_IN = 2 * D_POOL + 2 * D_QK + 2 * D_V + 2 * N_HEADS + N_BRANCH * D_MODEL
DEEPNORM_ALPHA = (2 * DEPTH) ** 0.25
DEEPNORM_BETA = (8 * DEPTH) ** -0.25
LN_EPS = 1e-5
RMS_EPS = 1e-6
L2_EPS = 1e-6

kernel_name = 'pool_gdn_gated_hybrid_step'


def _in_offsets():
    sizes = (D_POOL, D_POOL, D_QK, D_QK, D_V, D_V, N_HEADS, N_HEADS, D_MODEL, D_MODEL)
    return [int(s) for s in np.cumsum(sizes)[:-1]]


def _pool_mixer(u, buf, pos0, pool_w, pool_scale):
    bsz, t_len, _ = u.shape
    ext = jnp.concatenate([buf.astype(u.dtype), u], axis=1)
    ext32 = ext.astype(jnp.float32)
    cs = jnp.concatenate([jnp.zeros_like(ext32[:, :1]), jnp.cumsum(ext32, axis=1)], axis=1)
    pos = pos0 + jnp.arange(t_len, dtype=jnp.int32)
    means = []
    for gi, w in enumerate(POOL_WINDOWS):
        ch = slice(gi * POOL_GROUP, (gi + 1) * POOL_GROUP)
        hi = cs[:, POOL_BUF + 1:POOL_BUF + 1 + t_len, ch]
        lo = cs[:, POOL_BUF + 1 - w:POOL_BUF + 1 - w + t_len, ch]
        cnt = jnp.minimum(pos + 1, w).astype(jnp.float32)[None, :, None]
        means.append((hi - lo) / cnt)
    pooled = (jnp.concatenate(means, axis=-1) - ext32[:, POOL_BUF:]).astype(u.dtype)
    pooled = pooled.reshape(bsz, t_len, N_POOL_GROUPS, POOL_GROUP)
    mixed = jnp.einsum('btgc,gcd->btgd', pooled, pool_w).reshape(bsz, t_len, D_POOL) * pool_scale
    return mixed, ext[:, -POOL_BUF:]


def _short_conv(xc, buf, conv_w):
    t_len = xc.shape[1]
    ext = jnp.concatenate([buf.astype(xc.dtype), xc], axis=1)
    out = sum(ext[:, i:i + t_len] * conv_w[i] for i in range(CONV_W))
    return jax.nn.silu(out), ext[:, -(CONV_W - 1):]


def _l2norm(a):
    return a * lax.rsqrt(jnp.sum(a * a, axis=-1, keepdims=True) + L2_EPS)


def _chunk_gated_delta(q, k, v, beta, g, s0):
    bsz, t_len, n_h, _ = q.shape
    c = min(CHUNK, t_len)
    n_c = -(-t_len // c)
    pad = n_c * c - t_len

    def prep(a):
        a = jnp.pad(a, [(0, 0), (0, pad)] + [(0, 0)] * (a.ndim - 2))
        a = a.reshape((bsz, n_c, c) + a.shape[2:])
        return jnp.moveaxis(jnp.moveaxis(a, 3, 2), 1, 0)

    q, k, v, beta, g = prep(q), prep(k), prep(v), prep(beta), prep(g)
    gc = jnp.cumsum(g, axis=-1)
    incl = jnp.tril(jnp.ones((c, c), dtype=bool))
    strict = jnp.tril(jnp.ones((c, c), dtype=bool), -1)
    decay = jnp.exp(jnp.where(incl, gc[..., :, None] - gc[..., None, :], -jnp.inf))
    kk = jnp.einsum('nbhrd,nbhjd->nbhrj', k, k)
    lmat = jnp.where(strict, beta[..., :, None] * decay * kk, 0.0) + jnp.eye(c, dtype=jnp.float32)
    gam = jnp.exp(gc)
    rhs = jnp.concatenate([beta[..., None] * v, (beta * gam)[..., None] * k], axis=-1)
    sol = lax.linalg.triangular_solve(lmat, rhs, left_side=True, lower=True, unit_diagonal=True)
    w_v, w_k = sol[..., :HEAD_V], sol[..., HEAD_V:]
    qk = jnp.einsum('nbhrd,nbhjd->nbhrj', q, k) * decay
    qg = q * gam[..., None]
    kt = k * jnp.exp(gc[..., -1:] - gc)[..., None]
    glast = gam[..., -1]

    def step(s, xs):
        wv_c, wk_c, qk_c, qg_c, kt_c, gl_c = xs
        u = wv_c - jnp.einsum('bhcd,bhde->bhce', wk_c, s)
        o = jnp.einsum('bhcd,bhde->bhce', qg_c, s) + jnp.einsum('bhrj,bhje->bhre', qk_c, u)
        s = gl_c[..., None, None] * s + jnp.einsum('bhcd,bhce->bhde', kt_c, u)
        return s, o

    s_fin, o = lax.scan(step, s0, (w_v, w_k, qk, qg, kt, glast))
    o = jnp.transpose(o, (1, 0, 3, 2, 4)).reshape(bsz, n_c * c, n_h, HEAD_V)[:, :t_len]
    return o, s_fin


def _layer(x, c, pool_buf, conv_buf, s0, pos0, w_ada, b_ada, w_in, conv_w, a_log, dt_bias,
           head_norm_w, pool_w, pool_scale, p_a, p_b, w_out, ln_g, ln_b):
    dt = x.dtype
    bsz, t_len, _ = x.shape
    mod = jax.nn.silu(c) @ w_ada + b_ada
    shift, scale, gate = jnp.split(mod[:, None, :], 3, axis=-1)
    h = x * (1 + scale) + shift
    proj = h @ w_in
    u_a, z_a, q, k, v, z_b, b_raw, a_raw, ga_raw, gb_raw = jnp.split(proj, _in_offsets(), axis=-1)
    y_a, new_pool = _pool_mixer(u_a, pool_buf, pos0, pool_w, pool_scale)
    y_a = y_a * jax.nn.silu(z_a)
    qkv, new_conv = _short_conv(jnp.concatenate([q, k, v], axis=-1), conv_buf, conv_w)
    q, k, v = jnp.split(qkv.astype(jnp.float32), [D_QK, 2 * D_QK], axis=-1)
    q = _l2norm(q.reshape(bsz, t_len, N_HEADS, HEAD_K)) * (HEAD_K ** -0.5)
    k = _l2norm(k.reshape(bsz, t_len, N_HEADS, HEAD_K))
    v = v.reshape(bsz, t_len, N_HEADS, HEAD_V)
    beta = jax.nn.sigmoid(b_raw.astype(jnp.float32))
    g = -jnp.exp(a_log.astype(jnp.float32)) * jax.nn.softplus(a_raw.astype(jnp.float32) + dt_bias.astype(jnp.float32))
    o, s_new = _chunk_gated_delta(q, k, v, beta, g, s0.astype(jnp.float32))
    o = o * lax.rsqrt(jnp.mean(o * o, axis=-1, keepdims=True) + RMS_EPS) * head_norm_w.astype(jnp.float32)
    y_b = o.reshape(bsz, t_len, D_V).astype(dt) * jax.nn.silu(z_b)
    merged = jax.nn.sigmoid(ga_raw) * (y_a @ p_a) + jax.nn.sigmoid(gb_raw) * (y_b @ p_b)
    sub = (1 + gate) * (merged @ w_out)
    r = (DEEPNORM_ALPHA * x + sub).astype(jnp.float32)
    mu = jnp.mean(r, axis=-1, keepdims=True)
    var = jnp.mean(jnp.square(r - mu), axis=-1, keepdims=True)
    y = (r - mu) * lax.rsqrt(var + LN_EPS) * ln_g.astype(jnp.float32) + ln_b.astype(jnp.float32)
    return y.astype(dt), new_pool, new_conv, s_new.astype(dt)


def setup_inputs(seed: int = 0) -> dict:
    key = jax.random.key(seed)
    ks = jax.random.split(key, 24)
    f32 = jnp.float32
    nrm = lambda kk, shape, s: jax.random.normal(kk, shape, f32) * s
    dt_init = jnp.exp(jax.random.uniform(ks[10], (DEPTH, N_HEADS), f32, math.log(1e-3), math.log(1e-1)))
    return {
        'x_prompt': nrm(ks[0], (BATCH, SEQ, D_MODEL), 1.0),
        'x_sample': nrm(ks[1], (DEC_BATCH, DEC_SEQ, D_MODEL), 1.0),
        'state_pool': nrm(ks[2], (DEPTH, DEC_BATCH, POOL_BUF, D_POOL), 1.0),
        'state_conv': nrm(ks[3], (DEPTH, DEC_BATCH, CONV_W - 1, D_CONV_CH), 1.0),
        'state_delta': nrm(ks[4], (DEPTH, DEC_BATCH, N_HEADS, HEAD_K, HEAD_V), 0.05),
        'c_prompt': nrm(ks[5], (BATCH, D_MODEL), 1.0),
        'c_sample': nrm(ks[6], (DEC_BATCH, D_MODEL), 1.0),
        'w_ada': nrm(ks[7], (DEPTH, D_MODEL, 3 * D_MODEL), 0.02 * D_MODEL ** -0.5),
        'b_ada': nrm(ks[8], (DEPTH, 3 * D_MODEL), 0.02),
        'w_in': nrm(ks[9], (DEPTH, D_MODEL, D_IN), D_MODEL ** -0.5),
        'conv_w': nrm(ks[11], (DEPTH, CONV_W, D_CONV_CH), CONV_W ** -0.5),
        'a_log': jnp.log(jax.random.uniform(ks[12], (DEPTH, N_HEADS), f32, 1.0, 16.0)),
        'dt_bias': dt_init + jnp.log(-jnp.expm1(-dt_init)),
        'head_norm_w': 1.0 + nrm(ks[13], (DEPTH, HEAD_V), 0.02),
        'pool_w': nrm(ks[14], (DEPTH, N_POOL_GROUPS, POOL_GROUP, POOL_GROUP), POOL_GROUP ** -0.5),
        'pool_scale': 1.0 + nrm(ks[15], (DEPTH, D_POOL), 0.02),
        'p_a': nrm(ks[16], (DEPTH, D_POOL, D_MODEL), DEEPNORM_BETA * D_POOL ** -0.5),
        'p_b': nrm(ks[17], (DEPTH, D_V, D_MODEL), DEEPNORM_BETA * D_V ** -0.5),
        'w_out': nrm(ks[18], (DEPTH, D_MODEL, D_MODEL), DEEPNORM_BETA * D_MODEL ** -0.5),
        'ln_g': 1.0 + nrm(ks[19], (DEPTH, D_MODEL), 0.02),
        'ln_b': nrm(ks[20], (DEPTH, D_MODEL), 0.02),
    }


def reference(x_prompt, x_sample, state_pool, state_conv, state_delta, c_prompt, c_sample,
              w_ada, b_ada, w_in, conv_w, a_log, dt_bias, head_norm_w, pool_w, pool_scale,
              p_a, p_b, w_out, ln_g, ln_b):
    bp = x_prompt.shape[0]
    hp, hs = x_prompt, x_sample
    pool_p, conv_p, delta_p, pool_s, conv_s, delta_s = [], [], [], [], [], []
    for l in range(DEPTH):
        wl = (w_ada[l], b_ada[l], w_in[l], conv_w[l], a_log[l], dt_bias[l], head_norm_w[l],
              pool_w[l], pool_scale[l], p_a[l], p_b[l], w_out[l], ln_g[l], ln_b[l])
        zp = jnp.zeros((bp, POOL_BUF, D_POOL), hp.dtype)
        zc = jnp.zeros((bp, CONV_W - 1, D_CONV_CH), hp.dtype)
        zs = jnp.zeros((bp, N_HEADS, HEAD_K, HEAD_V), jnp.float32)
        hp, npl, ncv, nst = _layer(hp, c_prompt, zp, zc, zs, 0, *wl)
        pool_p.append(npl); conv_p.append(ncv); delta_p.append(nst)
        hs, npl, ncv, nst = _layer(hs, c_sample, state_pool[l], state_conv[l], state_delta[l], PAST_LEN, *wl)
        pool_s.append(npl); conv_s.append(ncv); delta_s.append(nst)
    pool_prompt = jnp.stack(pool_p)
    conv_prompt = jnp.stack(conv_p)
    delta_prompt = jnp.stack(delta_p)
    pool_sample = jnp.stack(pool_s)
    conv_sample = jnp.stack(conv_s)
    delta_sample = jnp.stack(delta_s)
    return (hp, hs, pool_prompt, conv_prompt, delta_prompt, pool_sample, conv_sample, delta_sample)
```

```python
import functools

import numpy as np
import jax
import jax.numpy as jnp
from jax import lax
from jax.experimental import pallas as pl
from jax.experimental.pallas import tpu as pltpu

D_MODEL = 1024
N_HEADS = 8
HEAD = 128
POOL_WINDOWS = (2, 4, 8, 16)
POOL_GROUP = 256
POOL_BUF = 15
CONV_W = 4
CHUNK = 64
DEEPNORM_ALPHA = 2.0 ** 0.25
LN_EPS = 1e-5
RMS_EPS = 1e-6
L2_EPS = 1e-6
NEG_BIG = -1e30

OFF_MAIN_END = 6 * D_MODEL
OFF_GATE = OFF_MAIN_END + 2 * N_HEADS

PROMPT_TILE = 256
SAMPLE_SEQS = 8
VMEM_LIMIT_BYTES = 58 * 1024 * 1024

_BF = jnp.bfloat16
_F32 = jnp.float32


def _dot(a, b):
    return jnp.dot(a.astype(_BF), b.astype(_BF), preferred_element_type=_F32)


def _dot_nt(a, b):
    return lax.dot_general(a.astype(_BF), b.astype(_BF), (((1,), (1,)), ((), ())),
                           preferred_element_type=_F32)


def _silu(x):
    return x * jax.nn.sigmoid(x)


def _softplus(x):
    return jnp.maximum(x, 0.0) + jnp.log1p(jnp.exp(-jnp.abs(x)))


def _lane_block_diag(x, width):
    nblk = x.shape[1] // width
    lane = lax.broadcasted_iota(jnp.int32, x.shape, 1)
    return jnp.concatenate([jnp.where(lane // width == i, x, 0.0) for i in range(nblk)], axis=0)


def _segment_cumsum(x, seg):
    row = lax.broadcasted_iota(jnp.int32, x.shape, 0) % seg
    shift = 1
    while shift < seg:
        x = x + jnp.where(row >= shift, pltpu.roll(x, shift, axis=0), 0.0)
        shift *= 2
    return x


def _expand_heads(narrow, e3_ref):
    lane = lax.broadcasted_iota(jnp.int32, narrow.shape, 1)
    x = jnp.where(lane < 2 * N_HEADS, narrow, 0.0)
    hi = x.astype(_BF).astype(_F32)
    rem = x - hi
    mid = rem.astype(_BF).astype(_F32)
    lo = (rem - mid).astype(_BF).astype(_F32)
    packed = hi + pltpu.roll(mid, 2 * N_HEADS, axis=1) + pltpu.roll(lo, 4 * N_HEADS, axis=1)
    wide = jnp.dot(packed.astype(_BF), e3_ref[...], preferred_element_type=_F32)
    beta_e = wide[:, 0:1024]
    beta_5 = wide[:, 1024:1536]
    gc_e = wide[:, 1536:2560]
    gc_5 = wide[:, 2560:3072]
    return beta_e, beta_5, gc_e, gc_5


def _per_head_rsqrt_scale(x, eps, mean, post):
    outs = []
    for h in range(N_HEADS):
        xh = x[:, h * HEAD:(h + 1) * HEAD]
        ss = jnp.sum(xh * xh, axis=-1, keepdims=True)
        if mean:
            ss = ss * (1.0 / HEAD)
        outs.append(xh * (lax.rsqrt(ss + eps) * post))
    return jnp.concatenate(outs, axis=1)


def _gate_scalars(ba, alog_ref, dtb_ref, seg):
    beta = jax.nn.sigmoid(ba)
    g = -jnp.exp(alog_ref[...]) * _softplus(ba + dtb_ref[...])
    gc = _segment_cumsum(g, seg)
    lane = lax.broadcasted_iota(jnp.int32, ba.shape, 1)
    return jnp.where(lane < N_HEADS, beta, gc)


def _chunk_intra(qc, kc, vc, beta_e, beta_5, gc_e, gc_5, glast_e, incl, strict, eye5):
    gam = jnp.exp(gc_e)
    gc_row = jnp.sum(jnp.where(eye5, gc_5, 0.0), axis=0, keepdims=True)
    dec = jnp.exp(jnp.where(incl, gc_5 - gc_row, NEG_BIG))

    lane256 = lax.broadcasted_iota(jnp.int32, (CHUNK, 2 * HEAD), 1)
    kk_parts, qk_parts = [], []
    for p in range(N_HEADS // 2):
        sl = slice(p * 2 * HEAD, (p + 1) * 2 * HEAD)
        kp, qp = kc[:, sl], qc[:, sl]
        lhs = jnp.concatenate([kp, qp], axis=0)
        rhs_t = jnp.concatenate([jnp.where(lane256 < HEAD, kp, 0.0), jnp.where(lane256 >= HEAD, kp, 0.0)], axis=0)
        g2 = _dot_nt(lhs, rhs_t)
        kk_parts.append(g2[:CHUNK])
        qk_parts.append(g2[CHUNK:])
    kk = jnp.concatenate(kk_parts, axis=1)
    qk = jnp.concatenate(qk_parts, axis=1)

    neg_l = jnp.where(strict, -(beta_5 * dec * kk), 0.0)

    t_parts = []
    for grp in range(2):
        m = neg_l[:, grp * 256:(grp + 1) * 256]
        s = eye5[:, :256].astype(_F32) + m
        pw = _dot(m, _lane_block_diag(m, CHUNK))
        for _ in range(4):
            r = _dot(jnp.concatenate([pw, s], axis=0), _lane_block_diag(pw, CHUNK))
            pw = r[:CHUNK]
            s = s + r[CHUNK:]
        s = s + _dot(s, _lane_block_diag(pw, CHUNK))
        t_parts.append(s)

    bv = beta_e * vc
    gbk = beta_e * gam * kc
    lane512 = lax.broadcasted_iota(jnp.int32, (CHUNK, 4 * HEAD), 1) % (2 * HEAD)
    wv_parts, wk_parts = [], []
    for p in range(N_HEADS // 2):
        sl = slice(p * 2 * HEAD, (p + 1) * 2 * HEAD)
        t_pair = t_parts[p // 2][:, (p % 2) * HEAD:(p % 2 + 1) * HEAD]
        x = jnp.concatenate([bv[:, sl], gbk[:, sl]], axis=1)
        w = jnp.concatenate([jnp.where(lane512 < HEAD, x, 0.0), jnp.where(lane512 >= HEAD, x, 0.0)], axis=0)
        sol = _dot(t_pair, w)
        wv_parts.append(sol[:, :2 * HEAD])
        wk_parts.append(sol[:, 2 * HEAD:])
    w_v = jnp.concatenate(wv_parts, axis=1)
    w_k = jnp.concatenate(wk_parts, axis=1)
    qg = qc * gam
    kt = kc * jnp.exp(glast_e - gc_e)
    return w_v, w_k, qg, kt, qk * dec


def _intra_output(qkd, u):
    lane256 = lax.broadcasted_iota(jnp.int32, (CHUNK, 2 * HEAD), 1)
    outs = []
    for p in range(N_HEADS // 2):
        up = u[:, p * 2 * HEAD:(p + 1) * 2 * HEAD]
        ubd = jnp.concatenate([jnp.where(lane256 < HEAD, up, 0.0), jnp.where(lane256 >= HEAD, up, 0.0)], axis=0)
        outs.append(_dot(qkd[:, p * HEAD:(p + 1) * HEAD], ubd))
    return jnp.concatenate(outs, axis=1)


def _pool_mean_minus_token(read_rows, pos, n_rows):
    parts = []
    for gi, w in enumerate(POOL_WINDOWS):
        c0, c1 = gi * POOL_GROUP, (gi + 1) * POOL_GROUP
        cur = read_rows(0, c0, c1)
        acc = cur
        for s in range(1, w):
            acc = acc + read_rows(s, c0, c1)
        inv_cnt = 1.0 / jnp.minimum(pos + 1, w).astype(_F32)
        parts.append(acc * inv_cnt - cur)
    return parts


def _pool_branch(pooled_parts, z_a, poolw_ref, pscale_ref):
    mixed = jnp.concatenate([_dot(pp, poolw_ref[gi]) for gi, pp in enumerate(pooled_parts)], axis=1)
    return mixed * pscale_ref[...] * _silu(z_a)


def _merge_out_ln(x, y_a, y_b, ga, gb, gate, pa_ref, pb_ref, wout_ref, lng_ref, lnb_ref):
    merged = jax.nn.sigmoid(ga) * _dot(y_a, pa_ref[...]) + jax.nn.sigmoid(gb) * _dot(y_b, pb_ref[...])
    sub = (1.0 + gate) * _dot(merged, wout_ref[...])
    r = DEEPNORM_ALPHA * x + sub
    mu = jnp.mean(r, axis=-1, keepdims=True)
    rc = r - mu
    var = jnp.mean(rc * rc, axis=-1, keepdims=True)
    return rc * lax.rsqrt(var + LN_EPS) * lng_ref[...] + lnb_ref[...]


def _ada_kernel(c_ref, w_ref, b_ref, o_ref):
    o_ref[...] = _dot(_silu(c_ref[...]), w_ref[...]) + b_ref[...]


def _ada_call(c_all, w_ada, b_ada):
    n = c_all.shape[0]
    return pl.pallas_call(
        _ada_kernel,
        grid=(3,),
        in_specs=[pl.BlockSpec((n, D_MODEL), lambda j: (0, 0)),
                  pl.BlockSpec((D_MODEL, D_MODEL), lambda j: (0, j)),
                  pl.BlockSpec((1, D_MODEL), lambda j: (0, j))],
        out_specs=pl.BlockSpec((n, D_MODEL), lambda j: (0, j)),
        out_shape=jax.ShapeDtypeStruct((n, 3 * D_MODEL), _F32),
        compiler_params=pltpu.CompilerParams(dimension_semantics=("arbitrary",)),
        name="adaln_mod",
    )(c_all, w_ada, b_ada)


def _prompt_kernel(x_ref, mod_ref, wmain_ref, wba_ref, wg_ref, convw_ref, alog_ref, dtb_ref, hnw_ref,
                   poolw_ref, pscale_ref, pa_ref, pb_ref, wout_ref, lng_ref, lnb_ref, e3_ref,
                   y_ref, pool_out_ref, conv_out_ref, delta_out_ref,
                   extp_ref, extc_ref, sbd_ref, o_scr, *, tile, n_tiles):
    t = pl.program_id(1)

    @pl.when(t == 0)
    def _():
        extp_ref[0:16, :] = jnp.zeros((16, D_MODEL), _F32)
        extc_ref[0:8, :] = jnp.zeros((8, 3 * D_MODEL), _F32)
        sbd_ref[...] = jnp.zeros(sbd_ref.shape, _F32)

    x = x_ref[0]
    mod = mod_ref[0]
    shift, scale, gate = mod[:, 0:D_MODEL], mod[:, D_MODEL:2 * D_MODEL], mod[:, 2 * D_MODEL:]
    h = (x * (1.0 + scale) + shift).astype(_BF)

    extp_ref[16:16 + tile, :] = jnp.dot(h, wmain_ref[:, 0:D_MODEL], preferred_element_type=_F32)
    z_a = jnp.dot(h, wmain_ref[:, D_MODEL:2 * D_MODEL], preferred_element_type=_F32)
    pos = t * tile + lax.broadcasted_iota(jnp.int32, (tile, 1), 0)
    pooled = _pool_mean_minus_token(lambda s, c0, c1: extp_ref[pl.ds(16 - s, tile), c0:c1], pos, tile)
    y_a = _pool_branch(pooled, z_a, poolw_ref, pscale_ref).astype(_BF)

    extc_ref[8:8 + tile, :] = jnp.dot(h, wmain_ref[:, 2 * D_MODEL:5 * D_MODEL], preferred_element_type=_F32)
    conv = extc_ref[pl.ds(5, tile), :] * convw_ref[0:1, :]
    for i in range(1, CONV_W):
        conv = conv + extc_ref[pl.ds(5 + i, tile), :] * convw_ref[i:i + 1, :]
    qkv = _silu(conv)
    q = _per_head_rsqrt_scale(qkv[:, 0:D_MODEL], L2_EPS, False, HEAD ** -0.5)
    k = _per_head_rsqrt_scale(qkv[:, D_MODEL:2 * D_MODEL], L2_EPS, False, 1.0)
    v = qkv[:, 2 * D_MODEL:]

    ba = jnp.dot(h, wba_ref[...], preferred_element_type=_F32)
    beta_e, beta_5, gc_e, gc_5 = _expand_heads(_gate_scalars(ba, alog_ref, dtb_ref, CHUNK), e3_ref)

    row5 = lax.broadcasted_iota(jnp.int32, (CHUNK, 4 * HEAD), 0)
    col5 = lax.broadcasted_iota(jnp.int32, (CHUNK, 4 * HEAD), 1) % CHUNK
    incl, strict, eye5 = row5 >= col5, row5 > col5, row5 == col5
    row_bd = lax.broadcasted_iota(jnp.int32, (2 * HEAD, 2 * HEAD), 0) // HEAD
    col_bd = lax.broadcasted_iota(jnp.int32, (2 * HEAD, 2 * HEAD), 1) // HEAD
    same_head = row_bd == col_bd

    for c in range(tile // CHUNK):
        rows = slice(c * CHUNK, (c + 1) * CHUNK)
        gce = gc_e[rows]
        glast = gce[CHUNK - 1:CHUNK, :]
        w_v, w_k, qg, kt, qkd = _chunk_intra(q[rows], k[rows], v[rows], beta_e[rows], beta_5[rows], gce,
                                             gc_5[rows], glast, incl, strict, eye5)
        gl_row = jnp.exp(glast)
        u_parts, oi_parts = [], []
        for p in range(N_HEADS // 2):
            sl = slice(p * 2 * HEAD, (p + 1) * 2 * HEAD)
            r = _dot(jnp.concatenate([w_k[:, sl], qg[:, sl]], axis=0), sbd_ref[p])
            u_parts.append(w_v[:, sl] - r[:CHUNK])
            oi_parts.append(r[CHUNK:])
        u = jnp.concatenate(u_parts, axis=1)
        o_scr[rows, :] = jnp.concatenate(oi_parts, axis=1) + _intra_output(qkd, u)
        for p in range(N_HEADS // 2):
            sl = slice(p * 2 * HEAD, (p + 1) * 2 * HEAD)
            upd = _dot(kt[:, sl].T, u[:, sl])
            sbd_ref[p] = gl_row[:, sl] * sbd_ref[p] + jnp.where(same_head, upd, 0.0)

    o = _per_head_rsqrt_scale(o_scr[...], RMS_EPS, True, hnw_ref[...])
    z_b = jnp.dot(h, wmain_ref[:, 5 * D_MODEL:6 * D_MODEL], preferred_element_type=_F32)
    y_b = (o * _silu(z_b)).astype(_BF)

    ga = jnp.dot(h, wg_ref[:, 0:D_MODEL], preferred_element_type=_F32)
    gb = jnp.dot(h, wg_ref[:, D_MODEL:], preferred_element_type=_F32)
    y_ref[0] = _merge_out_ln(x, y_a, y_b, ga, gb, gate, pa_ref, pb_ref, wout_ref, lng_ref, lnb_ref)

    @pl.when(t == n_tiles - 1)
    def _():
        pool_out_ref[0] = extp_ref[pl.ds(tile + 1, POOL_BUF), :]
        conv_out_ref[0] = extc_ref[pl.ds(tile + 5, CONV_W - 1), :]
        for hd in range(N_HEADS):
            o0 = (hd % 2) * HEAD
            delta_out_ref[0, hd] = sbd_ref[hd // 2, o0:o0 + HEAD, o0:o0 + HEAD]

    @pl.when(t < n_tiles - 1)
    def _():
        extp_ref[0:16, :] = extp_ref[tile:tile + 16, :]
        extc_ref[0:8, :] = extc_ref[tile:tile + 8, :]


def _const_spec(shape):
    nd = len(shape)
    return pl.BlockSpec(shape, lambda *_: (0,) * nd, pipeline_mode=pl.Buffered(1))


def _weight_specs(weights):
    return [_const_spec(w.shape) for w in weights]


def _prompt_call(x, mod, weights, tile):
    bsz, seq, _ = x.shape
    n_tiles = seq // tile
    kern = functools.partial(_prompt_kernel, tile=tile, n_tiles=n_tiles)
    return pl.pallas_call(
        kern,
        grid=(bsz, n_tiles),
        in_specs=[pl.BlockSpec((1, tile, D_MODEL), lambda b, t: (b, t, 0)),
                  pl.BlockSpec((1, 1, 3 * D_MODEL), lambda b, t: (b, 0, 0))] + _weight_specs(weights),
        out_specs=[pl.BlockSpec((1, tile, D_MODEL), lambda b, t: (b, t, 0)),
                   pl.BlockSpec((1, POOL_BUF, D_MODEL), lambda b, t: (b, 0, 0)),
                   pl.BlockSpec((1, CONV_W - 1, 3 * D_MODEL), lambda b, t: (b, 0, 0)),
                   pl.BlockSpec((1, N_HEADS, HEAD, HEAD), lambda b, t: (b, 0, 0, 0))],
        out_shape=[jax.ShapeDtypeStruct((bsz, seq, D_MODEL), _F32),
                   jax.ShapeDtypeStruct((bsz, POOL_BUF, D_MODEL), _F32),
                   jax.ShapeDtypeStruct((bsz, CONV_W - 1, 3 * D_MODEL), _F32),
                   jax.ShapeDtypeStruct((bsz, N_HEADS, HEAD, HEAD), _F32)],
        scratch_shapes=[pltpu.VMEM((16 + tile, D_MODEL), _F32),
                        pltpu.VMEM((8 + tile, 3 * D_MODEL), _F32),
                        pltpu.VMEM((N_HEADS // 2, 2 * HEAD, 2 * HEAD), _F32),
                        pltpu.VMEM((tile, D_MODEL), _F32)],
        compiler_params=pltpu.CompilerParams(dimension_semantics=("arbitrary", "arbitrary"),
                                             vmem_limit_bytes=VMEM_LIMIT_BYTES),
        name="prompt_layer",
    )(x, mod, *weights)


def _sample_kernel(x_ref, mod_ref, spool_ref, sconv_ref, sdelta_ref,
                   wmain_ref, wba_ref, wg_ref, convw_ref, alog_ref, dtb_ref, hnw_ref,
                   poolw_ref, pscale_ref, pa_ref, pb_ref, wout_ref, lng_ref, lnb_ref, e3_ref,
                   y_ref, pool_out_ref, conv_out_ref, delta_out_ref,
                   extp_ref, extc_ref, *, n_seq, n_tok, pos0):
    rows = n_seq * n_tok
    mod = mod_ref[...]
    shift, scale, gate3 = mod[:, :, 0:D_MODEL], mod[:, :, D_MODEL:2 * D_MODEL], mod[:, :, 2 * D_MODEL:]
    x3 = x_ref[...]
    x = x3.reshape(rows, D_MODEL)
    h = (x3 * (1.0 + scale) + shift).reshape(rows, D_MODEL).astype(_BF)
    gate = jnp.broadcast_to(gate3, (n_seq, n_tok, D_MODEL)).reshape(rows, D_MODEL)

    extp_ref[:, 1:16, :] = spool_ref[...]
    extp_ref[:, 16:16 + n_tok, :] = jnp.dot(h, wmain_ref[:, 0:D_MODEL],
                                             preferred_element_type=_F32).reshape(n_seq, n_tok, D_MODEL)
    z_a = jnp.dot(h, wmain_ref[:, D_MODEL:2 * D_MODEL], preferred_element_type=_F32)
    pos = pos0 + lax.broadcasted_iota(jnp.int32, (rows, 1), 0) % n_tok
    pooled = _pool_mean_minus_token(
        lambda s, c0, c1: extp_ref[:, pl.ds(16 - s, n_tok), c0:c1].reshape(rows, c1 - c0), pos, rows)
    y_a = _pool_branch(pooled, z_a, poolw_ref, pscale_ref).astype(_BF)
    pool_out_ref[...] = extp_ref[:, pl.ds(16 + n_tok - POOL_BUF, POOL_BUF), :]

    extc_ref[:, 5:8, :] = sconv_ref[...]
    extc_ref[:, 8:8 + n_tok, :] = jnp.dot(h, wmain_ref[:, 2 * D_MODEL:5 * D_MODEL],
                                           preferred_element_type=_F32).reshape(n_seq, n_tok, 3 * D_MODEL)
    conv = extc_ref[:, pl.ds(5, n_tok), :].reshape(rows, 3 * D_MODEL) * convw_ref[0:1, :]
    for i in range(1, CONV_W):
        conv = conv + extc_ref[:, pl.ds(5 + i, n_tok), :].reshape(rows, 3 * D_MODEL) * convw_ref[i:i + 1, :]
    conv_out_ref[...] = extc_ref[:, pl.ds(8 + n_tok - (CONV_W - 1), CONV_W - 1), :]
    qkv = _silu(conv)
    q = _per_head_rsqrt_scale(qkv[:, 0:D_MODEL], L2_EPS, False, HEAD ** -0.5)
    k = _per_head_rsqrt_scale(qkv[:, D_MODEL:2 * D_MODEL], L2_EPS, False, 1.0)
    v = qkv[:, 2 * D_MODEL:]

    ba = jnp.dot(h, wba_ref[...], preferred_element_type=_F32)
    beta_e, beta_5, gc_e, gc_5 = _expand_heads(_gate_scalars(ba, alog_ref, dtb_ref, n_tok), e3_ref)

    row5 = lax.broadcasted_iota(jnp.int32, (CHUNK, 4 * HEAD), 0)
    col5 = lax.broadcasted_iota(jnp.int32, (CHUNK, 4 * HEAD), 1) % CHUNK
    same_seq = (row5 // n_tok) == (col5 // n_tok)
    incl, strict, eye5 = same_seq & (row5 >= col5), same_seq & (row5 > col5), row5 == col5

    glast = jnp.concatenate(
        [jnp.broadcast_to(gc_e[(s + 1) * n_tok - 1:(s + 1) * n_tok, :], (n_tok, D_MODEL)) for s in range(n_seq)], axis=0)
    w_v, w_k, qg, kt, qkd = _chunk_intra(q, k, v, beta_e, beta_5, gc_e, gc_5, glast, incl, strict, eye5)
    gl_all = jnp.exp(glast)

    u_rows, oi_rows = [], []
    for s in range(n_seq):
        rs = slice(s * n_tok, (s + 1) * n_tok)
        u_h, oi_h = [], []
        for hd in range(N_HEADS):
            cs = slice(hd * HEAD, (hd + 1) * HEAD)
            r = _dot(jnp.concatenate([w_k[rs, cs], qg[rs, cs]], axis=0), sdelta_ref[s, hd])
            u_h.append(w_v[rs, cs] - r[:n_tok])
            oi_h.append(r[n_tok:])
        u_rows.append(jnp.concatenate(u_h, axis=1))
        oi_rows.append(jnp.concatenate(oi_h, axis=1))
    u = jnp.concatenate(u_rows, axis=0)
    o = jnp.concatenate(oi_rows, axis=0) + _intra_output(qkd, u)

    lane64 = lax.broadcasted_iota(jnp.int32, (HEAD, rows), 1) // n_tok
    for hd in range(N_HEADS):
        cs = slice(hd * HEAD, (hd + 1) * HEAD)
        kt_t = kt[:, cs].T
        lhs = jnp.concatenate([jnp.where(lane64 == s, kt_t, 0.0) for s in range(n_seq)], axis=0)
        upd = _dot(lhs, u[:, cs])
        for s in range(n_seq):
            gl = gl_all[s * n_tok:s * n_tok + 1, cs]
            delta_out_ref[s, hd] = gl * sdelta_ref[s, hd] + upd[s * HEAD:(s + 1) * HEAD]

    o = _per_head_rsqrt_scale(o, RMS_EPS, True, hnw_ref[...])
    z_b = jnp.dot(h, wmain_ref[:, 5 * D_MODEL:6 * D_MODEL], preferred_element_type=_F32)
    y_b = (o * _silu(z_b)).astype(_BF)
    ga = jnp.dot(h, wg_ref[:, 0:D_MODEL], preferred_element_type=_F32)
    gb = jnp.dot(h, wg_ref[:, D_MODEL:], preferred_element_type=_F32)
    y = _merge_out_ln(x, y_a, y_b, ga, gb, gate, pa_ref, pb_ref, wout_ref, lng_ref, lnb_ref)
    y_ref[...] = y.reshape(n_seq, n_tok, D_MODEL)


def _sample_call(x, mod, spool, sconv, sdelta, weights, pos0):
    bsz, n_tok, _ = x.shape
    n_seq = SAMPLE_SEQS
    assert n_seq * n_tok == CHUNK and bsz % n_seq == 0
    kern = functools.partial(_sample_kernel, n_seq=n_seq, n_tok=n_tok, pos0=pos0)
    seq_block = lambda shape: pl.BlockSpec((n_seq,) + shape, lambda i: (i,) + (0,) * len(shape))
    return pl.pallas_call(
        kern,
        grid=(bsz // n_seq,),
        in_specs=[seq_block((n_tok, D_MODEL)), seq_block((1, 3 * D_MODEL)),
                  seq_block((POOL_BUF, D_MODEL)), seq_block((CONV_W - 1, 3 * D_MODEL)),
                  seq_block((N_HEADS, HEAD, HEAD))] + _weight_specs(weights),
        out_specs=[seq_block((n_tok, D_MODEL)), seq_block((POOL_BUF, D_MODEL)),
                   seq_block((CONV_W - 1, 3 * D_MODEL)), seq_block((N_HEADS, HEAD, HEAD))],
        out_shape=[jax.ShapeDtypeStruct((bsz, n_tok, D_MODEL), _F32),
                   jax.ShapeDtypeStruct((bsz, POOL_BUF, D_MODEL), _F32),
                   jax.ShapeDtypeStruct((bsz, CONV_W - 1, 3 * D_MODEL), _F32),
                   jax.ShapeDtypeStruct((bsz, N_HEADS, HEAD, HEAD), _F32)],
        scratch_shapes=[pltpu.VMEM((n_seq, 16 + n_tok, D_MODEL), _F32),
                        pltpu.VMEM((n_seq, 8 + n_tok, 3 * D_MODEL), _F32)],
        compiler_params=pltpu.CompilerParams(dimension_semantics=("arbitrary",),
                                             vmem_limit_bytes=VMEM_LIMIT_BYTES),
        name="sample_layer",
    )(x, mod, spool, sconv, sdelta, *weights)


def _head_expansion_matrix():
    e = np.zeros((HEAD, 3072), np.float32)
    for piece in range(3):
        for hd in range(N_HEADS):
            rb, rg = piece * 2 * N_HEADS + hd, piece * 2 * N_HEADS + N_HEADS + hd
            e[rb, hd * HEAD:(hd + 1) * HEAD] = 1.0
            e[rb, 1024 + hd * CHUNK:1024 + (hd + 1) * CHUNK] = 1.0
            e[rg, 1536 + hd * HEAD:1536 + (hd + 1) * HEAD] = 1.0
            e[rg, 2560 + hd * CHUNK:2560 + (hd + 1) * CHUNK] = 1.0
    return jnp.asarray(e, _BF)


def _layer_weights(w_in, conv_w, a_log, dt_bias, head_norm_w, pool_w, pool_scale, p_a, p_b, w_out, ln_g, ln_b):
    w_ba = jnp.zeros((D_MODEL, HEAD), _BF).at[:, 0:2 * N_HEADS].set(w_in[:, OFF_MAIN_END:OFF_GATE].astype(_BF))
    lane_pad = lambda a: jnp.zeros((1, HEAD), _F32).at[0, N_HEADS:2 * N_HEADS].set(a)
    return (w_in[:, :OFF_MAIN_END].astype(_BF), w_ba, w_in[:, OFF_GATE:].astype(_BF), conv_w,
            lane_pad(a_log), lane_pad(dt_bias), head_norm_w.reshape(1, HEAD), pool_w.astype(_BF),
            pool_scale.reshape(1, D_MODEL), p_a.astype(_BF), p_b.astype(_BF), w_out.astype(_BF),
            ln_g.reshape(1, D_MODEL), ln_b.reshape(1, D_MODEL), _head_expansion_matrix())


def kernel(x_prompt, x_sample, state_pool, state_conv, state_delta, c_prompt, c_sample, w_ada, b_ada, w_in, conv_w, a_log, dt_bias, head_norm_w, pool_w, pool_scale, p_a, p_b, w_out, ln_g, ln_b):
    depth = w_in.shape[0]
    assert depth == 1, "single-layer trunk"
    bp = x_prompt.shape[0]
    past_len = 16384
    hp, hs = x_prompt, x_sample
    outs = []
    for l in range(depth):
        mod = _ada_call(jnp.concatenate([c_prompt, c_sample], axis=0), w_ada[l], b_ada[l].reshape(1, -1))
        mod = mod.reshape(mod.shape[0], 1, 3 * D_MODEL)
        weights = _layer_weights(w_in[l], conv_w[l], a_log[l], dt_bias[l], head_norm_w[l], pool_w[l],
                                 pool_scale[l], p_a[l], p_b[l], w_out[l], ln_g[l], ln_b[l])
        hp, pool_p, conv_p, delta_p = _prompt_call(hp, mod[:bp], weights, min(PROMPT_TILE, hp.shape[1]))
        hs, pool_s, conv_s, delta_s = _sample_call(hs, mod[bp:], state_pool[l], state_conv[l], state_delta[l],
                                                   weights, past_len)
        outs.append((pool_p, conv_p, delta_p, pool_s, conv_s, delta_s))
    stack = lambda i: jnp.stack([o[i] for o in outs])
    return (hp, hs, stack(0), stack(1), stack(2), stack(3), stack(4), stack(5))
```

```python
import functools

import numpy as np
import jax
import jax.numpy as jnp
from jax import lax
from jax.experimental import pallas as pl
from jax.experimental.pallas import tpu as pltpu

D_MODEL = 1024
N_HEADS = 8
HEAD = 128
N_PAIRS = N_HEADS // 2
POOL_WINDOWS = (2, 4, 8, 16)
POOL_GROUP = 256
POOL_BUF = 15
POOL_PAD = 16
CONV_W = 4
CONV_PAD = 8
CHUNK = 64
DEEPNORM_ALPHA = 2.0 ** 0.25
LN_EPS = 1e-5
RMS_EPS = 1e-6
L2_EPS = 1e-6
NEG_BIG = -1e30

OFF_MAIN_END = 6 * D_MODEL
OFF_GATE = OFF_MAIN_END + 2 * N_HEADS

PROMPT_SEQS = 4
PROMPT_TOKENS = CHUNK
SAMPLE_SEQS = 8
VMEM_LIMIT_BYTES = 58 * 1024 * 1024

_BF = jnp.bfloat16
_F32 = jnp.float32


def _dot(a, b):
    return jnp.dot(a.astype(_BF), b.astype(_BF), preferred_element_type=_F32)


def _dot_nt(a, b):
    return lax.dot_general(a.astype(_BF), b.astype(_BF), (((1,), (1,)), ((), ())),
                           preferred_element_type=_F32)


def _silu(x):
    return x * jax.nn.sigmoid(x)


def _softplus(x):
    return jnp.maximum(x, 0.0) + jnp.log1p(jnp.exp(-jnp.abs(x)))


def _lane_block_diag(x, width):
    nblk = x.shape[1] // width
    lane = lax.broadcasted_iota(jnp.int32, x.shape, 1)
    return jnp.concatenate([jnp.where(lane // width == i, x, 0.0) for i in range(nblk)], axis=0)


def _pair_block_diag(x):
    lane = lax.broadcasted_iota(jnp.int32, x.shape, 1) % (2 * HEAD)
    return jnp.concatenate([jnp.where(lane < HEAD, x, 0.0), jnp.where(lane >= HEAD, x, 0.0)], axis=0)


def _segment_cumsum(x, seg):
    row = lax.broadcasted_iota(jnp.int32, x.shape, 0) % seg
    shift = 1
    while shift < seg:
        x = x + jnp.where(row >= shift, pltpu.roll(x, shift, axis=0), 0.0)
        shift *= 2
    return x


def _expand_heads(narrow, e3_ref):
    lane = lax.broadcasted_iota(jnp.int32, narrow.shape, 1)
    x = jnp.where(lane < 2 * N_HEADS, narrow, 0.0)
    hi = x.astype(_BF).astype(_F32)
    rem = x - hi
    mid = rem.astype(_BF).astype(_F32)
    lo = (rem - mid).astype(_BF).astype(_F32)
    packed = hi + pltpu.roll(mid, 2 * N_HEADS, axis=1) + pltpu.roll(lo, 4 * N_HEADS, axis=1)
    wide = jnp.dot(packed.astype(_BF), e3_ref[...], preferred_element_type=_F32)
    return wide[:, 0:1024], wide[:, 1024:1536], wide[:, 1536:2560], wide[:, 2560:3072]


def _per_head_rsqrt_scale(x, eps, mean, post):
    outs = []
    for h in range(N_HEADS):
        xh = x[:, h * HEAD:(h + 1) * HEAD]
        ss = jnp.sum(xh * xh, axis=-1, keepdims=True)
        if mean:
            ss = ss * (1.0 / HEAD)
        outs.append(xh * (lax.rsqrt(ss + eps) * post))
    return jnp.concatenate(outs, axis=1)


def _gate_scalars(ba, alog_ref, dtb_ref, seg):
    beta = jax.nn.sigmoid(ba)
    g = -jnp.exp(alog_ref[...]) * _softplus(ba + dtb_ref[...])
    gc = _segment_cumsum(g, seg)
    lane = lax.broadcasted_iota(jnp.int32, ba.shape, 1)
    return jnp.where(lane < N_HEADS, beta, gc)


def _chunk_masks(n_tok):
    row = lax.broadcasted_iota(jnp.int32, (CHUNK, 4 * HEAD), 0)
    col = lax.broadcasted_iota(jnp.int32, (CHUNK, 4 * HEAD), 1) % CHUNK
    same_seq = (row // n_tok) == (col // n_tok)
    return same_seq & (row >= col), same_seq & (row > col), row == col


def _chunks_intra(q, k, v, beta_e, beta_5, gc_e, gc_5, n_tok):
    n_chunks = q.shape[0] // CHUNK
    incl, strict, eye5 = _chunk_masks(n_tok)
    rows = [slice(c * CHUNK, (c + 1) * CHUNK) for c in range(n_chunks)]
    pair = [slice(p * 2 * HEAD, (p + 1) * 2 * HEAD) for p in range(N_PAIRS)]
    lane256 = lax.broadcasted_iota(jnp.int32, (CHUNK, 2 * HEAD), 1)

    kk, qk = [], []
    for rs in rows:
        kk_parts, qk_parts = [], []
        for sl in pair:
            kp, qp = k[rs, sl], q[rs, sl]
            g2 = _dot_nt(jnp.concatenate([kp, qp], axis=0), _pair_block_diag(kp))
            kk_parts.append(g2[:CHUNK])
            qk_parts.append(g2[CHUNK:])
        kk.append(jnp.concatenate(kk_parts, axis=1))
        qk.append(jnp.concatenate(qk_parts, axis=1))

    dec, glast = [], []
    for rs in rows:
        g5 = gc_5[rs]
        gc_row = jnp.sum(jnp.where(eye5, g5, 0.0), axis=0, keepdims=True)
        dec.append(jnp.exp(jnp.where(incl, g5 - gc_row, NEG_BIG)))
        ge = gc_e[rs]
        glast.append(jnp.concatenate(
            [jnp.broadcast_to(ge[(s + 1) * n_tok - 1:(s + 1) * n_tok, :], (n_tok, D_MODEL))
             for s in range(CHUNK // n_tok)], axis=0))

    prob = [(c, grp) for c in range(n_chunks) for grp in range(2)]
    m, s, pw = {}, {}, {}
    for c, grp in prob:
        neg_l = jnp.where(strict[:, :256], -(beta_5[rows[c], grp * 256:(grp + 1) * 256]
                                              * dec[c][:, grp * 256:(grp + 1) * 256]
                                              * kk[c][:, grp * 256:(grp + 1) * 256]), 0.0)
        m[c, grp] = neg_l
        s[c, grp] = eye5[:, :256].astype(_F32) + neg_l
    for key in prob:
        pw[key] = _dot(m[key], _lane_block_diag(m[key], CHUNK))
    for _ in range(4):
        for key in prob:
            r = _dot(jnp.concatenate([pw[key], s[key]], axis=0), _lane_block_diag(pw[key], CHUNK))
            pw[key] = r[:CHUNK]
            s[key] = s[key] + r[CHUNK:]
    for key in prob:
        s[key] = s[key] + _dot(s[key], _lane_block_diag(pw[key], CHUNK))

    out = []
    for c, rs in enumerate(rows):
        gam = jnp.exp(gc_e[rs])
        be = beta_e[rs]
        bv = be * v[rs]
        gbk = be * gam * k[rs]
        wv_parts, wk_parts = [], []
        for p, sl in enumerate(pair):
            t_pair = s[c, p // 2][:, (p % 2) * HEAD:(p % 2 + 1) * HEAD]
            sol = _dot(t_pair, _pair_block_diag(jnp.concatenate([bv[:, sl], gbk[:, sl]], axis=1)))
            wv_parts.append(sol[:, :2 * HEAD])
            wk_parts.append(sol[:, 2 * HEAD:])
        out.append(dict(w_v=jnp.concatenate(wv_parts, axis=1), w_k=jnp.concatenate(wk_parts, axis=1),
                        qg=q[rs] * gam, kt=k[rs] * jnp.exp(glast[c] - gc_e[rs]),
                        qkd=qk[c] * dec[c], gl=jnp.exp(glast[c])))
    return out


def _intra_output(qkd, u):
    outs = []
    for p in range(N_PAIRS):
        up = u[:, p * 2 * HEAD:(p + 1) * 2 * HEAD]
        outs.append(_dot(qkd[:, p * HEAD:(p + 1) * HEAD], _pair_block_diag(up)))
    return jnp.concatenate(outs, axis=1)


def _front(x3, mod3, pos, refs, extp_ref, extc_ref):
    n_seq, n_tok, _ = x3.shape
    rows = n_seq * n_tok
    shift, scale = mod3[:, :, 0:D_MODEL], mod3[:, :, D_MODEL:2 * D_MODEL]
    h = (x3 * (1.0 + scale) + shift).reshape(rows, D_MODEL).astype(_BF)
    wmain_ref = refs["wmain"]
    proj = lambda c0, c1: jnp.dot(h, wmain_ref[:, c0:c1], preferred_element_type=_F32)

    extp_ref[:, POOL_PAD:POOL_PAD + n_tok, :] = proj(0, D_MODEL).reshape(n_seq, n_tok, D_MODEL)
    extc_ref[:, CONV_PAD:CONV_PAD + n_tok, :] = proj(2 * D_MODEL, 5 * D_MODEL).reshape(n_seq, n_tok, 3 * D_MODEL)
    z_a = proj(D_MODEL, 2 * D_MODEL)
    ba = jnp.dot(h, refs["wba"][...], preferred_element_type=_F32)
    z_b = proj(5 * D_MODEL, 6 * D_MODEL)
    ga = jnp.dot(h, refs["wg"][:, 0:D_MODEL], preferred_element_type=_F32)
    gb = jnp.dot(h, refs["wg"][:, D_MODEL:], preferred_element_type=_F32)

    seg_p = POOL_PAD + n_tok
    pooled = []
    for gi, w in enumerate(POOL_WINDOWS):
        e = extp_ref[:, :, gi * POOL_GROUP:(gi + 1) * POOL_GROUP].reshape(n_seq * seg_p, POOL_GROUP)
        acc, span = e, 1
        while span < w:
            acc = acc + pltpu.roll(acc, span, axis=0)
            span *= 2
        take = lambda a: a.reshape(n_seq, seg_p, POOL_GROUP)[:, POOL_PAD:, :].reshape(rows, POOL_GROUP)
        inv_cnt = 1.0 / jnp.minimum(pos + 1, w).astype(_F32)
        pooled.append(take(acc) * inv_cnt - take(e))
    mixed = jnp.concatenate([_dot(pp, refs["poolw"][gi]) for gi, pp in enumerate(pooled)], axis=1)
    y_a = (mixed * refs["pscale"][...] * _silu(z_a)).astype(_BF)
    a_proj = _dot(y_a, refs["pa"][...])

    seg_c = CONV_PAD + n_tok
    convw_ref = refs["convw"]
    e = extc_ref[...].reshape(n_seq * seg_c, 3 * D_MODEL)
    acc = e * convw_ref[0:1, :]
    for i in range(1, CONV_W):
        acc = pltpu.roll(acc, 1, axis=0) + e * convw_ref[i:i + 1, :]
    qkv = _silu(acc.reshape(n_seq, seg_c, 3 * D_MODEL)[:, CONV_PAD:, :].reshape(rows, 3 * D_MODEL))
    q = _per_head_rsqrt_scale(qkv[:, 0:D_MODEL], L2_EPS, False, HEAD ** -0.5)
    k = _per_head_rsqrt_scale(qkv[:, D_MODEL:2 * D_MODEL], L2_EPS, False, 1.0)
    v = qkv[:, 2 * D_MODEL:]

    beta_e, beta_5, gc_e, gc_5 = _expand_heads(_gate_scalars(ba, refs["alog"], refs["dtb"], n_tok), refs["e3"])
    return dict(q=q, k=k, v=v, beta_e=beta_e, beta_5=beta_5, gc_e=gc_e, gc_5=gc_5,
                a_proj=a_proj, z_b=z_b, ga=ga, gb=gb)


def _back(x, o, f, gate, refs):
    o = _per_head_rsqrt_scale(o, RMS_EPS, True, refs["hnw"][...])
    y_b = (o * _silu(f["z_b"])).astype(_BF)
    merged = jax.nn.sigmoid(f["ga"]) * f["a_proj"] + jax.nn.sigmoid(f["gb"]) * _dot(y_b, refs["pb"][...])
    sub = (1.0 + gate) * _dot(merged, refs["wout"][...])
    r = DEEPNORM_ALPHA * x + sub
    mu = jnp.mean(r, axis=-1, keepdims=True)
    rc = r - mu
    var = jnp.mean(rc * rc, axis=-1, keepdims=True)
    return rc * lax.rsqrt(var + LN_EPS) * refs["lng"][...] + refs["lnb"][...]


_WEIGHT_NAMES = ("wmain", "wba", "wg", "convw", "alog", "dtb", "hnw", "poolw", "pscale", "pa", "pb",
                 "wout", "lng", "lnb", "e3")


def _ada_kernel(c_ref, w_ref, b_ref, o_ref):
    o_ref[...] = _dot(_silu(c_ref[...]), w_ref[...]) + b_ref[...]


def _ada_call(c_all, w_ada, b_ada):
    n = c_all.shape[0]
    return pl.pallas_call(
        _ada_kernel,
        grid=(3,),
        in_specs=[pl.BlockSpec((n, D_MODEL), lambda j: (0, 0)),
                  pl.BlockSpec((D_MODEL, D_MODEL), lambda j: (0, j)),
                  pl.BlockSpec((1, D_MODEL), lambda j: (0, j))],
        out_specs=pl.BlockSpec((n, D_MODEL), lambda j: (0, j)),
        out_shape=jax.ShapeDtypeStruct((n, 3 * D_MODEL), _F32),
        compiler_params=pltpu.CompilerParams(dimension_semantics=("arbitrary",)),
        name="adaln_mod",
    )(c_all, w_ada, b_ada)


def _prompt_kernel(x_ref, mod_ref, *rest, n_seq, n_tok, n_steps):
    refs = dict(zip(_WEIGHT_NAMES, rest[:len(_WEIGHT_NAMES)]))
    (y_ref, pool_out_ref, conv_out_ref, delta_out_ref,
     extp_ref, extc_ref, sbd_ref) = rest[len(_WEIGHT_NAMES):]
    t = pl.program_id(1)
    rows = n_seq * n_tok

    @pl.when(t == 0)
    def _():
        extp_ref[:, 0:POOL_PAD, :] = jnp.zeros((n_seq, POOL_PAD, D_MODEL), _F32)
        extc_ref[:, 0:CONV_PAD, :] = jnp.zeros((n_seq, CONV_PAD, 3 * D_MODEL), _F32)
        sbd_ref[...] = jnp.zeros(sbd_ref.shape, _F32)

    x3 = x_ref[...]
    mod3 = mod_ref[...]
    pos = t * n_tok + lax.broadcasted_iota(jnp.int32, (rows, 1), 0) % n_tok
    f = _front(x3, mod3, pos, refs, extp_ref, extc_ref)
    chunks = _chunks_intra(f["q"], f["k"], f["v"], f["beta_e"], f["beta_5"], f["gc_e"], f["gc_5"], n_tok)

    pair = [slice(p * 2 * HEAD, (p + 1) * 2 * HEAD) for p in range(N_PAIRS)]
    row_bd = lax.broadcasted_iota(jnp.int32, (2 * HEAD, 2 * HEAD), 0) // HEAD
    col_bd = lax.broadcasted_iota(jnp.int32, (2 * HEAD, 2 * HEAD), 1) // HEAD
    same_head = row_bd == col_bd
    us, ois = [], []
    for c, ck in enumerate(chunks):
        u_parts, oi_parts = [], []
        for p, sl in enumerate(pair):
            r = _dot(jnp.concatenate([ck["w_k"][:, sl], ck["qg"][:, sl]], axis=0), sbd_ref[c * N_PAIRS + p])
            u_parts.append(ck["w_v"][:, sl] - r[:CHUNK])
            oi_parts.append(r[CHUNK:])
        us.append(jnp.concatenate(u_parts, axis=1))
        ois.append(jnp.concatenate(oi_parts, axis=1))
    o = jnp.concatenate([ois[c] + _intra_output(ck["qkd"], us[c]) for c, ck in enumerate(chunks)], axis=0)
    for c, ck in enumerate(chunks):
        for p, sl in enumerate(pair):
            upd = _dot(ck["kt"][:, sl].T, us[c][:, sl])
            i = c * N_PAIRS + p
            sbd_ref[i] = ck["gl"][0:1, sl] * sbd_ref[i] + jnp.where(same_head, upd, 0.0)

    gate = jnp.broadcast_to(mod3[:, :, 2 * D_MODEL:], (n_seq, n_tok, D_MODEL)).reshape(rows, D_MODEL)
    y = _back(x3.reshape(rows, D_MODEL), o, f, gate, refs)
    y_ref[...] = y.reshape(n_seq, n_tok, D_MODEL)

    @pl.when(t == n_steps - 1)
    def _():
        pool_out_ref[...] = extp_ref[:, pl.ds(POOL_PAD + n_tok - POOL_BUF, POOL_BUF), :]
        conv_out_ref[...] = extc_ref[:, pl.ds(CONV_PAD + n_tok - (CONV_W - 1), CONV_W - 1), :]
        for s in range(n_seq):
            for hd in range(N_HEADS):
                o0 = (hd % 2) * HEAD
                delta_out_ref[s, hd] = sbd_ref[s * N_PAIRS + hd // 2, o0:o0 + HEAD, o0:o0 + HEAD]

    @pl.when(t < n_steps - 1)
    def _():
        extp_ref[:, 0:POOL_PAD, :] = extp_ref[:, n_tok:n_tok + POOL_PAD, :]
        extc_ref[:, 0:CONV_PAD, :] = extc_ref[:, n_tok:n_tok + CONV_PAD, :]


def _const_spec(shape):
    nd = len(shape)
    return pl.BlockSpec(shape, lambda *_: (0,) * nd, pipeline_mode=pl.Buffered(1))


def _weight_specs(weights):
    return [_const_spec(w.shape) for w in weights]


def _prompt_call(x, mod, weights, n_seq, n_tok):
    bsz, seq, _ = x.shape
    n_steps = seq // n_tok
    assert n_tok == CHUNK and bsz % n_seq == 0 and seq % n_tok == 0
    kern = functools.partial(_prompt_kernel, n_seq=n_seq, n_tok=n_tok, n_steps=n_steps)
    seq_block = lambda shape: pl.BlockSpec((n_seq,) + shape, lambda g, t: (g,) + (0,) * len(shape))
    return pl.pallas_call(
        kern,
        grid=(bsz // n_seq, n_steps),
        in_specs=[pl.BlockSpec((n_seq, n_tok, D_MODEL), lambda g, t: (g, t, 0)),
                  seq_block((1, 3 * D_MODEL))] + _weight_specs(weights),
        out_specs=[pl.BlockSpec((n_seq, n_tok, D_MODEL), lambda g, t: (g, t, 0)),
                   seq_block((POOL_BUF, D_MODEL)), seq_block((CONV_W - 1, 3 * D_MODEL)),
                   seq_block((N_HEADS, HEAD, HEAD))],
        out_shape=[jax.ShapeDtypeStruct((bsz, seq, D_MODEL), _F32),
                   jax.ShapeDtypeStruct((bsz, POOL_BUF, D_MODEL), _F32),
                   jax.ShapeDtypeStruct((bsz, CONV_W - 1, 3 * D_MODEL), _F32),
                   jax.ShapeDtypeStruct((bsz, N_HEADS, HEAD, HEAD), _F32)],
        scratch_shapes=[pltpu.VMEM((n_seq, POOL_PAD + n_tok, D_MODEL), _F32),
                        pltpu.VMEM((n_seq, CONV_PAD + n_tok, 3 * D_MODEL), _F32),
                        pltpu.VMEM((n_seq * N_PAIRS, 2 * HEAD, 2 * HEAD), _F32)],
        compiler_params=pltpu.CompilerParams(dimension_semantics=("arbitrary", "arbitrary"),
                                             vmem_limit_bytes=VMEM_LIMIT_BYTES),
        name="prompt_layer",
    )(x, mod, *weights)


def _sample_kernel(x_ref, mod_ref, spool_ref, sconv_ref, sdelta_ref, *rest, n_seq, n_tok, pos0):
    refs = dict(zip(_WEIGHT_NAMES, rest[:len(_WEIGHT_NAMES)]))
    (y_ref, pool_out_ref, conv_out_ref, delta_out_ref, extp_ref, extc_ref) = rest[len(_WEIGHT_NAMES):]
    rows = n_seq * n_tok

    extp_ref[:, POOL_PAD - POOL_BUF:POOL_PAD, :] = spool_ref[...]
    extp_ref[:, 0:POOL_PAD - POOL_BUF, :] = jnp.zeros((n_seq, POOL_PAD - POOL_BUF, D_MODEL), _F32)
    extc_ref[:, CONV_PAD - (CONV_W - 1):CONV_PAD, :] = sconv_ref[...]
    extc_ref[:, 0:CONV_PAD - (CONV_W - 1), :] = jnp.zeros((n_seq, CONV_PAD - (CONV_W - 1), 3 * D_MODEL), _F32)

    x3 = x_ref[...]
    mod3 = mod_ref[...]
    pos = pos0 + lax.broadcasted_iota(jnp.int32, (rows, 1), 0) % n_tok
    f = _front(x3, mod3, pos, refs, extp_ref, extc_ref)
    pool_out_ref[...] = extp_ref[:, pl.ds(POOL_PAD + n_tok - POOL_BUF, POOL_BUF), :]
    conv_out_ref[...] = extc_ref[:, pl.ds(CONV_PAD + n_tok - (CONV_W - 1), CONV_W - 1), :]
    (ck,) = _chunks_intra(f["q"], f["k"], f["v"], f["beta_e"], f["beta_5"], f["gc_e"], f["gc_5"], n_tok)

    u_rows, oi_rows = [], []
    for s in range(n_seq):
        rs = slice(s * n_tok, (s + 1) * n_tok)
        u_h, oi_h = [], []
        for hd in range(N_HEADS):
            cs = slice(hd * HEAD, (hd + 1) * HEAD)
            r = _dot(jnp.concatenate([ck["w_k"][rs, cs], ck["qg"][rs, cs]], axis=0), sdelta_ref[s, hd])
            u_h.append(ck["w_v"][rs, cs] - r[:n_tok])
            oi_h.append(r[n_tok:])
        u_rows.append(jnp.concatenate(u_h, axis=1))
        oi_rows.append(jnp.concatenate(oi_h, axis=1))
    u = jnp.concatenate(u_rows, axis=0)
    o = jnp.concatenate(oi_rows, axis=0) + _intra_output(ck["qkd"], u)

    lane_seq = lax.broadcasted_iota(jnp.int32, (HEAD, rows), 1) // n_tok
    for hd in range(N_HEADS):
        cs = slice(hd * HEAD, (hd + 1) * HEAD)
        kt_t = ck["kt"][:, cs].T
        lhs = jnp.concatenate([jnp.where(lane_seq == s, kt_t, 0.0) for s in range(n_seq)], axis=0)
        upd = _dot(lhs, u[:, cs])
        for s in range(n_seq):
            gl = ck["gl"][s * n_tok:s * n_tok + 1, cs]
            delta_out_ref[s, hd] = gl * sdelta_ref[s, hd] + upd[s * HEAD:(s + 1) * HEAD]

    gate = jnp.broadcast_to(mod3[:, :, 2 * D_MODEL:], (n_seq, n_tok, D_MODEL)).reshape(rows, D_MODEL)
    y = _back(x3.reshape(rows, D_MODEL), o, f, gate, refs)
    y_ref[...] = y.reshape(n_seq, n_tok, D_MODEL)


def _sample_call(x, mod, spool, sconv, sdelta, weights, pos0):
    bsz, n_tok, _ = x.shape
    n_seq = SAMPLE_SEQS
    assert n_seq * n_tok == CHUNK and bsz % n_seq == 0
    kern = functools.partial(_sample_kernel, n_seq=n_seq, n_tok=n_tok, pos0=pos0)
    seq_block = lambda shape: pl.BlockSpec((n_seq,) + shape, lambda i: (i,) + (0,) * len(shape))
    return pl.pallas_call(
        kern,
        grid=(bsz // n_seq,),
        in_specs=[seq_block((n_tok, D_MODEL)), seq_block((1, 3 * D_MODEL)),
                  seq_block((POOL_BUF, D_MODEL)), seq_block((CONV_W - 1, 3 * D_MODEL)),
                  seq_block((N_HEADS, HEAD, HEAD))] + _weight_specs(weights),
        out_specs=[seq_block((n_tok, D_MODEL)), seq_block((POOL_BUF, D_MODEL)),
                   seq_block((CONV_W - 1, 3 * D_MODEL)), seq_block((N_HEADS, HEAD, HEAD))],
        out_shape=[jax.ShapeDtypeStruct((bsz, n_tok, D_MODEL), _F32),
                   jax.ShapeDtypeStruct((bsz, POOL_BUF, D_MODEL), _F32),
                   jax.ShapeDtypeStruct((bsz, CONV_W - 1, 3 * D_MODEL), _F32),
                   jax.ShapeDtypeStruct((bsz, N_HEADS, HEAD, HEAD), _F32)],
        scratch_shapes=[pltpu.VMEM((n_seq, POOL_PAD + n_tok, D_MODEL), _F32),
                        pltpu.VMEM((n_seq, CONV_PAD + n_tok, 3 * D_MODEL), _F32)],
        compiler_params=pltpu.CompilerParams(dimension_semantics=("arbitrary",),
                                             vmem_limit_bytes=VMEM_LIMIT_BYTES),
        name="sample_layer",
    )(x, mod, spool, sconv, sdelta, *weights)


def _head_expansion_matrix():
    e = np.zeros((HEAD, 3072), np.float32)
    for piece in range(3):
        for hd in range(N_HEADS):
            rb, rg = piece * 2 * N_HEADS + hd, piece * 2 * N_HEADS + N_HEADS + hd
            e[rb, hd * HEAD:(hd + 1) * HEAD] = 1.0
            e[rb, 1024 + hd * CHUNK:1024 + (hd + 1) * CHUNK] = 1.0
            e[rg, 1536 + hd * HEAD:1536 + (hd + 1) * HEAD] = 1.0
            e[rg, 2560 + hd * CHUNK:2560 + (hd + 1) * CHUNK] = 1.0
    return jnp.asarray(e, _BF)


def _layer_weights(w_in, conv_w, a_log, dt_bias, head_norm_w, pool_w, pool_scale, p_a, p_b, w_out, ln_g, ln_b):
    w_ba = jnp.zeros((D_MODEL, HEAD), _BF).at[:, 0:2 * N_HEADS].set(w_in[:, OFF_MAIN_END:OFF_GATE].astype(_BF))
    lane_pad = lambda a: jnp.zeros((1, HEAD), _F32).at[0, N_HEADS:2 * N_HEADS].set(a)
    return (w_in[:, :OFF_MAIN_END].astype(_BF), w_ba, w_in[:, OFF_GATE:].astype(_BF), conv_w,
            lane_pad(a_log), lane_pad(dt_bias), head_norm_w.reshape(1, HEAD), pool_w.astype(_BF),
            pool_scale.reshape(1, D_MODEL), p_a.astype(_BF), p_b.astype(_BF), w_out.astype(_BF),
            ln_g.reshape(1, D_MODEL), ln_b.reshape(1, D_MODEL), _head_expansion_matrix())


def kernel(x_prompt, x_sample, state_pool, state_conv, state_delta, c_prompt, c_sample, w_ada, b_ada, w_in, conv_w, a_log, dt_bias, head_norm_w, pool_w, pool_scale, p_a, p_b, w_out, ln_g, ln_b):
    depth = w_in.shape[0]
    assert depth == 1, "single-layer trunk"
    bp = x_prompt.shape[0]
    past_len = 16384
    hp, hs = x_prompt, x_sample
    outs = []
    for l in range(depth):
        mod = _ada_call(jnp.concatenate([c_prompt, c_sample], axis=0), w_ada[l], b_ada[l].reshape(1, -1))
        mod = mod.reshape(mod.shape[0], 1, 3 * D_MODEL)
        weights = _layer_weights(w_in[l], conv_w[l], a_log[l], dt_bias[l], head_norm_w[l], pool_w[l],
                                 pool_scale[l], p_a[l], p_b[l], w_out[l], ln_g[l], ln_b[l])
        hp, pool_p, conv_p, delta_p = _prompt_call(hp, mod[:bp], weights, PROMPT_SEQS, PROMPT_TOKENS)
        hs, pool_s, conv_s, delta_s = _sample_call(hs, mod[bp:], state_pool[l], state_conv[l], state_delta[l],
                                                   weights, past_len)
        outs.append((pool_p, conv_p, delta_p, pool_s, conv_s, delta_s))
    stack = lambda i: jnp.stack([o[i] for o in outs])
    return (hp, hs, stack(0), stack(1), stack(2), stack(3), stack(4), stack(5))
```

```python
import functools

import numpy as np
import jax
import jax.numpy as jnp
from jax import lax
from jax.experimental import pallas as pl
from jax.experimental.pallas import tpu as pltpu

D_MODEL = 1024
N_HEADS = 8
HEAD = 128
N_PAIRS = N_HEADS // 2
POOL_WINDOWS = (2, 4, 8, 16)
POOL_GROUP = 256
POOL_BUF = 15
POOL_PAD = 16
CONV_W = 4
CONV_PAD = 8
CHUNK = 64
DEEPNORM_ALPHA = 2.0 ** 0.25
LN_EPS = 1e-5
RMS_EPS = 1e-6
L2_EPS = 1e-6
NEG_BIG = -1e30

OFF_MAIN_END = 6 * D_MODEL
OFF_GATE = OFF_MAIN_END + 2 * N_HEADS

PROMPT_SEQS = 4
PROMPT_TOKENS = CHUNK
SAMPLE_SEQS = 8
VMEM_LIMIT_BYTES = 58 * 1024 * 1024

_BF = jnp.bfloat16
_F32 = jnp.float32


def _dot(a, b):
    return jnp.dot(a.astype(_BF), b.astype(_BF), preferred_element_type=_F32)


def _dot_nt(a, b):
    return lax.dot_general(a.astype(_BF), b.astype(_BF), (((1,), (1,)), ((), ())),
                           preferred_element_type=_F32)


def _sigmoid(x):
    return 0.5 + 0.5 * jnp.tanh(0.5 * x)


def _silu(x):
    hx = 0.5 * x
    return hx + hx * jnp.tanh(hx)


def _softplus(x):
    return jnp.maximum(x, 0.0) + jnp.log1p(jnp.exp(-jnp.abs(x)))


def _lane_block_diag(x, width):
    nblk = x.shape[1] // width
    lane = lax.broadcasted_iota(jnp.int32, x.shape, 1)
    return jnp.concatenate([jnp.where(lane // width == i, x, 0.0) for i in range(nblk)], axis=0)


def _pair_block_diag(x):
    lane = lax.broadcasted_iota(jnp.int32, x.shape, 1) % (2 * HEAD)
    return jnp.concatenate([jnp.where(lane < HEAD, x, 0.0), jnp.where(lane >= HEAD, x, 0.0)], axis=0)


def _segment_cumsum(x, seg):
    row = lax.broadcasted_iota(jnp.int32, x.shape, 0) % seg
    shift = 1
    while shift < seg:
        x = x + jnp.where(row >= shift, pltpu.roll(x, shift, axis=0), 0.0)
        shift *= 2
    return x


def _expand_heads(narrow, e3_ref):
    lane = lax.broadcasted_iota(jnp.int32, narrow.shape, 1)
    x = jnp.where(lane < 2 * N_HEADS, narrow, 0.0)
    hi = x.astype(_BF).astype(_F32)
    rem = x - hi
    mid = rem.astype(_BF).astype(_F32)
    lo = (rem - mid).astype(_BF).astype(_F32)
    packed = hi + pltpu.roll(mid, 2 * N_HEADS, axis=1) + pltpu.roll(lo, 4 * N_HEADS, axis=1)
    wide = jnp.dot(packed.astype(_BF), e3_ref[...], preferred_element_type=_F32)
    return wide[:, 0:1024], wide[:, 1024:1536], wide[:, 1536:2560], wide[:, 2560:3072]


def _per_head_rsqrt_scale(x, eps, mean, post):
    outs = []
    for h in range(N_HEADS):
        xh = x[:, h * HEAD:(h + 1) * HEAD]
        ss = jnp.sum(xh * xh, axis=-1, keepdims=True)
        if mean:
            ss = ss * (1.0 / HEAD)
        outs.append(xh * (lax.rsqrt(ss + eps) * post))
    return jnp.concatenate(outs, axis=1)


def _gate_scalars(ba, alog_ref, dtb_ref, seg):
    beta = _sigmoid(ba)
    g = -jnp.exp(alog_ref[...]) * _softplus(ba + dtb_ref[...])
    gc = _segment_cumsum(g, seg)
    lane = lax.broadcasted_iota(jnp.int32, ba.shape, 1)
    return jnp.where(lane < N_HEADS, beta, gc)


def _chunk_masks(n_tok):
    row = lax.broadcasted_iota(jnp.int32, (CHUNK, 4 * HEAD), 0)
    col = lax.broadcasted_iota(jnp.int32, (CHUNK, 4 * HEAD), 1) % CHUNK
    same_seq = (row // n_tok) == (col // n_tok)
    return same_seq & (row >= col), same_seq & (row > col), row == col


def _chunks_intra(q, k, v, beta_e, beta_5, gc_e, gc_5, n_tok):
    n_chunks = q.shape[0] // CHUNK
    incl, strict, eye5 = _chunk_masks(n_tok)
    rows = [slice(c * CHUNK, (c + 1) * CHUNK) for c in range(n_chunks)]
    pair = [slice(p * 2 * HEAD, (p + 1) * 2 * HEAD) for p in range(N_PAIRS)]

    kk, qk = [], []
    for rs in rows:
        kk_parts, qk_parts = [], []
        for sl in pair:
            kp, qp = k[rs, sl], q[rs, sl]
            g2 = _dot_nt(jnp.concatenate([kp, qp], axis=0), _pair_block_diag(kp))
            kk_parts.append(g2[:CHUNK])
            qk_parts.append(g2[CHUNK:])
        kk.append(jnp.concatenate(kk_parts, axis=1))
        qk.append(jnp.concatenate(qk_parts, axis=1))

    dec, glast = [], []
    for rs in rows:
        g5 = gc_5[rs]
        gc_row = jnp.sum(jnp.where(eye5, g5, 0.0), axis=0, keepdims=True)
        dec.append(jnp.exp(jnp.where(incl, g5 - gc_row, NEG_BIG)))
        ge = gc_e[rs]
        glast.append(jnp.concatenate(
            [jnp.broadcast_to(ge[(s + 1) * n_tok - 1:(s + 1) * n_tok, :], (n_tok, D_MODEL))
             for s in range(CHUNK // n_tok)], axis=0))

    prob = [(c, grp) for c in range(n_chunks) for grp in range(2)]
    m, s, pw = {}, {}, {}
    for c, grp in prob:
        neg_l = jnp.where(strict[:, :256], -(beta_5[rows[c], grp * 256:(grp + 1) * 256]
                                              * dec[c][:, grp * 256:(grp + 1) * 256]
                                              * kk[c][:, grp * 256:(grp + 1) * 256]), 0.0)
        m[c, grp] = neg_l
        s[c, grp] = eye5[:, :256].astype(_F32) + neg_l
    for key in prob:
        pw[key] = _dot(m[key], _lane_block_diag(m[key], CHUNK))
    for _ in range(4):
        for key in prob:
            r = _dot(jnp.concatenate([pw[key], s[key]], axis=0), _lane_block_diag(pw[key], CHUNK))
            pw[key] = r[:CHUNK]
            s[key] = s[key] + r[CHUNK:]
    for key in prob:
        s[key] = s[key] + _dot(s[key], _lane_block_diag(pw[key], CHUNK))

    out = []
    for c, rs in enumerate(rows):
        gam = jnp.exp(gc_e[rs])
        be = beta_e[rs]
        bv = be * v[rs]
        gbk = be * gam * k[rs]
        wv_parts, wk_parts = [], []
        for p, sl in enumerate(pair):
            t_pair = s[c, p // 2][:, (p % 2) * HEAD:(p % 2 + 1) * HEAD]
            sol = _dot(t_pair, _pair_block_diag(jnp.concatenate([bv[:, sl], gbk[:, sl]], axis=1)))
            wv_parts.append(sol[:, :2 * HEAD])
            wk_parts.append(sol[:, 2 * HEAD:])
        out.append(dict(w_v=jnp.concatenate(wv_parts, axis=1), w_k=jnp.concatenate(wk_parts, axis=1),
                        qg=q[rs] * gam, kt=k[rs] * jnp.exp(glast[c] - gc_e[rs]),
                        qkd=qk[c] * dec[c], gl=jnp.exp(glast[c])))
    return out


def _intra_output(qkd, u):
    outs = []
    for p in range(N_PAIRS):
        up = u[:, p * 2 * HEAD:(p + 1) * 2 * HEAD]
        outs.append(_dot(qkd[:, p * HEAD:(p + 1) * HEAD], _pair_block_diag(up)))
    return jnp.concatenate(outs, axis=1)


def _front(x3, mod3, pos, refs, extp_ref, extc_ref):
    n_seq, n_tok, _ = x3.shape
    rows = n_seq * n_tok
    shift, scale = mod3[:, :, 0:D_MODEL], mod3[:, :, D_MODEL:2 * D_MODEL]
    h = (x3 * (1.0 + scale) + shift).reshape(rows, D_MODEL).astype(_BF)
    wmain_ref = refs["wmain"]
    proj = lambda c0, c1: jnp.dot(h, wmain_ref[:, c0:c1], preferred_element_type=_F32)

    extc_ref[:, CONV_PAD:CONV_PAD + n_tok, :] = proj(2 * D_MODEL, 5 * D_MODEL).reshape(n_seq, n_tok, 3 * D_MODEL)

    seg_c = CONV_PAD + n_tok
    convw_ref = refs["convw"]
    e = extc_ref[...].reshape(n_seq * seg_c, 3 * D_MODEL)
    acc = e * convw_ref[0:1, :]
    for i in range(1, CONV_W):
        acc = pltpu.roll(acc, 1, axis=0) + e * convw_ref[i:i + 1, :]
    qkv = _silu(acc.reshape(n_seq, seg_c, 3 * D_MODEL)[:, CONV_PAD:, :].reshape(rows, 3 * D_MODEL))
    q = _per_head_rsqrt_scale(qkv[:, 0:D_MODEL], L2_EPS, False, HEAD ** -0.5)
    k = _per_head_rsqrt_scale(qkv[:, D_MODEL:2 * D_MODEL], L2_EPS, False, 1.0)
    v = qkv[:, 2 * D_MODEL:]

    extp_ref[:, POOL_PAD:POOL_PAD + n_tok, :] = proj(0, D_MODEL).reshape(n_seq, n_tok, D_MODEL)
    z_a = proj(D_MODEL, 2 * D_MODEL)
    ba = jnp.dot(h, refs["wba"][...], preferred_element_type=_F32)
    z_b = proj(5 * D_MODEL, 6 * D_MODEL)
    ga = jnp.dot(h, refs["wg"][:, 0:D_MODEL], preferred_element_type=_F32)
    gb = jnp.dot(h, refs["wg"][:, D_MODEL:], preferred_element_type=_F32)

    seg_p = POOL_PAD + n_tok
    pooled = []
    for gi, w in enumerate(POOL_WINDOWS):
        e = extp_ref[:, :, gi * POOL_GROUP:(gi + 1) * POOL_GROUP].reshape(n_seq * seg_p, POOL_GROUP)
        acc, span = e, 1
        while span < w:
            acc = acc + pltpu.roll(acc, span, axis=0)
            span *= 2
        take = lambda a: a.reshape(n_seq, seg_p, POOL_GROUP)[:, POOL_PAD:, :].reshape(rows, POOL_GROUP)
        inv_cnt = 1.0 / jnp.minimum(pos + 1, w).astype(_F32)
        pooled.append(take(acc) * inv_cnt - take(e))
    mixed = jnp.concatenate([_dot(pp, refs["poolw"][gi]) for gi, pp in enumerate(pooled)], axis=1)
    y_a = (mixed * refs["pscale"][...] * _silu(z_a)).astype(_BF)
    a_proj = _dot(y_a, refs["pa"][...])

    beta_e, beta_5, gc_e, gc_5 = _expand_heads(_gate_scalars(ba, refs["alog"], refs["dtb"], n_tok), refs["e3"])
    return dict(q=q, k=k, v=v, beta_e=beta_e, beta_5=beta_5, gc_e=gc_e, gc_5=gc_5,
                a_proj=a_proj, z_b=z_b, ga=ga, gb=gb)


def _back(x, o, f, gate, refs):
    o = _per_head_rsqrt_scale(o, RMS_EPS, True, refs["hnw"][...])
    y_b = (o * _silu(f["z_b"])).astype(_BF)
    merged = _sigmoid(f["ga"]) * f["a_proj"] + _sigmoid(f["gb"]) * _dot(y_b, refs["pb"][...])
    sub = (1.0 + gate) * _dot(merged, refs["wout"][...])
    r = DEEPNORM_ALPHA * x + sub
    mu = jnp.mean(r, axis=-1, keepdims=True)
    rc = r - mu
    var = jnp.mean(rc * rc, axis=-1, keepdims=True)
    return rc * lax.rsqrt(var + LN_EPS) * refs["lng"][...] + refs["lnb"][...]


_WEIGHT_NAMES = ("wmain", "wba", "wg", "convw", "alog", "dtb", "hnw", "poolw", "pscale", "pa", "pb",
                 "wout", "lng", "lnb", "e3")


def _ada_kernel(c_ref, w_ref, b_ref, o_ref):
    o_ref[...] = _dot(_silu(c_ref[...]), w_ref[...]) + b_ref[...]


def _ada_call(c_all, w_ada, b_ada):
    n = c_all.shape[0]
    return pl.pallas_call(
        _ada_kernel,
        grid=(3,),
        in_specs=[pl.BlockSpec((n, D_MODEL), lambda j: (0, 0)),
                  pl.BlockSpec((D_MODEL, D_MODEL), lambda j: (0, j)),
                  pl.BlockSpec((1, D_MODEL), lambda j: (0, j))],
        out_specs=pl.BlockSpec((n, D_MODEL), lambda j: (0, j)),
        out_shape=jax.ShapeDtypeStruct((n, 3 * D_MODEL), _F32),
        compiler_params=pltpu.CompilerParams(dimension_semantics=("arbitrary",)),
        name="adaln_mod",
    )(c_all, w_ada, b_ada)


def _prompt_kernel(x_ref, mod_ref, *rest, n_seq, n_tok, n_steps):
    refs = dict(zip(_WEIGHT_NAMES, rest[:len(_WEIGHT_NAMES)]))
    (y_ref, pool_out_ref, conv_out_ref, delta_out_ref,
     extp_ref, extc_ref, sbd_ref) = rest[len(_WEIGHT_NAMES):]
    t = pl.program_id(1)
    rows = n_seq * n_tok

    @pl.when(t == 0)
    def _():
        extp_ref[:, 0:POOL_PAD, :] = jnp.zeros((n_seq, POOL_PAD, D_MODEL), _F32)
        extc_ref[:, 0:CONV_PAD, :] = jnp.zeros((n_seq, CONV_PAD, 3 * D_MODEL), _F32)
        sbd_ref[...] = jnp.zeros(sbd_ref.shape, _F32)

    x3 = x_ref[...]
    mod3 = mod_ref[...]
    pos = t * n_tok + lax.broadcasted_iota(jnp.int32, (rows, 1), 0) % n_tok
    f = _front(x3, mod3, pos, refs, extp_ref, extc_ref)
    chunks = _chunks_intra(f["q"], f["k"], f["v"], f["beta_e"], f["beta_5"], f["gc_e"], f["gc_5"], n_tok)

    pair = [slice(p * 2 * HEAD, (p + 1) * 2 * HEAD) for p in range(N_PAIRS)]
    row_bd = lax.broadcasted_iota(jnp.int32, (2 * HEAD, 2 * HEAD), 0) // HEAD
    col_bd = lax.broadcasted_iota(jnp.int32, (2 * HEAD, 2 * HEAD), 1) // HEAD
    same_head = row_bd == col_bd
    us, ois = [], []
    for c, ck in enumerate(chunks):
        u_parts, oi_parts = [], []
        for p, sl in enumerate(pair):
            r = _dot(jnp.concatenate([ck["w_k"][:, sl], ck["qg"][:, sl]], axis=0), sbd_ref[c * N_PAIRS + p])
            u_parts.append(ck["w_v"][:, sl] - r[:CHUNK])
            oi_parts.append(r[CHUNK:])
        us.append(jnp.concatenate(u_parts, axis=1))
        ois.append(jnp.concatenate(oi_parts, axis=1))
    o = jnp.concatenate([ois[c] + _intra_output(ck["qkd"], us[c]) for c, ck in enumerate(chunks)], axis=0)
    for c, ck in enumerate(chunks):
        for p, sl in enumerate(pair):
            upd = _dot(ck["kt"][:, sl].T, us[c][:, sl])
            i = c * N_PAIRS + p
            sbd_ref[i] = ck["gl"][0:1, sl] * sbd_ref[i] + jnp.where(same_head, upd, 0.0)

    gate = jnp.broadcast_to(mod3[:, :, 2 * D_MODEL:], (n_seq, n_tok, D_MODEL)).reshape(rows, D_MODEL)
    y = _back(x3.reshape(rows, D_MODEL), o, f, gate, refs)
    y_ref[...] = y.reshape(n_seq, n_tok, D_MODEL)

    @pl.when(t == n_steps - 1)
    def _():
        pool_out_ref[...] = extp_ref[:, pl.ds(POOL_PAD + n_tok - POOL_BUF, POOL_BUF), :]
        conv_out_ref[...] = extc_ref[:, pl.ds(CONV_PAD + n_tok - (CONV_W - 1), CONV_W - 1), :]
        for s in range(n_seq):
            for hd in range(N_HEADS):
                o0 = (hd % 2) * HEAD
                delta_out_ref[s, hd] = sbd_ref[s * N_PAIRS + hd // 2, o0:o0 + HEAD, o0:o0 + HEAD]

    @pl.when(t < n_steps - 1)
    def _():
        extp_ref[:, 0:POOL_PAD, :] = extp_ref[:, n_tok:n_tok + POOL_PAD, :]
        extc_ref[:, 0:CONV_PAD, :] = extc_ref[:, n_tok:n_tok + CONV_PAD, :]


def _const_spec(shape, index=None):
    index = (0,) * len(shape) if index is None else index
    return pl.BlockSpec(shape, lambda *_: index, pipeline_mode=pl.Buffered(1))


def _weight_specs(weights):
    specs = [_const_spec((D_MODEL, OFF_MAIN_END)), _const_spec((D_MODEL, HEAD), (0, OFF_MAIN_END // HEAD))]
    return specs + [_const_spec(w.shape) for w in weights[2:]]


def _prompt_call(x, mod, weights, n_seq, n_tok):
    bsz, seq, _ = x.shape
    n_steps = seq // n_tok
    assert n_tok == CHUNK and bsz % n_seq == 0 and seq % n_tok == 0
    kern = functools.partial(_prompt_kernel, n_seq=n_seq, n_tok=n_tok, n_steps=n_steps)
    seq_block = lambda shape: pl.BlockSpec((n_seq,) + shape, lambda g, t: (g,) + (0,) * len(shape))
    return pl.pallas_call(
        kern,
        grid=(bsz // n_seq, n_steps),
        in_specs=[pl.BlockSpec((n_seq, n_tok, D_MODEL), lambda g, t: (g, t, 0)),
                  seq_block((1, 3 * D_MODEL))] + _weight_specs(weights),
        out_specs=[pl.BlockSpec((n_seq, n_tok, D_MODEL), lambda g, t: (g, t, 0)),
                   seq_block((POOL_BUF, D_MODEL)), seq_block((CONV_W - 1, 3 * D_MODEL)),
                   seq_block((N_HEADS, HEAD, HEAD))],
        out_shape=[jax.ShapeDtypeStruct((bsz, seq, D_MODEL), _F32),
                   jax.ShapeDtypeStruct((bsz, POOL_BUF, D_MODEL), _F32),
                   jax.ShapeDtypeStruct((bsz, CONV_W - 1, 3 * D_MODEL), _F32),
                   jax.ShapeDtypeStruct((bsz, N_HEADS, HEAD, HEAD), _F32)],
        scratch_shapes=[pltpu.VMEM((n_seq, POOL_PAD + n_tok, D_MODEL), _F32),
                        pltpu.VMEM((n_seq, CONV_PAD + n_tok, 3 * D_MODEL), _F32),
                        pltpu.VMEM((n_seq * N_PAIRS, 2 * HEAD, 2 * HEAD), _F32)],
        compiler_params=pltpu.CompilerParams(dimension_semantics=("arbitrary", "arbitrary"),
                                             vmem_limit_bytes=VMEM_LIMIT_BYTES),
        name="prompt_layer",
    )(x, mod, *weights)


def _sample_kernel(x_ref, mod_ref, spool_ref, sconv_ref, sdelta_ref, *rest, n_seq, n_tok, pos0):
    refs = dict(zip(_WEIGHT_NAMES, rest[:len(_WEIGHT_NAMES)]))
    (y_ref, pool_out_ref, conv_out_ref, delta_out_ref, extp_ref, extc_ref) = rest[len(_WEIGHT_NAMES):]
    rows = n_seq * n_tok

    extp_ref[:, POOL_PAD - POOL_BUF:POOL_PAD, :] = spool_ref[...]
    extp_ref[:, 0:POOL_PAD - POOL_BUF, :] = jnp.zeros((n_seq, POOL_PAD - POOL_BUF, D_MODEL), _F32)
    extc_ref[:, CONV_PAD - (CONV_W - 1):CONV_PAD, :] = sconv_ref[...]
    extc_ref[:, 0:CONV_PAD - (CONV_W - 1), :] = jnp.zeros((n_seq, CONV_PAD - (CONV_W - 1), 3 * D_MODEL), _F32)

    x3 = x_ref[...]
    mod3 = mod_ref[...]
    pos = pos0 + lax.broadcasted_iota(jnp.int32, (rows, 1), 0) % n_tok
    f = _front(x3, mod3, pos, refs, extp_ref, extc_ref)
    pool_out_ref[...] = extp_ref[:, pl.ds(POOL_PAD + n_tok - POOL_BUF, POOL_BUF), :]
    conv_out_ref[...] = extc_ref[:, pl.ds(CONV_PAD + n_tok - (CONV_W - 1), CONV_W - 1), :]
    (ck,) = _chunks_intra(f["q"], f["k"], f["v"], f["beta_e"], f["beta_5"], f["gc_e"], f["gc_5"], n_tok)

    u_rows, oi_rows = [], []
    for s in range(n_seq):
        rs = slice(s * n_tok, (s + 1) * n_tok)
        u_h, oi_h = [], []
        for hd in range(N_HEADS):
            cs = slice(hd * HEAD, (hd + 1) * HEAD)
            r = _dot(jnp.concatenate([ck["w_k"][rs, cs], ck["qg"][rs, cs]], axis=0), sdelta_ref[s, hd])
            u_h.append(ck["w_v"][rs, cs] - r[:n_tok])
            oi_h.append(r[n_tok:])
        u_rows.append(jnp.concatenate(u_h, axis=1))
        oi_rows.append(jnp.concatenate(oi_h, axis=1))
    u = jnp.concatenate(u_rows, axis=0)
    o = jnp.concatenate(oi_rows, axis=0) + _intra_output(ck["qkd"], u)

    lane_seq = lax.broadcasted_iota(jnp.int32, (HEAD, rows), 1) // n_tok
    for hd in range(N_HEADS):
        cs = slice(hd * HEAD, (hd + 1) * HEAD)
        kt_t = ck["kt"][:, cs].T
        lhs = jnp.concatenate([jnp.where(lane_seq == s, kt_t, 0.0) for s in range(n_seq)], axis=0)
        upd = _dot(lhs, u[:, cs])
        for s in range(n_seq):
            gl = ck["gl"][s * n_tok:s * n_tok + 1, cs]
            delta_out_ref[s, hd] = gl * sdelta_ref[s, hd] + upd[s * HEAD:(s + 1) * HEAD]

    gate = jnp.broadcast_to(mod3[:, :, 2 * D_MODEL:], (n_seq, n_tok, D_MODEL)).reshape(rows, D_MODEL)
    y = _back(x3.reshape(rows, D_MODEL), o, f, gate, refs)
    y_ref[...] = y.reshape(n_seq, n_tok, D_MODEL)


def _sample_call(x, mod, spool, sconv, sdelta, weights, pos0):
    bsz, n_tok, _ = x.shape
    n_seq = SAMPLE_SEQS
    assert n_seq * n_tok == CHUNK and bsz % n_seq == 0
    kern = functools.partial(_sample_kernel, n_seq=n_seq, n_tok=n_tok, pos0=pos0)
    seq_block = lambda shape: pl.BlockSpec((n_seq,) + shape, lambda i: (i,) + (0,) * len(shape))
    return pl.pallas_call(
        kern,
        grid=(bsz // n_seq,),
        in_specs=[seq_block((n_tok, D_MODEL)), seq_block((1, 3 * D_MODEL)),
                  seq_block((POOL_BUF, D_MODEL)), seq_block((CONV_W - 1, 3 * D_MODEL)),
                  seq_block((N_HEADS, HEAD, HEAD))] + _weight_specs(weights),
        out_specs=[seq_block((n_tok, D_MODEL)), seq_block((POOL_BUF, D_MODEL)),
                   seq_block((CONV_W - 1, 3 * D_MODEL)), seq_block((N_HEADS, HEAD, HEAD))],
        out_shape=[jax.ShapeDtypeStruct((bsz, n_tok, D_MODEL), _F32),
                   jax.ShapeDtypeStruct((bsz, POOL_BUF, D_MODEL), _F32),
                   jax.ShapeDtypeStruct((bsz, CONV_W - 1, 3 * D_MODEL), _F32),
                   jax.ShapeDtypeStruct((bsz, N_HEADS, HEAD, HEAD), _F32)],
        scratch_shapes=[pltpu.VMEM((n_seq, POOL_PAD + n_tok, D_MODEL), _F32),
                        pltpu.VMEM((n_seq, CONV_PAD + n_tok, 3 * D_MODEL), _F32)],
        compiler_params=pltpu.CompilerParams(dimension_semantics=("arbitrary",),
                                             vmem_limit_bytes=VMEM_LIMIT_BYTES),
        name="sample_layer",
    )(x, mod, spool, sconv, sdelta, *weights)


def _head_expansion_matrix():
    e = np.zeros((HEAD, 3072), np.float32)
    for piece in range(3):
        for hd in range(N_HEADS):
            rb, rg = piece * 2 * N_HEADS + hd, piece * 2 * N_HEADS + N_HEADS + hd
            e[rb, hd * HEAD:(hd + 1) * HEAD] = 1.0
            e[rb, 1024 + hd * CHUNK:1024 + (hd + 1) * CHUNK] = 1.0
            e[rg, 1536 + hd * HEAD:1536 + (hd + 1) * HEAD] = 1.0
            e[rg, 2560 + hd * CHUNK:2560 + (hd + 1) * CHUNK] = 1.0
    return jnp.asarray(e, _BF)


def _layer_weights(w_in, conv_w, a_log, dt_bias, head_norm_w, pool_w, pool_scale, p_a, p_b, w_out, ln_g, ln_b):
    w_bf = w_in.astype(_BF)
    lane_pad = lambda a: jnp.zeros((1, HEAD), _F32).at[0, N_HEADS:2 * N_HEADS].set(a.reshape(N_HEADS))
    return (w_bf, w_bf, w_bf[:, OFF_GATE:], conv_w,
            lane_pad(a_log), lane_pad(dt_bias), head_norm_w.reshape(1, HEAD), pool_w.astype(_BF),
            pool_scale.reshape(1, D_MODEL), p_a.astype(_BF), p_b.astype(_BF), w_out.astype(_BF),
            ln_g.reshape(1, D_MODEL), ln_b.reshape(1, D_MODEL), _head_expansion_matrix())


def kernel(x_prompt, x_sample, state_pool, state_conv, state_delta, c_prompt, c_sample, w_ada, b_ada, w_in, conv_w, a_log, dt_bias, head_norm_w, pool_w, pool_scale, p_a, p_b, w_out, ln_g, ln_b):
    assert w_in.shape[0] == 1, "single-layer trunk"
    bp, bs = x_prompt.shape[0], x_sample.shape[0]
    past_len = 16384
    drop = lambda a: a.reshape(a.shape[1:])
    mod = _ada_call(jnp.concatenate([c_prompt, c_sample], axis=0), drop(w_ada), b_ada)
    mod = mod.reshape(bp + bs, 1, 3 * D_MODEL)
    weights = _layer_weights(drop(w_in), drop(conv_w), a_log, dt_bias, head_norm_w, drop(pool_w), pool_scale,
                             drop(p_a), drop(p_b), drop(w_out), ln_g, ln_b)
    y_p, pool_p, conv_p, delta_p = _prompt_call(x_prompt, mod[:bp], weights, PROMPT_SEQS, PROMPT_TOKENS)
    y_s, pool_s, conv_s, delta_s = _sample_call(x_sample, mod[bp:], drop(state_pool), drop(state_conv),
                                                drop(state_delta), weights, past_len)
    lift = lambda a: a.reshape((1,) + a.shape)
    return (y_p, y_s, lift(pool_p), lift(conv_p), lift(delta_p), lift(pool_s), lift(conv_s), lift(delta_s))
```

```python
import functools

import numpy as np
import jax
import jax.numpy as jnp
from jax import lax
from jax.experimental import pallas as pl
from jax.experimental.pallas import tpu as pltpu

D_MODEL = 1024
N_HEADS = 8
HEAD = 128
N_PAIRS = N_HEADS // 2
POOL_WINDOWS = (2, 4, 8, 16)
POOL_GROUP = 256
POOL_BUF = 15
POOL_PAD = 16
CONV_W = 4
CONV_PAD = 8
CHUNK = 64
DEEPNORM_ALPHA = 2.0 ** 0.25
LN_EPS = 1e-5
RMS_EPS = 1e-6
L2_EPS = 1e-6
NEG_BIG = -1e30

OFF_MAIN_END = 6 * D_MODEL
OFF_GATE = OFF_MAIN_END + 2 * N_HEADS

PROMPT_SEQS = 4
PROMPT_TOKENS = CHUNK
SAMPLE_SEQS = 8
VMEM_LIMIT_BYTES = 58 * 1024 * 1024

_BF = jnp.bfloat16
_F32 = jnp.float32


def _dot(a, b):
    return jnp.dot(a.astype(_BF), b.astype(_BF), preferred_element_type=_F32)


def _dot_nt(a, b):
    return lax.dot_general(a.astype(_BF), b.astype(_BF), (((1,), (1,)), ((), ())),
                           preferred_element_type=_F32)


def _sigmoid(x):
    return 0.5 + 0.5 * jnp.tanh(0.5 * x)


def _silu(x):
    hx = 0.5 * x
    return hx + hx * jnp.tanh(hx)


def _softplus(x):
    return jnp.maximum(x, 0.0) + jnp.log1p(jnp.exp(-jnp.abs(x)))


def _lane_block_diag(x, width):
    nblk = x.shape[1] // width
    lane = lax.broadcasted_iota(jnp.int32, x.shape, 1)
    return jnp.concatenate([jnp.where(lane // width == i, x, 0.0) for i in range(nblk)], axis=0)


def _pair_block_diag(x):
    lane = lax.broadcasted_iota(jnp.int32, x.shape, 1) % (2 * HEAD)
    return jnp.concatenate([jnp.where(lane < HEAD, x, 0.0), jnp.where(lane >= HEAD, x, 0.0)], axis=0)


def _segment_cumsum(x, seg):
    row = lax.broadcasted_iota(jnp.int32, x.shape, 0) % seg
    shift = 1
    while shift < seg:
        x = x + jnp.where(row >= shift, pltpu.roll(x, shift, axis=0), 0.0)
        shift *= 2
    return x


def _expand_heads(narrow, e3_ref):
    lane = lax.broadcasted_iota(jnp.int32, narrow.shape, 1)
    x = jnp.where(lane < 2 * N_HEADS, narrow, 0.0)
    hi = x.astype(_BF).astype(_F32)
    rem = x - hi
    mid = rem.astype(_BF).astype(_F32)
    lo = (rem - mid).astype(_BF).astype(_F32)
    packed = hi + pltpu.roll(mid, 2 * N_HEADS, axis=1) + pltpu.roll(lo, 4 * N_HEADS, axis=1)
    wide = jnp.dot(packed.astype(_BF), e3_ref[...], preferred_element_type=_F32)
    return wide[:, 0:1024], wide[:, 1024:1536], wide[:, 1536:2560], wide[:, 2560:3072]


def _per_head_rsqrt_scale(x, eps, mean, post):
    outs = []
    for h in range(N_HEADS):
        xh = x[:, h * HEAD:(h + 1) * HEAD]
        ss = jnp.sum(xh * xh, axis=-1, keepdims=True)
        if mean:
            ss = ss * (1.0 / HEAD)
        outs.append(xh * (lax.rsqrt(ss + eps) * post))
    return jnp.concatenate(outs, axis=1)


def _gate_scalars(ba, alog_ref, dtb_ref, seg):
    beta = _sigmoid(ba)
    g = -jnp.exp(alog_ref[...]) * _softplus(ba + dtb_ref[...])
    gc = _segment_cumsum(g, seg)
    lane = lax.broadcasted_iota(jnp.int32, ba.shape, 1)
    return jnp.where(lane < N_HEADS, beta, gc)


def _chunk_masks(n_tok):
    row = lax.broadcasted_iota(jnp.int32, (CHUNK, 4 * HEAD), 0)
    col = lax.broadcasted_iota(jnp.int32, (CHUNK, 4 * HEAD), 1) % CHUNK
    same_seq = (row // n_tok) == (col // n_tok)
    return same_seq & (row >= col), same_seq & (row > col), row == col


def _chunks_intra(q, k, v, beta_e, beta_5, gc_e, gc_5, n_tok):
    n_chunks = q.shape[0] // CHUNK
    incl, strict, eye5 = _chunk_masks(n_tok)
    rows = [slice(c * CHUNK, (c + 1) * CHUNK) for c in range(n_chunks)]
    pair = [slice(p * 2 * HEAD, (p + 1) * 2 * HEAD) for p in range(N_PAIRS)]

    kk, qk = [], []
    for rs in rows:
        kk_parts, qk_parts = [], []
        for sl in pair:
            kp, qp = k[rs, sl], q[rs, sl]
            g2 = _dot_nt(jnp.concatenate([kp, qp], axis=0), _pair_block_diag(kp))
            kk_parts.append(g2[:CHUNK])
            qk_parts.append(g2[CHUNK:])
        kk.append(jnp.concatenate(kk_parts, axis=1))
        qk.append(jnp.concatenate(qk_parts, axis=1))

    dec, glast = [], []
    for rs in rows:
        g5 = gc_5[rs]
        gc_row = jnp.sum(jnp.where(eye5, g5, 0.0), axis=0, keepdims=True)
        dec.append(jnp.exp(jnp.where(incl, g5 - gc_row, NEG_BIG)))
        ge = gc_e[rs]
        glast.append(jnp.concatenate(
            [jnp.broadcast_to(ge[(s + 1) * n_tok - 1:(s + 1) * n_tok, :], (n_tok, D_MODEL))
             for s in range(CHUNK // n_tok)], axis=0))

    prob = [(c, grp) for c in range(n_chunks) for grp in range(2)]
    m, s, pw = {}, {}, {}
    for c, grp in prob:
        neg_l = jnp.where(strict[:, :256], -(beta_5[rows[c], grp * 256:(grp + 1) * 256]
                                              * dec[c][:, grp * 256:(grp + 1) * 256]
                                              * kk[c][:, grp * 256:(grp + 1) * 256]), 0.0)
        m[c, grp] = neg_l
        s[c, grp] = eye5[:, :256].astype(_F32) + neg_l
    for key in prob:
        pw[key] = _dot(m[key], _lane_block_diag(m[key], CHUNK))
    for _ in range(4):
        for key in prob:
            r = _dot(jnp.concatenate([pw[key], s[key]], axis=0), _lane_block_diag(pw[key], CHUNK))
            pw[key] = r[:CHUNK]
            s[key] = s[key] + r[CHUNK:]
    for key in prob:
        s[key] = s[key] + _dot(s[key], _lane_block_diag(pw[key], CHUNK))

    out = []
    for c, rs in enumerate(rows):
        gam = jnp.exp(gc_e[rs])
        be = beta_e[rs]
        bv = be * v[rs]
        gbk = be * gam * k[rs]
        wv_parts, wk_parts = [], []
        for p, sl in enumerate(pair):
            t_pair = s[c, p // 2][:, (p % 2) * HEAD:(p % 2 + 1) * HEAD]
            sol = _dot(t_pair, _pair_block_diag(jnp.concatenate([bv[:, sl], gbk[:, sl]], axis=1)))
            wv_parts.append(sol[:, :2 * HEAD])
            wk_parts.append(sol[:, 2 * HEAD:])
        out.append(dict(w_v=jnp.concatenate(wv_parts, axis=1), w_k=jnp.concatenate(wk_parts, axis=1),
                        qg=q[rs] * gam, kt=k[rs] * jnp.exp(glast[c] - gc_e[rs]),
                        qkd=qk[c] * dec[c], gl=jnp.exp(glast[c])))
    return out


def _intra_output(qkd, u):
    outs = []
    for p in range(N_PAIRS):
        up = u[:, p * 2 * HEAD:(p + 1) * 2 * HEAD]
        outs.append(_dot(qkd[:, p * HEAD:(p + 1) * HEAD], _pair_block_diag(up)))
    return jnp.concatenate(outs, axis=1)


def _state_to_rows(state_ref, ext_ref, row0, n_rows, n_seq):
    for s in range(n_seq):
        for r in range(n_rows):
            ext_ref[s, row0 + r:row0 + r + 1, :] = state_ref[r, s:s + 1, :]


def _rows_to_state(ext_ref, row0, n_rows, state_ref, seq0, n_seq):
    for s in range(n_seq):
        for r in range(n_rows):
            state_ref[r, seq0 + s:seq0 + s + 1, :] = ext_ref[s, row0 + r:row0 + r + 1, :]


def _front(x3, mod3, pos, refs, extp_ref, extc_ref):
    n_seq, n_tok, _ = x3.shape
    rows = n_seq * n_tok
    shift, scale = mod3[:, :, 0:D_MODEL], mod3[:, :, D_MODEL:2 * D_MODEL]
    h = (x3 * (1.0 + scale) + shift).reshape(rows, D_MODEL).astype(_BF)
    wmain_ref = refs["wmain"]
    proj = lambda c0, c1: jnp.dot(h, wmain_ref[:, c0:c1], preferred_element_type=_F32)

    extc_ref[:, CONV_PAD:CONV_PAD + n_tok, :] = proj(2 * D_MODEL, 5 * D_MODEL).reshape(n_seq, n_tok, 3 * D_MODEL)

    seg_c = CONV_PAD + n_tok
    convw_ref = refs["convw"]
    e = extc_ref[...].reshape(n_seq * seg_c, 3 * D_MODEL)
    acc = e * convw_ref[0:1, :]
    for i in range(1, CONV_W):
        acc = pltpu.roll(acc, 1, axis=0) + e * convw_ref[i:i + 1, :]
    qkv = _silu(acc.reshape(n_seq, seg_c, 3 * D_MODEL)[:, CONV_PAD:, :].reshape(rows, 3 * D_MODEL))
    q = _per_head_rsqrt_scale(qkv[:, 0:D_MODEL], L2_EPS, False, HEAD ** -0.5)
    k = _per_head_rsqrt_scale(qkv[:, D_MODEL:2 * D_MODEL], L2_EPS, False, 1.0)
    v = qkv[:, 2 * D_MODEL:]

    extp_ref[:, POOL_PAD:POOL_PAD + n_tok, :] = proj(0, D_MODEL).reshape(n_seq, n_tok, D_MODEL)
    z_a = proj(D_MODEL, 2 * D_MODEL)
    ba = jnp.dot(h, refs["wba"][...], preferred_element_type=_F32)
    z_b = proj(5 * D_MODEL, 6 * D_MODEL)
    ga = jnp.dot(h, refs["wg"][:, 0:D_MODEL], preferred_element_type=_F32)
    gb = jnp.dot(h, refs["wg"][:, D_MODEL:], preferred_element_type=_F32)

    seg_p = POOL_PAD + n_tok
    pooled = []
    for gi, w in enumerate(POOL_WINDOWS):
        e = extp_ref[:, :, gi * POOL_GROUP:(gi + 1) * POOL_GROUP].reshape(n_seq * seg_p, POOL_GROUP)
        acc, span = e, 1
        while span < w:
            acc = acc + pltpu.roll(acc, span, axis=0)
            span *= 2
        take = lambda a: a.reshape(n_seq, seg_p, POOL_GROUP)[:, POOL_PAD:, :].reshape(rows, POOL_GROUP)
        inv_cnt = 1.0 / jnp.minimum(pos + 1, w).astype(_F32)
        pooled.append(take(acc) * inv_cnt - take(e))
    mixed = jnp.concatenate([_dot(pp, refs["poolw"][gi]) for gi, pp in enumerate(pooled)], axis=1)
    y_a = (mixed * refs["pscale"][...] * _silu(z_a)).astype(_BF)
    a_proj = _dot(y_a, refs["pa"][...])

    beta_e, beta_5, gc_e, gc_5 = _expand_heads(_gate_scalars(ba, refs["alog"], refs["dtb"], n_tok), refs["e3"])
    return dict(q=q, k=k, v=v, beta_e=beta_e, beta_5=beta_5, gc_e=gc_e, gc_5=gc_5,
                a_proj=a_proj, z_b=z_b, ga=ga, gb=gb)


def _back(x, o, f, gate, refs):
    o = _per_head_rsqrt_scale(o, RMS_EPS, True, refs["hnw"][...])
    y_b = (o * _silu(f["z_b"])).astype(_BF)
    merged = _sigmoid(f["ga"]) * f["a_proj"] + _sigmoid(f["gb"]) * _dot(y_b, refs["pb"][...])
    sub = (1.0 + gate) * _dot(merged, refs["wout"][...])
    r = DEEPNORM_ALPHA * x + sub
    mu = jnp.mean(r, axis=-1, keepdims=True)
    rc = r - mu
    var = jnp.mean(rc * rc, axis=-1, keepdims=True)
    return rc * lax.rsqrt(var + LN_EPS) * refs["lng"][...] + refs["lnb"][...]


_WEIGHT_NAMES = ("wmain", "wba", "wg", "convw", "alog", "dtb", "hnw", "poolw", "pscale", "pa", "pb",
                 "wout", "lng", "lnb", "e3")


def _ada_kernel(c_ref, w_ref, b_ref, o_ref):
    o_ref[...] = _dot(_silu(c_ref[...]), w_ref[...]) + b_ref[...]


def _ada_call(c_all, w_ada, b_ada):
    n = c_all.shape[0]
    return pl.pallas_call(
        _ada_kernel,
        grid=(3,),
        in_specs=[pl.BlockSpec((n, D_MODEL), lambda j: (0, 0)),
                  pl.BlockSpec((D_MODEL, D_MODEL), lambda j: (0, j)),
                  pl.BlockSpec((1, D_MODEL), lambda j: (0, j))],
        out_specs=pl.BlockSpec((n, D_MODEL), lambda j: (0, j)),
        out_shape=jax.ShapeDtypeStruct((n, 3 * D_MODEL), _F32),
        compiler_params=pltpu.CompilerParams(dimension_semantics=("arbitrary",)),
        name="adaln_mod",
    )(c_all, w_ada, b_ada)


def _prompt_kernel(x_ref, mod_ref, *rest, n_seq, n_tok, n_steps, n_groups):
    refs = dict(zip(_WEIGHT_NAMES, rest[:len(_WEIGHT_NAMES)]))
    (y_ref, pool_out_ref, conv_out_ref, delta_out_ref,
     extp_ref, extc_ref, sbd_ref) = rest[len(_WEIGHT_NAMES):]
    t = pl.program_id(1)
    rows = n_seq * n_tok

    @pl.when(t == 0)
    def _():
        extp_ref[:, 0:POOL_PAD, :] = jnp.zeros((n_seq, POOL_PAD, D_MODEL), _F32)
        extc_ref[:, 0:CONV_PAD, :] = jnp.zeros((n_seq, CONV_PAD, 3 * D_MODEL), _F32)
        sbd_ref[...] = jnp.zeros(sbd_ref.shape, _F32)

    x3 = x_ref[...]
    mod3 = mod_ref[...]
    pos = t * n_tok + lax.broadcasted_iota(jnp.int32, (rows, 1), 0) % n_tok
    f = _front(x3, mod3, pos, refs, extp_ref, extc_ref)
    chunks = _chunks_intra(f["q"], f["k"], f["v"], f["beta_e"], f["beta_5"], f["gc_e"], f["gc_5"], n_tok)

    pair = [slice(p * 2 * HEAD, (p + 1) * 2 * HEAD) for p in range(N_PAIRS)]
    row_bd = lax.broadcasted_iota(jnp.int32, (2 * HEAD, 2 * HEAD), 0) // HEAD
    col_bd = lax.broadcasted_iota(jnp.int32, (2 * HEAD, 2 * HEAD), 1) // HEAD
    same_head = row_bd == col_bd
    us, ois = [], []
    for c, ck in enumerate(chunks):
        u_parts, oi_parts = [], []
        for p, sl in enumerate(pair):
            r = _dot(jnp.concatenate([ck["w_k"][:, sl], ck["qg"][:, sl]], axis=0), sbd_ref[c * N_PAIRS + p])
            u_parts.append(ck["w_v"][:, sl] - r[:CHUNK])
            oi_parts.append(r[CHUNK:])
        us.append(jnp.concatenate(u_parts, axis=1))
        ois.append(jnp.concatenate(oi_parts, axis=1))
    o = jnp.concatenate([ois[c] + _intra_output(ck["qkd"], us[c]) for c, ck in enumerate(chunks)], axis=0)
    for c, ck in enumerate(chunks):
        for p, sl in enumerate(pair):
            upd = _dot(ck["kt"][:, sl].T, us[c][:, sl])
            i = c * N_PAIRS + p
            sbd_ref[i] = ck["gl"][0:1, sl] * sbd_ref[i] + jnp.where(same_head, upd, 0.0)

    gate = jnp.broadcast_to(mod3[:, :, 2 * D_MODEL:], (n_seq, n_tok, D_MODEL)).reshape(rows, D_MODEL)
    y = _back(x3.reshape(rows, D_MODEL), o, f, gate, refs)
    y_ref[...] = y.reshape(n_seq, n_tok, D_MODEL)

    @pl.when(t == n_steps - 1)
    def _():
        for s in range(n_seq):
            for hd in range(N_HEADS):
                o0 = (hd % 2) * HEAD
                delta_out_ref[s, hd] = sbd_ref[s * N_PAIRS + hd // 2, o0:o0 + HEAD, o0:o0 + HEAD]

    for grp in range(n_groups):
        @pl.when((t == n_steps - 1) & (pl.program_id(0) == grp))
        def _():
            _rows_to_state(extp_ref, POOL_PAD + n_tok - POOL_BUF, POOL_BUF, pool_out_ref, grp * n_seq, n_seq)
            _rows_to_state(extc_ref, CONV_PAD + n_tok - (CONV_W - 1), CONV_W - 1, conv_out_ref, grp * n_seq, n_seq)

    @pl.when(t < n_steps - 1)
    def _():
        extp_ref[:, 0:POOL_PAD, :] = extp_ref[:, n_tok:n_tok + POOL_PAD, :]
        extc_ref[:, 0:CONV_PAD, :] = extc_ref[:, n_tok:n_tok + CONV_PAD, :]


def _const_spec(shape, index=None):
    index = (0,) * len(shape) if index is None else index
    return pl.BlockSpec(shape, lambda *_: index, pipeline_mode=pl.Buffered(1))


def _weight_specs(weights):
    specs = [_const_spec((D_MODEL, OFF_MAIN_END)), _const_spec((D_MODEL, HEAD), (0, OFF_MAIN_END // HEAD))]
    return specs + [_const_spec(w.shape) for w in weights[2:]]


def _prompt_call(x, mod, weights, n_seq, n_tok):
    bsz, seq, _ = x.shape
    n_steps = seq // n_tok
    assert n_tok == CHUNK and bsz % n_seq == 0 and seq % n_tok == 0
    kern = functools.partial(_prompt_kernel, n_seq=n_seq, n_tok=n_tok, n_steps=n_steps, n_groups=bsz // n_seq)
    seq_block = lambda shape: pl.BlockSpec((n_seq,) + shape, lambda g, t: (g,) + (0,) * len(shape))
    return pl.pallas_call(
        kern,
        grid=(bsz // n_seq, n_steps),
        in_specs=[pl.BlockSpec((n_seq, n_tok, D_MODEL), lambda g, t: (g, t, 0)),
                  seq_block((1, 3 * D_MODEL))] + _weight_specs(weights),
        out_specs=[pl.BlockSpec((n_seq, n_tok, D_MODEL), lambda g, t: (g, t, 0)),
                   pl.BlockSpec((POOL_BUF, bsz, D_MODEL), lambda g, t: (0, 0, 0)),
                   pl.BlockSpec((CONV_W - 1, bsz, 3 * D_MODEL), lambda g, t: (0, 0, 0)),
                   seq_block((N_HEADS, HEAD, HEAD))],
        out_shape=[jax.ShapeDtypeStruct((bsz, seq, D_MODEL), _F32),
                   jax.ShapeDtypeStruct((POOL_BUF, bsz, D_MODEL), _F32),
                   jax.ShapeDtypeStruct((CONV_W - 1, bsz, 3 * D_MODEL), _F32),
                   jax.ShapeDtypeStruct((bsz, N_HEADS, HEAD, HEAD), _F32)],
        scratch_shapes=[pltpu.VMEM((n_seq, POOL_PAD + n_tok, D_MODEL), _F32),
                        pltpu.VMEM((n_seq, CONV_PAD + n_tok, 3 * D_MODEL), _F32),
                        pltpu.VMEM((n_seq * N_PAIRS, 2 * HEAD, 2 * HEAD), _F32)],
        compiler_params=pltpu.CompilerParams(dimension_semantics=("arbitrary", "arbitrary"),
                                             vmem_limit_bytes=VMEM_LIMIT_BYTES),
        name="prompt_layer",
    )(x, mod, *weights)


def _sample_kernel(x_ref, mod_ref, spool_ref, sconv_ref, sdelta_ref, *rest, n_seq, n_tok, pos0):
    refs = dict(zip(_WEIGHT_NAMES, rest[:len(_WEIGHT_NAMES)]))
    (y_ref, pool_out_ref, conv_out_ref, delta_out_ref, extp_ref, extc_ref) = rest[len(_WEIGHT_NAMES):]
    rows = n_seq * n_tok

    _state_to_rows(spool_ref, extp_ref, POOL_PAD - POOL_BUF, POOL_BUF, n_seq)
    extp_ref[:, 0:POOL_PAD - POOL_BUF, :] = jnp.zeros((n_seq, POOL_PAD - POOL_BUF, D_MODEL), _F32)
    _state_to_rows(sconv_ref, extc_ref, CONV_PAD - (CONV_W - 1), CONV_W - 1, n_seq)
    extc_ref[:, 0:CONV_PAD - (CONV_W - 1), :] = jnp.zeros((n_seq, CONV_PAD - (CONV_W - 1), 3 * D_MODEL), _F32)

    x3 = x_ref[...]
    mod3 = mod_ref[...]
    pos = pos0 + lax.broadcasted_iota(jnp.int32, (rows, 1), 0) % n_tok
    f = _front(x3, mod3, pos, refs, extp_ref, extc_ref)
    _rows_to_state(extp_ref, POOL_PAD + n_tok - POOL_BUF, POOL_BUF, pool_out_ref, 0, n_seq)
    _rows_to_state(extc_ref, CONV_PAD + n_tok - (CONV_W - 1), CONV_W - 1, conv_out_ref, 0, n_seq)
    (ck,) = _chunks_intra(f["q"], f["k"], f["v"], f["beta_e"], f["beta_5"], f["gc_e"], f["gc_5"], n_tok)

    u_rows, oi_rows = [], []
    for s in range(n_seq):
        rs = slice(s * n_tok, (s + 1) * n_tok)
        u_h, oi_h = [], []
        for hd in range(N_HEADS):
            cs = slice(hd * HEAD, (hd + 1) * HEAD)
            r = _dot(jnp.concatenate([ck["w_k"][rs, cs], ck["qg"][rs, cs]], axis=0), sdelta_ref[s, hd])
            u_h.append(ck["w_v"][rs, cs] - r[:n_tok])
            oi_h.append(r[n_tok:])
        u_rows.append(jnp.concatenate(u_h, axis=1))
        oi_rows.append(jnp.concatenate(oi_h, axis=1))
    u = jnp.concatenate(u_rows, axis=0)
    o = jnp.concatenate(oi_rows, axis=0) + _intra_output(ck["qkd"], u)

    lane_seq = lax.broadcasted_iota(jnp.int32, (HEAD, rows), 1) // n_tok
    for hd in range(N_HEADS):
        cs = slice(hd * HEAD, (hd + 1) * HEAD)
        kt_t = ck["kt"][:, cs].T
        lhs = jnp.concatenate([jnp.where(lane_seq == s, kt_t, 0.0) for s in range(n_seq)], axis=0)
        upd = _dot(lhs, u[:, cs])
        for s in range(n_seq):
            gl = ck["gl"][s * n_tok:s * n_tok + 1, cs]
            delta_out_ref[s, hd] = gl * sdelta_ref[s, hd] + upd[s * HEAD:(s + 1) * HEAD]

    gate = jnp.broadcast_to(mod3[:, :, 2 * D_MODEL:], (n_seq, n_tok, D_MODEL)).reshape(rows, D_MODEL)
    y = _back(x3.reshape(rows, D_MODEL), o, f, gate, refs)
    y_ref[...] = y.reshape(n_seq, n_tok, D_MODEL)


def _sample_call(x, mod, spool, sconv, sdelta, weights, pos0):
    bsz, n_tok, _ = x.shape
    n_seq = SAMPLE_SEQS
    assert n_seq * n_tok == CHUNK and bsz % n_seq == 0
    kern = functools.partial(_sample_kernel, n_seq=n_seq, n_tok=n_tok, pos0=pos0)
    seq_block = lambda shape: pl.BlockSpec((n_seq,) + shape, lambda i: (i,) + (0,) * len(shape))
    row_state = lambda n_rows, width: pl.BlockSpec((n_rows, n_seq, width), lambda i: (0, i, 0))
    return pl.pallas_call(
        kern,
        grid=(bsz // n_seq,),
        in_specs=[seq_block((n_tok, D_MODEL)), seq_block((1, 3 * D_MODEL)),
                  row_state(POOL_BUF, D_MODEL), row_state(CONV_W - 1, 3 * D_MODEL),
                  seq_block((N_HEADS, HEAD, HEAD))] + _weight_specs(weights),
        out_specs=[seq_block((n_tok, D_MODEL)), row_state(POOL_BUF, D_MODEL),
                   row_state(CONV_W - 1, 3 * D_MODEL), seq_block((N_HEADS, HEAD, HEAD))],
        out_shape=[jax.ShapeDtypeStruct((bsz, n_tok, D_MODEL), _F32),
                   jax.ShapeDtypeStruct((POOL_BUF, bsz, D_MODEL), _F32),
                   jax.ShapeDtypeStruct((CONV_W - 1, bsz, 3 * D_MODEL), _F32),
                   jax.ShapeDtypeStruct((bsz, N_HEADS, HEAD, HEAD), _F32)],
        scratch_shapes=[pltpu.VMEM((n_seq, POOL_PAD + n_tok, D_MODEL), _F32),
                        pltpu.VMEM((n_seq, CONV_PAD + n_tok, 3 * D_MODEL), _F32)],
        compiler_params=pltpu.CompilerParams(dimension_semantics=("arbitrary",),
                                             vmem_limit_bytes=VMEM_LIMIT_BYTES),
        name="sample_layer",
    )(x, mod, spool, sconv, sdelta, *weights)


def _head_expansion_matrix():
    e = np.zeros((HEAD, 3072), np.float32)
    for piece in range(3):
        for hd in range(N_HEADS):
            rb, rg = piece * 2 * N_HEADS + hd, piece * 2 * N_HEADS + N_HEADS + hd
            e[rb, hd * HEAD:(hd + 1) * HEAD] = 1.0
            e[rb, 1024 + hd * CHUNK:1024 + (hd + 1) * CHUNK] = 1.0
            e[rg, 1536 + hd * HEAD:1536 + (hd + 1) * HEAD] = 1.0
            e[rg, 2560 + hd * CHUNK:2560 + (hd + 1) * CHUNK] = 1.0
    return jnp.asarray(e, _BF)


def _layer_weights(w_in, conv_w, a_log, dt_bias, head_norm_w, pool_w, pool_scale, p_a, p_b, w_out, ln_g, ln_b):
    w_bf = w_in.astype(_BF)
    lane_pad = lambda a: jnp.zeros((1, HEAD), _F32).at[0, N_HEADS:2 * N_HEADS].set(a.reshape(N_HEADS))
    return (w_bf, w_bf, w_bf[:, OFF_GATE:], conv_w,
            lane_pad(a_log), lane_pad(dt_bias), head_norm_w.reshape(1, HEAD), pool_w.astype(_BF),
            pool_scale.reshape(1, D_MODEL), p_a.astype(_BF), p_b.astype(_BF), w_out.astype(_BF),
            ln_g.reshape(1, D_MODEL), ln_b.reshape(1, D_MODEL), _head_expansion_matrix())


def kernel(x_prompt, x_sample, state_pool, state_conv, state_delta, c_prompt, c_sample, w_ada, b_ada, w_in, conv_w, a_log, dt_bias, head_norm_w, pool_w, pool_scale, p_a, p_b, w_out, ln_g, ln_b):
    assert w_in.shape[0] == 1, "single-layer trunk"
    bp, bs = x_prompt.shape[0], x_sample.shape[0]
    past_len = 16384
    drop = lambda a: a.reshape(a.shape[1:])
    mod = _ada_call(jnp.concatenate([c_prompt, c_sample], axis=0), drop(w_ada), b_ada)
    mod = mod.reshape(bp + bs, 1, 3 * D_MODEL)
    weights = _layer_weights(drop(w_in), drop(conv_w), a_log, dt_bias, head_norm_w, drop(pool_w), pool_scale,
                             drop(p_a), drop(p_b), drop(w_out), ln_g, ln_b)
    rows_major = lambda a: jnp.transpose(a, (1, 0, 2))
    y_p, pool_p, conv_p, delta_p = _prompt_call(x_prompt, mod[:bp], weights, PROMPT_SEQS, PROMPT_TOKENS)
    y_s, pool_s, conv_s, delta_s = _sample_call(x_sample, mod[bp:], rows_major(drop(state_pool)),
                                                rows_major(drop(state_conv)), drop(state_delta), weights, past_len)
    lift = lambda a: a.reshape((1,) + a.shape)
    return (y_p, y_s, lift(rows_major(pool_p)), lift(rows_major(conv_p)), lift(delta_p),
            lift(rows_major(pool_s)), lift(rows_major(conv_s)), lift(delta_s))
```

```python
import functools

import numpy as np
import jax
import jax.numpy as jnp
from jax import lax
from jax.experimental import pallas as pl
from jax.experimental.pallas import tpu as pltpu

D_MODEL = 1024
N_HEADS = 8
HEAD = 128
N_PAIRS = N_HEADS // 2
POOL_WINDOWS = (2, 4, 8, 16)
POOL_GROUP = 256
POOL_BUF = 15
POOL_PAD = 16
CONV_W = 4
CONV_PAD = 8
CHUNK = 64
DEEPNORM_ALPHA = 2.0 ** 0.25
LN_EPS = 1e-5
RMS_EPS = 1e-6
L2_EPS = 1e-6
NEG_BIG = -1e30

OFF_MAIN_END = 6 * D_MODEL
OFF_GATE = OFF_MAIN_END + 2 * N_HEADS

SEC = dict(u_a=(0, D_MODEL), z_a=(D_MODEL, 2 * D_MODEL), qkv=(2 * D_MODEL, 5 * D_MODEL),
           z_b=(5 * D_MODEL, 6 * D_MODEL), ba=(OFF_MAIN_END, OFF_MAIN_END + 128),
           ga=(OFF_MAIN_END + 128, OFF_MAIN_END + 128 + D_MODEL),
           gb=(OFF_MAIN_END + 128 + D_MODEL, OFF_MAIN_END + 128 + 2 * D_MODEL))
SEC_WIDTH = OFF_MAIN_END + 128 + 2 * D_MODEL

PROMPT_SEQS = 4
SAMPLE_RIDE = 2
PROMPT_TOKENS = CHUNK
SAMPLE_SEQS = 8
VMEM_LIMIT_BYTES = 58 * 1024 * 1024

_BF = jnp.bfloat16
_F32 = jnp.float32


def _dot(a, b):
    return jnp.dot(a.astype(_BF), b.astype(_BF), preferred_element_type=_F32)


def _dot_nt(a, b):
    return lax.dot_general(a.astype(_BF), b.astype(_BF), (((1,), (1,)), ((), ())),
                           preferred_element_type=_F32)


def _sigmoid(x):
    return 0.5 + 0.5 * jnp.tanh(0.5 * x)


def _silu(x):
    hx = 0.5 * x
    return hx + hx * jnp.tanh(hx)


def _softplus(x):
    return jnp.maximum(x, 0.0) + jnp.log1p(jnp.exp(-jnp.abs(x)))


def _lane_block_diag(x, width):
    nblk = x.shape[1] // width
    lane = lax.broadcasted_iota(jnp.int32, x.shape, 1)
    return jnp.concatenate([jnp.where(lane // width == i, x, 0.0) for i in range(nblk)], axis=0)


def _pair_block_diag(x):
    lane = lax.broadcasted_iota(jnp.int32, x.shape, 1) % (2 * HEAD)
    return jnp.concatenate([jnp.where(lane < HEAD, x, 0.0), jnp.where(lane >= HEAD, x, 0.0)], axis=0)


def _segment_cumsum(x, seg):
    row = lax.broadcasted_iota(jnp.int32, x.shape, 0) % seg
    shift = 1
    while shift < seg:
        x = x + jnp.where(row >= shift, pltpu.roll(x, shift, axis=0), 0.0)
        shift *= 2
    return x


def _expand_heads(narrow, e3_ref):
    lane = lax.broadcasted_iota(jnp.int32, narrow.shape, 1)
    x = jnp.where(lane < 2 * N_HEADS, narrow, 0.0)
    hi = x.astype(_BF).astype(_F32)
    rem = x - hi
    mid = rem.astype(_BF).astype(_F32)
    lo = (rem - mid).astype(_BF).astype(_F32)
    packed = hi + pltpu.roll(mid, 2 * N_HEADS, axis=1) + pltpu.roll(lo, 4 * N_HEADS, axis=1)
    wide = jnp.dot(packed.astype(_BF), e3_ref[...], preferred_element_type=_F32)
    return wide[:, 0:1024], wide[:, 1024:1536], wide[:, 1536:2560], wide[:, 2560:3072]


def _per_head_rsqrt_scale(x, eps, mean, post):
    outs = []
    for h in range(N_HEADS):
        xh = x[:, h * HEAD:(h + 1) * HEAD]
        ss = jnp.sum(xh * xh, axis=-1, keepdims=True)
        if mean:
            ss = ss * (1.0 / HEAD)
        outs.append(xh * (lax.rsqrt(ss + eps) * post))
    return jnp.concatenate(outs, axis=1)


def _gate_scalars(ba, alog_ref, dtb_ref, seg):
    beta = _sigmoid(ba)
    g = -jnp.exp(alog_ref[...]) * _softplus(ba + dtb_ref[...])
    gc = _segment_cumsum(g, seg)
    lane = lax.broadcasted_iota(jnp.int32, ba.shape, 1)
    return jnp.where(lane < N_HEADS, beta, gc)


def _chunk_masks(n_tok):
    row = lax.broadcasted_iota(jnp.int32, (CHUNK, 4 * HEAD), 0)
    col = lax.broadcasted_iota(jnp.int32, (CHUNK, 4 * HEAD), 1) % CHUNK
    same_seq = (row // n_tok) == (col // n_tok)
    return same_seq & (row >= col), same_seq & (row > col), row == col


def _chunks_intra(q, k, v, beta_e, beta_5, gc_e, gc_5, n_tok):
    n_chunks = q.shape[0] // CHUNK
    incl, strict, eye5 = _chunk_masks(n_tok)
    rows = [slice(c * CHUNK, (c + 1) * CHUNK) for c in range(n_chunks)]
    pair = [slice(p * 2 * HEAD, (p + 1) * 2 * HEAD) for p in range(N_PAIRS)]

    kk, qk = [], []
    for rs in rows:
        kk_parts, qk_parts = [], []
        for sl in pair:
            kp, qp = k[rs, sl], q[rs, sl]
            g2 = _dot_nt(jnp.concatenate([kp, qp], axis=0), _pair_block_diag(kp))
            kk_parts.append(g2[:CHUNK])
            qk_parts.append(g2[CHUNK:])
        kk.append(jnp.concatenate(kk_parts, axis=1))
        qk.append(jnp.concatenate(qk_parts, axis=1))

    dec, glast = [], []
    for rs in rows:
        g5 = gc_5[rs]
        gc_row = jnp.sum(jnp.where(eye5, g5, 0.0), axis=0, keepdims=True)
        dec.append(jnp.exp(jnp.where(incl, g5 - gc_row, NEG_BIG)))
        ge = gc_e[rs]
        glast.append(jnp.concatenate(
            [jnp.broadcast_to(ge[(s + 1) * n_tok - 1:(s + 1) * n_tok, :], (n_tok, D_MODEL))
             for s in range(CHUNK // n_tok)], axis=0))

    prob = [(c, grp) for c in range(n_chunks) for grp in range(2)]
    m, s, pw = {}, {}, {}
    for c, grp in prob:
        neg_l = jnp.where(strict[:, :256], -(beta_5[rows[c], grp * 256:(grp + 1) * 256]
                                              * dec[c][:, grp * 256:(grp + 1) * 256]
                                              * kk[c][:, grp * 256:(grp + 1) * 256]), 0.0)
        m[c, grp] = neg_l
        s[c, grp] = eye5[:, :256].astype(_F32) + neg_l
    for key in prob:
        pw[key] = _dot(m[key], _lane_block_diag(m[key], CHUNK))
    for _ in range(4):
        for key in prob:
            r = _dot(jnp.concatenate([pw[key], s[key]], axis=0), _lane_block_diag(pw[key], CHUNK))
            pw[key] = r[:CHUNK]
            s[key] = s[key] + r[CHUNK:]
    for key in prob:
        s[key] = s[key] + _dot(s[key], _lane_block_diag(pw[key], CHUNK))

    out = []
    for c, rs in enumerate(rows):
        gam = jnp.exp(gc_e[rs])
        be = beta_e[rs]
        bv = be * v[rs]
        gbk = be * gam * k[rs]
        wv_parts, wk_parts = [], []
        for p, sl in enumerate(pair):
            t_pair = s[c, p // 2][:, (p % 2) * HEAD:(p % 2 + 1) * HEAD]
            sol = _dot(t_pair, _pair_block_diag(jnp.concatenate([bv[:, sl], gbk[:, sl]], axis=1)))
            wv_parts.append(sol[:, :2 * HEAD])
            wk_parts.append(sol[:, 2 * HEAD:])
        out.append(dict(w_v=jnp.concatenate(wv_parts, axis=1), w_k=jnp.concatenate(wk_parts, axis=1),
                        qg=q[rs] * gam, kt=k[rs] * jnp.exp(glast[c] - gc_e[rs]),
                        qkd=qk[c] * dec[c], gl=jnp.exp(glast[c])))
    return out


def _intra_output(qkd, u):
    outs = []
    for p in range(N_PAIRS):
        up = u[:, p * 2 * HEAD:(p + 1) * 2 * HEAD]
        outs.append(_dot(qkd[:, p * HEAD:(p + 1) * HEAD], _pair_block_diag(up)))
    return jnp.concatenate(outs, axis=1)


def _state_to_rows(state_ref, ext_ref, row0, n_rows, n_seq):
    for s in range(n_seq):
        for r in range(n_rows):
            ext_ref[s, row0 + r:row0 + r + 1, :] = state_ref[r, s:s + 1, :]


def _rows_to_state(ext_ref, row0, n_rows, state_ref, seq0, n_seq):
    for s in range(n_seq):
        for r in range(n_rows):
            state_ref[r, seq0 + s:seq0 + s + 1, :] = ext_ref[s, row0 + r:row0 + r + 1, :]


def _modulated(x3, mod3):
    shift, scale = mod3[:, :, 0:D_MODEL], mod3[:, :, D_MODEL:2 * D_MODEL]
    return (x3 * (1.0 + scale) + shift).reshape(x3.shape[0] * x3.shape[1], D_MODEL).astype(_BF)


def _front(sec, pos, refs, extp_ref, extc_ref, n_seq, n_tok):
    rows = n_seq * n_tok
    extc_ref[:, CONV_PAD:CONV_PAD + n_tok, :] = sec("qkv").reshape(n_seq, n_tok, 3 * D_MODEL)

    seg_c = CONV_PAD + n_tok
    convw_ref = refs["convw"]
    e = extc_ref[...].reshape(n_seq * seg_c, 3 * D_MODEL)
    acc = e * convw_ref[0:1, :]
    for i in range(1, CONV_W):
        acc = pltpu.roll(acc, 1, axis=0) + e * convw_ref[i:i + 1, :]
    qkv = _silu(acc.reshape(n_seq, seg_c, 3 * D_MODEL)[:, CONV_PAD:, :].reshape(rows, 3 * D_MODEL))
    q = _per_head_rsqrt_scale(qkv[:, 0:D_MODEL], L2_EPS, False, HEAD ** -0.5)
    k = _per_head_rsqrt_scale(qkv[:, D_MODEL:2 * D_MODEL], L2_EPS, False, 1.0)
    v = qkv[:, 2 * D_MODEL:]

    extp_ref[:, POOL_PAD:POOL_PAD + n_tok, :] = sec("u_a").reshape(n_seq, n_tok, D_MODEL)
    z_a, ba, z_b, ga, gb = sec("z_a"), sec("ba"), sec("z_b"), sec("ga"), sec("gb")

    seg_p = POOL_PAD + n_tok
    pooled = []
    for gi, w in enumerate(POOL_WINDOWS):
        e = extp_ref[:, :, gi * POOL_GROUP:(gi + 1) * POOL_GROUP].reshape(n_seq * seg_p, POOL_GROUP)
        acc, span = e, 1
        while span < w:
            acc = acc + pltpu.roll(acc, span, axis=0)
            span *= 2
        take = lambda a: a.reshape(n_seq, seg_p, POOL_GROUP)[:, POOL_PAD:, :].reshape(rows, POOL_GROUP)
        inv_cnt = 1.0 / jnp.minimum(pos + 1, w).astype(_F32)
        pooled.append(take(acc) * inv_cnt - take(e))
    mixed = jnp.concatenate([_dot(pp, refs["poolw"][gi]) for gi, pp in enumerate(pooled)], axis=1)
    y_a = (mixed * refs["pscale"][...] * _silu(z_a)).astype(_BF)
    a_proj = _dot(y_a, refs["pa"][...])

    beta_e, beta_5, gc_e, gc_5 = _expand_heads(_gate_scalars(ba, refs["alog"], refs["dtb"], n_tok), refs["e3"])
    return dict(q=q, k=k, v=v, beta_e=beta_e, beta_5=beta_5, gc_e=gc_e, gc_5=gc_5,
                a_proj=a_proj, z_b=z_b, ga=ga, gb=gb)


def _back(x, o, f, gate, refs):
    o = _per_head_rsqrt_scale(o, RMS_EPS, True, refs["hnw"][...])
    y_b = (o * _silu(f["z_b"])).astype(_BF)
    merged = _sigmoid(f["ga"]) * f["a_proj"] + _sigmoid(f["gb"]) * _dot(y_b, refs["pb"][...])
    sub = (1.0 + gate) * _dot(merged, refs["wout"][...])
    r = DEEPNORM_ALPHA * x + sub
    mu = jnp.mean(r, axis=-1, keepdims=True)
    rc = r - mu
    var = jnp.mean(rc * rc, axis=-1, keepdims=True)
    return rc * lax.rsqrt(var + LN_EPS) * refs["lng"][...] + refs["lnb"][...]


_WEIGHT_NAMES = ("wmain", "wba", "wg", "convw", "alog", "dtb", "hnw", "poolw", "pscale", "pa", "pb",
                 "wout", "lng", "lnb", "e3")
_N_PROJ_WEIGHTS = 3
_SAMPLE_WEIGHT_NAMES = _WEIGHT_NAMES[_N_PROJ_WEIGHTS:]


def _ada_kernel(c_ref, w_ref, b_ref, o_ref):
    o_ref[...] = _dot(_silu(c_ref[...]), w_ref[...]) + b_ref[...]


def _ada_call(c_all, w_ada, b_ada):
    n = c_all.shape[0]
    return pl.pallas_call(
        _ada_kernel,
        grid=(3,),
        in_specs=[pl.BlockSpec((n, D_MODEL), lambda j: (0, 0)),
                  pl.BlockSpec((D_MODEL, D_MODEL), lambda j: (0, j)),
                  pl.BlockSpec((1, D_MODEL), lambda j: (0, j))],
        out_specs=pl.BlockSpec((n, D_MODEL), lambda j: (0, j)),
        out_shape=jax.ShapeDtypeStruct((n, 3 * D_MODEL), _F32),
        compiler_params=pltpu.CompilerParams(dimension_semantics=("arbitrary",)),
        name="adaln_mod",
    )(c_all, w_ada, b_ada)


def _prompt_kernel(x_ref, mod_ref, xs_ref, mods_ref, *rest, n_seq, n_tok, n_steps, n_groups):
    refs = dict(zip(_WEIGHT_NAMES, rest[:len(_WEIGHT_NAMES)]))
    (y_ref, pool_out_ref, conv_out_ref, delta_out_ref, projs_ref,
     extp_ref, extc_ref, sbd_ref) = rest[len(_WEIGHT_NAMES):]
    t = pl.program_id(1)
    rows = n_seq * n_tok

    @pl.when(t == 0)
    def _():
        extp_ref[:, 0:POOL_PAD, :] = jnp.zeros((n_seq, POOL_PAD, D_MODEL), _F32)
        extc_ref[:, 0:CONV_PAD, :] = jnp.zeros((n_seq, CONV_PAD, 3 * D_MODEL), _F32)
        sbd_ref[...] = jnp.zeros(sbd_ref.shape, _F32)

    x3 = x_ref[...]
    mod3 = mod_ref[...]
    pos = t * n_tok + lax.broadcasted_iota(jnp.int32, (rows, 1), 0) % n_tok
    h = jnp.concatenate([_modulated(x3, mod3), _modulated(xs_ref[...], mods_ref[...])], axis=0)
    w_of = dict(ba=(refs["wba"], 0, HEAD), ga=(refs["wg"], 0, D_MODEL), gb=(refs["wg"], D_MODEL, 2 * D_MODEL))

    def sec(name):
        w_ref, c0, c1 = w_of.get(name, (refs["wmain"],) + SEC[name])
        full = jnp.dot(h, w_ref[:, c0:c1], preferred_element_type=_F32)
        projs_ref[:, SEC[name][0]:SEC[name][1]] = full[rows:]
        return full[:rows]

    f = _front(sec, pos, refs, extp_ref, extc_ref, n_seq, n_tok)
    chunks = _chunks_intra(f["q"], f["k"], f["v"], f["beta_e"], f["beta_5"], f["gc_e"], f["gc_5"], n_tok)

    pair = [slice(p * 2 * HEAD, (p + 1) * 2 * HEAD) for p in range(N_PAIRS)]
    row_bd = lax.broadcasted_iota(jnp.int32, (2 * HEAD, 2 * HEAD), 0) // HEAD
    col_bd = lax.broadcasted_iota(jnp.int32, (2 * HEAD, 2 * HEAD), 1) // HEAD
    same_head = row_bd == col_bd
    us, ois = [], []
    for c, ck in enumerate(chunks):
        u_parts, oi_parts = [], []
        for p, sl in enumerate(pair):
            r = _dot(jnp.concatenate([ck["w_k"][:, sl], ck["qg"][:, sl]], axis=0), sbd_ref[c * N_PAIRS + p])
            u_parts.append(ck["w_v"][:, sl] - r[:CHUNK])
            oi_parts.append(r[CHUNK:])
        us.append(jnp.concatenate(u_parts, axis=1))
        ois.append(jnp.concatenate(oi_parts, axis=1))
    o = jnp.concatenate([ois[c] + _intra_output(ck["qkd"], us[c]) for c, ck in enumerate(chunks)], axis=0)
    for c, ck in enumerate(chunks):
        for p, sl in enumerate(pair):
            upd = _dot(ck["kt"][:, sl].T, us[c][:, sl])
            i = c * N_PAIRS + p
            sbd_ref[i] = ck["gl"][0:1, sl] * sbd_ref[i] + jnp.where(same_head, upd, 0.0)

    gate = jnp.broadcast_to(mod3[:, :, 2 * D_MODEL:], (n_seq, n_tok, D_MODEL)).reshape(rows, D_MODEL)
    y = _back(x3.reshape(rows, D_MODEL), o, f, gate, refs)
    y_ref[...] = y.reshape(n_seq, n_tok, D_MODEL)

    @pl.when(t == n_steps - 1)
    def _():
        for s in range(n_seq):
            for hd in range(N_HEADS):
                o0 = (hd % 2) * HEAD
                delta_out_ref[s, hd] = sbd_ref[s * N_PAIRS + hd // 2, o0:o0 + HEAD, o0:o0 + HEAD]

    for grp in range(n_groups):
        @pl.when((t == n_steps - 1) & (pl.program_id(0) == grp))
        def _():
            _rows_to_state(extp_ref, POOL_PAD + n_tok - POOL_BUF, POOL_BUF, pool_out_ref, grp * n_seq, n_seq)
            _rows_to_state(extc_ref, CONV_PAD + n_tok - (CONV_W - 1), CONV_W - 1, conv_out_ref, grp * n_seq, n_seq)

    @pl.when(t < n_steps - 1)
    def _():
        extp_ref[:, 0:POOL_PAD, :] = extp_ref[:, n_tok:n_tok + POOL_PAD, :]
        extc_ref[:, 0:CONV_PAD, :] = extc_ref[:, n_tok:n_tok + CONV_PAD, :]


def _const_spec(shape, index=None):
    index = (0,) * len(shape) if index is None else index
    return pl.BlockSpec(shape, lambda *_: index, pipeline_mode=pl.Buffered(1))


def _weight_specs(weights):
    specs = [_const_spec((D_MODEL, OFF_MAIN_END)), _const_spec((D_MODEL, HEAD), (0, OFF_MAIN_END // HEAD))]
    return specs + [_const_spec(w.shape) for w in weights[2:]]


def _prompt_call(x, mod, xs, mods, weights, n_seq, n_tok):
    bsz, seq, _ = x.shape
    n_steps = seq // n_tok
    n_sample, s_tok, _ = xs.shape
    assert n_tok == CHUNK and bsz % n_seq == 0 and seq % n_tok == 0
    assert n_sample == SAMPLE_RIDE * (bsz // n_seq) * n_steps
    ride = lambda shape: pl.BlockSpec((SAMPLE_RIDE,) + shape, lambda g, t: (g * n_steps + t,) + (0,) * len(shape))
    kern = functools.partial(_prompt_kernel, n_seq=n_seq, n_tok=n_tok, n_steps=n_steps, n_groups=bsz // n_seq)
    seq_block = lambda shape: pl.BlockSpec((n_seq,) + shape, lambda g, t: (g,) + (0,) * len(shape))
    return pl.pallas_call(
        kern,
        grid=(bsz // n_seq, n_steps),
        in_specs=[pl.BlockSpec((n_seq, n_tok, D_MODEL), lambda g, t: (g, t, 0)),
                  seq_block((1, 3 * D_MODEL)), ride((s_tok, D_MODEL)), ride((1, 3 * D_MODEL))]
                 + _weight_specs(weights),
        out_specs=[pl.BlockSpec((n_seq, n_tok, D_MODEL), lambda g, t: (g, t, 0)),
                   pl.BlockSpec((POOL_BUF, bsz, D_MODEL), lambda g, t: (0, 0, 0)),
                   pl.BlockSpec((CONV_W - 1, bsz, 3 * D_MODEL), lambda g, t: (0, 0, 0)),
                   seq_block((N_HEADS, HEAD, HEAD)),
                   pl.BlockSpec((SAMPLE_RIDE * s_tok, SEC_WIDTH), lambda g, t: (g * n_steps + t, 0))],
        out_shape=[jax.ShapeDtypeStruct((bsz, seq, D_MODEL), _F32),
                   jax.ShapeDtypeStruct((POOL_BUF, bsz, D_MODEL), _F32),
                   jax.ShapeDtypeStruct((CONV_W - 1, bsz, 3 * D_MODEL), _F32),
                   jax.ShapeDtypeStruct((bsz, N_HEADS, HEAD, HEAD), _F32),
                   jax.ShapeDtypeStruct((n_sample * s_tok, SEC_WIDTH), _F32)],
        scratch_shapes=[pltpu.VMEM((n_seq, POOL_PAD + n_tok, D_MODEL), _F32),
                        pltpu.VMEM((n_seq, CONV_PAD + n_tok, 3 * D_MODEL), _F32),
                        pltpu.VMEM((n_seq * N_PAIRS, 2 * HEAD, 2 * HEAD), _F32)],
        compiler_params=pltpu.CompilerParams(dimension_semantics=("arbitrary", "arbitrary"),
                                             vmem_limit_bytes=VMEM_LIMIT_BYTES),
        name="prompt_layer",
    )(x, mod, xs, mods, *weights)


def _sample_kernel(x_ref, mod_ref, projs_ref, spool_ref, sconv_ref, sdelta_ref, *rest, n_seq, n_tok, pos0):
    refs = dict(zip(_SAMPLE_WEIGHT_NAMES, rest[:len(_SAMPLE_WEIGHT_NAMES)]))
    (y_ref, pool_out_ref, conv_out_ref, delta_out_ref, extp_ref, extc_ref) = rest[len(_SAMPLE_WEIGHT_NAMES):]
    rows = n_seq * n_tok

    _state_to_rows(spool_ref, extp_ref, POOL_PAD - POOL_BUF, POOL_BUF, n_seq)
    extp_ref[:, 0:POOL_PAD - POOL_BUF, :] = jnp.zeros((n_seq, POOL_PAD - POOL_BUF, D_MODEL), _F32)
    _state_to_rows(sconv_ref, extc_ref, CONV_PAD - (CONV_W - 1), CONV_W - 1, n_seq)
    extc_ref[:, 0:CONV_PAD - (CONV_W - 1), :] = jnp.zeros((n_seq, CONV_PAD - (CONV_W - 1), 3 * D_MODEL), _F32)

    x3 = x_ref[...]
    mod3 = mod_ref[...]
    pos = pos0 + lax.broadcasted_iota(jnp.int32, (rows, 1), 0) % n_tok
    f = _front(lambda name: projs_ref[:, SEC[name][0]:SEC[name][1]], pos, refs, extp_ref, extc_ref, n_seq, n_tok)
    _rows_to_state(extp_ref, POOL_PAD + n_tok - POOL_BUF, POOL_BUF, pool_out_ref, 0, n_seq)
    _rows_to_state(extc_ref, CONV_PAD + n_tok - (CONV_W - 1), CONV_W - 1, conv_out_ref, 0, n_seq)
    (ck,) = _chunks_intra(f["q"], f["k"], f["v"], f["beta_e"], f["beta_5"], f["gc_e"], f["gc_5"], n_tok)

    u_rows, oi_rows = [], []
    for s in range(n_seq):
        rs = slice(s * n_tok, (s + 1) * n_tok)
        u_h, oi_h = [], []
        for hd in range(N_HEADS):
            cs = slice(hd * HEAD, (hd + 1) * HEAD)
            r = _dot(jnp.concatenate([ck["w_k"][rs, cs], ck["qg"][rs, cs]], axis=0), sdelta_ref[s, hd])
            u_h.append(ck["w_v"][rs, cs] - r[:n_tok])
            oi_h.append(r[n_tok:])
        u_rows.append(jnp.concatenate(u_h, axis=1))
        oi_rows.append(jnp.concatenate(oi_h, axis=1))
    u = jnp.concatenate(u_rows, axis=0)
    o = jnp.concatenate(oi_rows, axis=0) + _intra_output(ck["qkd"], u)

    lane_seq = lax.broadcasted_iota(jnp.int32, (HEAD, rows), 1) // n_tok
    for hd in range(N_HEADS):
        cs = slice(hd * HEAD, (hd + 1) * HEAD)
        kt_t = ck["kt"][:, cs].T
        lhs = jnp.concatenate([jnp.where(lane_seq == s, kt_t, 0.0) for s in range(n_seq)], axis=0)
        upd = _dot(lhs, u[:, cs])
        for s in range(n_seq):
            gl = ck["gl"][s * n_tok:s * n_tok + 1, cs]
            delta_out_ref[s, hd] = gl * sdelta_ref[s, hd] + upd[s * HEAD:(s + 1) * HEAD]

    gate = jnp.broadcast_to(mod3[:, :, 2 * D_MODEL:], (n_seq, n_tok, D_MODEL)).reshape(rows, D_MODEL)
    y = _back(x3.reshape(rows, D_MODEL), o, f, gate, refs)
    y_ref[...] = y.reshape(n_seq, n_tok, D_MODEL)


def _sample_call(x, mod, projs, spool, sconv, sdelta, weights, pos0):
    bsz, n_tok, _ = x.shape
    n_seq = SAMPLE_SEQS
    assert n_seq * n_tok == CHUNK and bsz % n_seq == 0
    kern = functools.partial(_sample_kernel, n_seq=n_seq, n_tok=n_tok, pos0=pos0)
    seq_block = lambda shape: pl.BlockSpec((n_seq,) + shape, lambda i: (i,) + (0,) * len(shape))
    row_state = lambda n_rows, width: pl.BlockSpec((n_rows, n_seq, width), lambda i: (0, i, 0))
    return pl.pallas_call(
        kern,
        grid=(bsz // n_seq,),
        in_specs=[seq_block((n_tok, D_MODEL)), seq_block((1, 3 * D_MODEL)),
                  pl.BlockSpec((n_seq * n_tok, SEC_WIDTH), lambda i: (i, 0)),
                  row_state(POOL_BUF, D_MODEL), row_state(CONV_W - 1, 3 * D_MODEL),
                  seq_block((N_HEADS, HEAD, HEAD))] + [_const_spec(w.shape) for w in weights],
        out_specs=[seq_block((n_tok, D_MODEL)), row_state(POOL_BUF, D_MODEL),
                   row_state(CONV_W - 1, 3 * D_MODEL), seq_block((N_HEADS, HEAD, HEAD))],
        out_shape=[jax.ShapeDtypeStruct((bsz, n_tok, D_MODEL), _F32),
                   jax.ShapeDtypeStruct((POOL_BUF, bsz, D_MODEL), _F32),
                   jax.ShapeDtypeStruct((CONV_W - 1, bsz, 3 * D_MODEL), _F32),
                   jax.ShapeDtypeStruct((bsz, N_HEADS, HEAD, HEAD), _F32)],
        scratch_shapes=[pltpu.VMEM((n_seq, POOL_PAD + n_tok, D_MODEL), _F32),
                        pltpu.VMEM((n_seq, CONV_PAD + n_tok, 3 * D_MODEL), _F32)],
        compiler_params=pltpu.CompilerParams(dimension_semantics=("arbitrary",),
                                             vmem_limit_bytes=VMEM_LIMIT_BYTES),
        name="sample_layer",
    )(x, mod, projs, spool, sconv, sdelta, *weights)


def _head_expansion_matrix():
    e = np.zeros((HEAD, 3072), np.float32)
    for piece in range(3):
        for hd in range(N_HEADS):
            rb, rg = piece * 2 * N_HEADS + hd, piece * 2 * N_HEADS + N_HEADS + hd
            e[rb, hd * HEAD:(hd + 1) * HEAD] = 1.0
            e[rb, 1024 + hd * CHUNK:1024 + (hd + 1) * CHUNK] = 1.0
            e[rg, 1536 + hd * HEAD:1536 + (hd + 1) * HEAD] = 1.0
            e[rg, 2560 + hd * CHUNK:2560 + (hd + 1) * CHUNK] = 1.0
    return jnp.asarray(e, _BF)


def _layer_weights(w_in, conv_w, a_log, dt_bias, head_norm_w, pool_w, pool_scale, p_a, p_b, w_out, ln_g, ln_b):
    w_bf = w_in.astype(_BF)
    lane_pad = lambda a: jnp.zeros((1, HEAD), _F32).at[0, N_HEADS:2 * N_HEADS].set(a.reshape(N_HEADS))
    return (w_bf, w_bf, w_bf[:, OFF_GATE:], conv_w,
            lane_pad(a_log), lane_pad(dt_bias), head_norm_w.reshape(1, HEAD), pool_w.astype(_BF),
            pool_scale.reshape(1, D_MODEL), p_a.astype(_BF), p_b.astype(_BF), w_out.astype(_BF),
            ln_g.reshape(1, D_MODEL), ln_b.reshape(1, D_MODEL), _head_expansion_matrix())


def kernel(x_prompt, x_sample, state_pool, state_conv, state_delta, c_prompt, c_sample, w_ada, b_ada, w_in, conv_w, a_log, dt_bias, head_norm_w, pool_w, pool_scale, p_a, p_b, w_out, ln_g, ln_b):
    assert w_in.shape[0] == 1, "single-layer trunk"
    bp, bs = x_prompt.shape[0], x_sample.shape[0]
    past_len = 16384
    drop = lambda a: a.reshape(a.shape[1:])
    mod = _ada_call(jnp.concatenate([c_prompt, c_sample], axis=0), drop(w_ada), b_ada)
    mod = mod.reshape(bp + bs, 1, 3 * D_MODEL)
    weights = _layer_weights(drop(w_in), drop(conv_w), a_log, dt_bias, head_norm_w, drop(pool_w), pool_scale,
                             drop(p_a), drop(p_b), drop(w_out), ln_g, ln_b)
    rows_major = lambda a: jnp.transpose(a, (1, 0, 2))
    y_p, pool_p, conv_p, delta_p, proj_s = _prompt_call(x_prompt, mod[:bp], x_sample, mod[bp:], weights,
                                                        PROMPT_SEQS, PROMPT_TOKENS)
    y_s, pool_s, conv_s, delta_s = _sample_call(x_sample, mod[bp:], proj_s, rows_major(drop(state_pool)),
                                                rows_major(drop(state_conv)), drop(state_delta),
                                                weights[_N_PROJ_WEIGHTS:], past_len)
    lift = lambda a: a.reshape((1,) + a.shape)
    return (y_p, y_s, lift(rows_major(pool_p)), lift(rows_major(conv_p)), lift(delta_p),
            lift(rows_major(pool_s)), lift(rows_major(conv_s)), lift(delta_s))
```

```python
import functools

import numpy as np
import jax
import jax.numpy as jnp
from jax import lax
from jax.experimental import pallas as pl
from jax.experimental.pallas import tpu as pltpu

D_MODEL = 1024
N_HEADS = 8
HEAD = 128
N_PAIRS = N_HEADS // 2
POOL_WINDOWS = (2, 4, 8, 16)
POOL_GROUP = 256
POOL_BUF = 15
POOL_PAD = 16
CONV_W = 4
CONV_PAD = 8
CHUNK = 64
PAST_LEN = 16384
DEEPNORM_ALPHA = 2.0 ** 0.25
LN_EPS = 1e-5
RMS_EPS = 1e-6
L2_EPS = 1e-6
NEG_BIG = -1e30

OFF_MAIN_END = 6 * D_MODEL
OFF_GATE = OFF_MAIN_END + 2 * N_HEADS

SEC = dict(u_a=(0, D_MODEL), z_a=(D_MODEL, 2 * D_MODEL), qkv=(2 * D_MODEL, 5 * D_MODEL),
           z_b=(5 * D_MODEL, 6 * D_MODEL), ba=(OFF_MAIN_END, OFF_MAIN_END + 128),
           ga=(OFF_MAIN_END + 128, OFF_MAIN_END + 128 + D_MODEL),
           gb=(OFF_MAIN_END + 128 + D_MODEL, OFF_MAIN_END + 128 + 2 * D_MODEL))
SEC_WIDTH = OFF_MAIN_END + 128 + 2 * D_MODEL

PROMPT_SEQS = 4
SAMPLE_RIDE = 2
PROMPT_TOKENS = CHUNK
SAMPLE_SEQS = 8
ADA_COLS = 512
VMEM_LIMIT_BYTES = 58 * 1024 * 1024

_BF = jnp.bfloat16
_F32 = jnp.float32


def _dot(a, b):
    return jnp.dot(a.astype(_BF), b.astype(_BF), preferred_element_type=_F32)


def _dot_nt(a, b):
    return lax.dot_general(a.astype(_BF), b.astype(_BF), (((1,), (1,)), ((), ())),
                           preferred_element_type=_F32)


def _sigmoid(x):
    return 0.5 + 0.5 * jnp.tanh(0.5 * x)


def _silu(x):
    hx = 0.5 * x
    return hx + hx * jnp.tanh(hx)


def _softplus(x):
    return jnp.maximum(x, 0.0) + jnp.log1p(jnp.exp(-jnp.abs(x)))


def _lane_block_diag(x, width):
    nblk = x.shape[1] // width
    lane = lax.broadcasted_iota(jnp.int32, x.shape, 1)
    return jnp.concatenate([jnp.where(lane // width == i, x, 0.0) for i in range(nblk)], axis=0)


def _pair_block_diag(x):
    lane = lax.broadcasted_iota(jnp.int32, x.shape, 1) % (2 * HEAD)
    return jnp.concatenate([jnp.where(lane < HEAD, x, 0.0), jnp.where(lane >= HEAD, x, 0.0)], axis=0)


def _segment_cumsum(x, seg):
    row = lax.broadcasted_iota(jnp.int32, x.shape, 0) % seg
    shift = 1
    while shift < seg:
        x = x + jnp.where(row >= shift, pltpu.roll(x, shift, axis=0), 0.0)
        shift *= 2
    return x


def _expand_heads(narrow, e3_ref):
    lane = lax.broadcasted_iota(jnp.int32, narrow.shape, 1)
    x = jnp.where(lane < 2 * N_HEADS, narrow, 0.0)
    hi = x.astype(_BF).astype(_F32)
    rem = x - hi
    mid = rem.astype(_BF).astype(_F32)
    lo = (rem - mid).astype(_BF).astype(_F32)
    packed = hi + pltpu.roll(mid, 2 * N_HEADS, axis=1) + pltpu.roll(lo, 4 * N_HEADS, axis=1)
    wide = jnp.dot(packed.astype(_BF), e3_ref[...], preferred_element_type=_F32)
    return wide[:, 0:1024], wide[:, 1024:1536], wide[:, 1536:2560], wide[:, 2560:3072]


def _per_head_rsqrt_scale(x, eps, mean, post):
    outs = []
    for h in range(N_HEADS):
        xh = x[:, h * HEAD:(h + 1) * HEAD]
        ss = jnp.sum(xh * xh, axis=-1, keepdims=True)
        if mean:
            ss = ss * (1.0 / HEAD)
        outs.append(xh * (lax.rsqrt(ss + eps) * post))
    return jnp.concatenate(outs, axis=1)


def _gate_scalars(ba, alog_ref, dtb_ref, seg):
    beta = _sigmoid(ba)
    g = -jnp.exp(alog_ref[...]) * _softplus(ba + dtb_ref[...])
    gc = _segment_cumsum(g, seg)
    lane = lax.broadcasted_iota(jnp.int32, ba.shape, 1)
    return jnp.where(lane < N_HEADS, beta, gc)


def _chunk_masks(n_tok):
    row = lax.broadcasted_iota(jnp.int32, (CHUNK, 4 * HEAD), 0)
    col = lax.broadcasted_iota(jnp.int32, (CHUNK, 4 * HEAD), 1) % CHUNK
    same_seq = (row // n_tok) == (col // n_tok)
    return same_seq & (row >= col), same_seq & (row > col), row == col


def _chunks_intra(q, k, v, beta_e, beta_5, gc_e, gc_5, n_tok):
    n_chunks = q.shape[0] // CHUNK
    incl, strict, eye5 = _chunk_masks(n_tok)
    rows = [slice(c * CHUNK, (c + 1) * CHUNK) for c in range(n_chunks)]
    pair = [slice(p * 2 * HEAD, (p + 1) * 2 * HEAD) for p in range(N_PAIRS)]

    kk, qk = [], []
    for rs in rows:
        kk_parts, qk_parts = [], []
        for sl in pair:
            kp, qp = k[rs, sl], q[rs, sl]
            g2 = _dot_nt(jnp.concatenate([kp, qp], axis=0), _pair_block_diag(kp))
            kk_parts.append(g2[:CHUNK])
            qk_parts.append(g2[CHUNK:])
        kk.append(jnp.concatenate(kk_parts, axis=1))
        qk.append(jnp.concatenate(qk_parts, axis=1))

    dec, glast = [], []
    for rs in rows:
        g5 = gc_5[rs]
        gc_row = jnp.sum(jnp.where(eye5, g5, 0.0), axis=0, keepdims=True)
        dec.append(jnp.exp(jnp.where(incl, g5 - gc_row, NEG_BIG)))
        ge = gc_e[rs]
        glast.append(jnp.concatenate(
            [jnp.broadcast_to(ge[(s + 1) * n_tok - 1:(s + 1) * n_tok, :], (n_tok, D_MODEL))
             for s in range(CHUNK // n_tok)], axis=0))

    prob = [(c, grp) for c in range(n_chunks) for grp in range(2)]
    m, s, pw = {}, {}, {}
    for c, grp in prob:
        neg_l = jnp.where(strict[:, :256], -(beta_5[rows[c], grp * 256:(grp + 1) * 256]
                                              * dec[c][:, grp * 256:(grp + 1) * 256]
                                              * kk[c][:, grp * 256:(grp + 1) * 256]), 0.0)
        m[c, grp] = neg_l
        s[c, grp] = eye5[:, :256].astype(_F32) + neg_l
    for key in prob:
        pw[key] = _dot(m[key], _lane_block_diag(m[key], CHUNK))
    for _ in range(4):
        for key in prob:
            r = _dot(jnp.concatenate([pw[key], s[key]], axis=0), _lane_block_diag(pw[key], CHUNK))
            pw[key] = r[:CHUNK]
            s[key] = s[key] + r[CHUNK:]
    for key in prob:
        s[key] = s[key] + _dot(s[key], _lane_block_diag(pw[key], CHUNK))

    out = []
    for c, rs in enumerate(rows):
        gam = jnp.exp(gc_e[rs])
        be = beta_e[rs]
        bv = be * v[rs]
        gbk = be * gam * k[rs]
        wv_parts, wk_parts = [], []
        for p, sl in enumerate(pair):
            t_pair = s[c, p // 2][:, (p % 2) * HEAD:(p % 2 + 1) * HEAD]
            sol = _dot(t_pair, _pair_block_diag(jnp.concatenate([bv[:, sl], gbk[:, sl]], axis=1)))
            wv_parts.append(sol[:, :2 * HEAD])
            wk_parts.append(sol[:, 2 * HEAD:])
        out.append(dict(w_v=jnp.concatenate(wv_parts, axis=1), w_k=jnp.concatenate(wk_parts, axis=1),
                        qg=q[rs] * gam, kt=k[rs] * jnp.exp(glast[c] - gc_e[rs]),
                        qkd=qk[c] * dec[c], gl=jnp.exp(glast[c])))
    return out


def _intra_output(qkd, u):
    outs = []
    for p in range(N_PAIRS):
        up = u[:, p * 2 * HEAD:(p + 1) * 2 * HEAD]
        outs.append(_dot(qkd[:, p * HEAD:(p + 1) * HEAD], _pair_block_diag(up)))
    return jnp.concatenate(outs, axis=1)


def _state_to_rows(state_ref, ext_ref, row0, n_rows, n_seq):
    for s in range(n_seq):
        for r in range(n_rows):
            ext_ref[s, row0 + r:row0 + r + 1, :] = state_ref[r, s:s + 1, :]


def _rows_to_state(ext_ref, row0, n_rows, state_ref, seq0, n_seq):
    for s in range(n_seq):
        for r in range(n_rows):
            state_ref[r, seq0 + s:seq0 + s + 1, :] = ext_ref[s, row0 + r:row0 + r + 1, :]


def _modulated(x3, mod3):
    shift, scale = mod3[:, :, 0:D_MODEL], mod3[:, :, D_MODEL:2 * D_MODEL]
    return (x3 * (1.0 + scale) + shift).reshape(x3.shape[0] * x3.shape[1], D_MODEL).astype(_BF)


def _front(sec, pos, refs, extp_ref, extc_ref, n_seq, n_tok):
    rows = n_seq * n_tok
    extc_ref[:, CONV_PAD:CONV_PAD + n_tok, :] = sec("qkv").reshape(n_seq, n_tok, 3 * D_MODEL)

    seg_c = CONV_PAD + n_tok
    convw_ref = refs["convw"]
    e = extc_ref[...].reshape(n_seq * seg_c, 3 * D_MODEL)
    acc = e * convw_ref[0:1, :]
    for i in range(1, CONV_W):
        acc = pltpu.roll(acc, 1, axis=0) + e * convw_ref[i:i + 1, :]
    qkv = _silu(acc.reshape(n_seq, seg_c, 3 * D_MODEL)[:, CONV_PAD:, :].reshape(rows, 3 * D_MODEL))
    q = _per_head_rsqrt_scale(qkv[:, 0:D_MODEL], L2_EPS, False, HEAD ** -0.5)
    k = _per_head_rsqrt_scale(qkv[:, D_MODEL:2 * D_MODEL], L2_EPS, False, 1.0)
    v = qkv[:, 2 * D_MODEL:]

    extp_ref[:, POOL_PAD:POOL_PAD + n_tok, :] = sec("u_a").reshape(n_seq, n_tok, D_MODEL)
    z_a, ba, z_b, ga, gb = sec("z_a"), sec("ba"), sec("z_b"), sec("ga"), sec("gb")

    seg_p = POOL_PAD + n_tok
    pooled = []
    for gi, w in enumerate(POOL_WINDOWS):
        e = extp_ref[:, :, gi * POOL_GROUP:(gi + 1) * POOL_GROUP].reshape(n_seq * seg_p, POOL_GROUP)
        acc, span = e, 1
        while span < w:
            acc = acc + pltpu.roll(acc, span, axis=0)
            span *= 2
        take = lambda a: a.reshape(n_seq, seg_p, POOL_GROUP)[:, POOL_PAD:, :].reshape(rows, POOL_GROUP)
        inv_cnt = 1.0 / jnp.minimum(pos + 1, w).astype(_F32)
        pooled.append(take(acc) * inv_cnt - take(e))
    mixed = jnp.concatenate([_dot(pp, refs["poolw"][gi]) for gi, pp in enumerate(pooled)], axis=1)
    y_a = (mixed * refs["pscale"][...] * _silu(z_a)).astype(_BF)
    a_proj = _dot(y_a, refs["pa"][...])

    beta_e, beta_5, gc_e, gc_5 = _expand_heads(_gate_scalars(ba, refs["alog"], refs["dtb"], n_tok), refs["e3"])
    return dict(q=q, k=k, v=v, beta_e=beta_e, beta_5=beta_5, gc_e=gc_e, gc_5=gc_5,
                a_proj=a_proj, z_b=z_b, ga=ga, gb=gb)


def _back(x, o, f, gate, refs):
    o = _per_head_rsqrt_scale(o, RMS_EPS, True, refs["hnw"][...])
    y_b = (o * _silu(f["z_b"])).astype(_BF)
    merged = _sigmoid(f["ga"]) * f["a_proj"] + _sigmoid(f["gb"]) * _dot(y_b, refs["pb"][...])
    sub = (1.0 + gate) * _dot(merged, refs["wout"][...])
    r = DEEPNORM_ALPHA * x + sub
    mu = jnp.mean(r, axis=-1, keepdims=True)
    rc = r - mu
    var = jnp.mean(rc * rc, axis=-1, keepdims=True)
    return rc * lax.rsqrt(var + LN_EPS) * refs["lng"][...] + refs["lnb"][...]


_WEIGHT_NAMES = ("w_t", "convw", "alog", "dtb", "hnw", "poolw", "pscale", "pa", "pb",
                 "wout", "lng", "lnb", "e3")
_N_PROJ_WEIGHTS = 1
_SAMPLE_WEIGHT_NAMES = _WEIGHT_NAMES[_N_PROJ_WEIGHTS:]


def _ada_kernel(c_ref, w_ref, b_ref, o_ref):
    o_ref[...] = _dot(_silu(c_ref[...]), w_ref[...]) + b_ref[...]


def _ada_call(c_all, w_ada, b_ada):
    n = c_all.shape[0]
    return pl.pallas_call(
        _ada_kernel,
        grid=(3 * D_MODEL // ADA_COLS,),
        in_specs=[pl.BlockSpec((n, D_MODEL), lambda j: (0, 0)),
                  pl.BlockSpec((D_MODEL, ADA_COLS), lambda j: (0, j)),
                  pl.BlockSpec((1, ADA_COLS), lambda j: (0, j))],
        out_specs=pl.BlockSpec((n, ADA_COLS), lambda j: (0, j)),
        out_shape=jax.ShapeDtypeStruct((n, 3 * D_MODEL), _F32),
        compiler_params=pltpu.CompilerParams(dimension_semantics=("arbitrary",)),
        name="adaln_mod",
    )(c_all, w_ada, b_ada)


def _prompt_kernel(x_ref, mod_ref, xs_ref, mods_ref, *rest, n_seq, n_tok, n_steps, n_groups):
    refs = dict(zip(_WEIGHT_NAMES, rest[:len(_WEIGHT_NAMES)]))
    (y_ref, pool_out_ref, conv_out_ref, delta_out_ref, projs_ref,
     extp_ref, extc_ref, sbd_ref) = rest[len(_WEIGHT_NAMES):]
    t = pl.program_id(1)
    rows = n_seq * n_tok

    @pl.when(t == 0)
    def _():
        extp_ref[:, 0:POOL_PAD, :] = jnp.zeros((n_seq, POOL_PAD, D_MODEL), _F32)
        extc_ref[:, 0:CONV_PAD, :] = jnp.zeros((n_seq, CONV_PAD, 3 * D_MODEL), _F32)
        sbd_ref[...] = jnp.zeros(sbd_ref.shape, _F32)

    x3 = x_ref[...]
    mod3 = mod_ref[...]
    pos = t * n_tok + lax.broadcasted_iota(jnp.int32, (rows, 1), 0) % n_tok
    h = jnp.concatenate([_modulated(x3, mod3), _modulated(xs_ref[...], mods_ref[...])], axis=0)
    w_rows = dict(SEC, ba=(OFF_MAIN_END, OFF_MAIN_END + HEAD), ga=(OFF_GATE, OFF_GATE + D_MODEL),
                  gb=(OFF_GATE + D_MODEL, OFF_GATE + 2 * D_MODEL))
    w_t_ref = refs["w_t"]

    def sec(name):
        r0, r1 = w_rows[name]
        full = lax.dot_general(h, w_t_ref[r0:r1, :], (((1,), (1,)), ((), ())), preferred_element_type=_F32)
        projs_ref[:, SEC[name][0]:SEC[name][1]] = full[rows:]
        return full[:rows]

    f = _front(sec, pos, refs, extp_ref, extc_ref, n_seq, n_tok)
    chunks = _chunks_intra(f["q"], f["k"], f["v"], f["beta_e"], f["beta_5"], f["gc_e"], f["gc_5"], n_tok)

    pair = [slice(p * 2 * HEAD, (p + 1) * 2 * HEAD) for p in range(N_PAIRS)]
    row_bd = lax.broadcasted_iota(jnp.int32, (2 * HEAD, 2 * HEAD), 0) // HEAD
    col_bd = lax.broadcasted_iota(jnp.int32, (2 * HEAD, 2 * HEAD), 1) // HEAD
    same_head = row_bd == col_bd
    us, ois = [], []
    for c, ck in enumerate(chunks):
        u_parts, oi_parts = [], []
        for p, sl in enumerate(pair):
            r = _dot(jnp.concatenate([ck["w_k"][:, sl], ck["qg"][:, sl]], axis=0), sbd_ref[c * N_PAIRS + p])
            u_parts.append(ck["w_v"][:, sl] - r[:CHUNK])
            oi_parts.append(r[CHUNK:])
        us.append(jnp.concatenate(u_parts, axis=1))
        ois.append(jnp.concatenate(oi_parts, axis=1))
    o = jnp.concatenate([ois[c] + _intra_output(ck["qkd"], us[c]) for c, ck in enumerate(chunks)], axis=0)
    for c, ck in enumerate(chunks):
        for p, sl in enumerate(pair):
            upd = _dot(ck["kt"][:, sl].T, us[c][:, sl])
            i = c * N_PAIRS + p
            sbd_ref[i] = ck["gl"][0:1, sl] * sbd_ref[i] + jnp.where(same_head, upd, 0.0)

    gate = jnp.broadcast_to(mod3[:, :, 2 * D_MODEL:], (n_seq, n_tok, D_MODEL)).reshape(rows, D_MODEL)
    y = _back(x3.reshape(rows, D_MODEL), o, f, gate, refs)
    y_ref[...] = y.reshape(n_seq, n_tok, D_MODEL)

    @pl.when(t == n_steps - 1)
    def _():
        for s in range(n_seq):
            for hd in range(N_HEADS):
                o0 = (hd % 2) * HEAD
                delta_out_ref[s, hd] = sbd_ref[s * N_PAIRS + hd // 2, o0:o0 + HEAD, o0:o0 + HEAD]

    for grp in range(n_groups):
        @pl.when((t == n_steps - 1) & (pl.program_id(0) == grp))
        def _():
            _rows_to_state(extp_ref, POOL_PAD + n_tok - POOL_BUF, POOL_BUF, pool_out_ref, grp * n_seq, n_seq)
            _rows_to_state(extc_ref, CONV_PAD + n_tok - (CONV_W - 1), CONV_W - 1, conv_out_ref, grp * n_seq, n_seq)

    @pl.when(t < n_steps - 1)
    def _():
        extp_ref[:, 0:POOL_PAD, :] = extp_ref[:, n_tok:n_tok + POOL_PAD, :]
        extc_ref[:, 0:CONV_PAD, :] = extc_ref[:, n_tok:n_tok + CONV_PAD, :]


def _const_spec(shape):
    nd = len(shape)
    return pl.BlockSpec(shape, lambda *_: (0,) * nd, pipeline_mode=pl.Buffered(1))


def _weight_specs(weights):
    return [_const_spec(w.shape) for w in weights]


def _prompt_call(x, mod, xs, mods, weights, n_seq, n_tok):
    bsz, seq, _ = x.shape
    n_steps = seq // n_tok
    n_sample, s_tok, _ = xs.shape
    assert n_tok == CHUNK and bsz % n_seq == 0 and seq % n_tok == 0
    assert n_sample == SAMPLE_RIDE * (bsz // n_seq) * n_steps
    ride = lambda shape: pl.BlockSpec((SAMPLE_RIDE,) + shape, lambda g, t: (g * n_steps + t,) + (0,) * len(shape))
    kern = functools.partial(_prompt_kernel, n_seq=n_seq, n_tok=n_tok, n_steps=n_steps, n_groups=bsz // n_seq)
    seq_block = lambda shape: pl.BlockSpec((n_seq,) + shape, lambda g, t: (g,) + (0,) * len(shape))
    return pl.pallas_call(
        kern,
        grid=(bsz // n_seq, n_steps),
        in_specs=[pl.BlockSpec((n_seq, n_tok, D_MODEL), lambda g, t: (g, t, 0)),
                  seq_block((1, 3 * D_MODEL)), ride((s_tok, D_MODEL)), ride((1, 3 * D_MODEL))]
                 + _weight_specs(weights),
        out_specs=[pl.BlockSpec((n_seq, n_tok, D_MODEL), lambda g, t: (g, t, 0)),
                   pl.BlockSpec((POOL_BUF, bsz, D_MODEL), lambda g, t: (0, 0, 0)),
                   pl.BlockSpec((CONV_W - 1, bsz, 3 * D_MODEL), lambda g, t: (0, 0, 0)),
                   seq_block((N_HEADS, HEAD, HEAD)),
                   pl.BlockSpec((SAMPLE_RIDE * s_tok, SEC_WIDTH), lambda g, t: (g * n_steps + t, 0))],
        out_shape=[jax.ShapeDtypeStruct((bsz, seq, D_MODEL), _F32),
                   jax.ShapeDtypeStruct((POOL_BUF, bsz, D_MODEL), _F32),
                   jax.ShapeDtypeStruct((CONV_W - 1, bsz, 3 * D_MODEL), _F32),
                   jax.ShapeDtypeStruct((bsz, N_HEADS, HEAD, HEAD), _F32),
                   jax.ShapeDtypeStruct((n_sample * s_tok, SEC_WIDTH), _F32)],
        scratch_shapes=[pltpu.VMEM((n_seq, POOL_PAD + n_tok, D_MODEL), _F32),
                        pltpu.VMEM((n_seq, CONV_PAD + n_tok, 3 * D_MODEL), _F32),
                        pltpu.VMEM((n_seq * N_PAIRS, 2 * HEAD, 2 * HEAD), _F32)],
        compiler_params=pltpu.CompilerParams(dimension_semantics=("arbitrary", "arbitrary"),
                                             vmem_limit_bytes=VMEM_LIMIT_BYTES),
        name="prompt_layer",
    )(x, mod, xs, mods, *weights)


def _sample_kernel(x_ref, mod_ref, projs_ref, spool_ref, sconv_ref, sdelta_ref, *rest, n_seq, n_tok, pos0):
    refs = dict(zip(_SAMPLE_WEIGHT_NAMES, rest[:len(_SAMPLE_WEIGHT_NAMES)]))
    (y_ref, pool_out_ref, conv_out_ref, delta_out_ref, extp_ref, extc_ref) = rest[len(_SAMPLE_WEIGHT_NAMES):]
    rows = n_seq * n_tok

    _state_to_rows(spool_ref, extp_ref, POOL_PAD - POOL_BUF, POOL_BUF, n_seq)
    extp_ref[:, 0:POOL_PAD - POOL_BUF, :] = jnp.zeros((n_seq, POOL_PAD - POOL_BUF, D_MODEL), _F32)
    _state_to_rows(sconv_ref, extc_ref, CONV_PAD - (CONV_W - 1), CONV_W - 1, n_seq)
    extc_ref[:, 0:CONV_PAD - (CONV_W - 1), :] = jnp.zeros((n_seq, CONV_PAD - (CONV_W - 1), 3 * D_MODEL), _F32)

    x3 = x_ref[...]
    mod3 = mod_ref[...]
    pos = pos0 + lax.broadcasted_iota(jnp.int32, (rows, 1), 0) % n_tok
    f = _front(lambda name: projs_ref[:, SEC[name][0]:SEC[name][1]], pos, refs, extp_ref, extc_ref, n_seq, n_tok)
    _rows_to_state(extp_ref, POOL_PAD + n_tok - POOL_BUF, POOL_BUF, pool_out_ref, 0, n_seq)
    _rows_to_state(extc_ref, CONV_PAD + n_tok - (CONV_W - 1), CONV_W - 1, conv_out_ref, 0, n_seq)
    (ck,) = _chunks_intra(f["q"], f["k"], f["v"], f["beta_e"], f["beta_5"], f["gc_e"], f["gc_5"], n_tok)

    u_rows, oi_rows = [], []
    for s in range(n_seq):
        rs = slice(s * n_tok, (s + 1) * n_tok)
        u_h, oi_h = [], []
        for hd in range(N_HEADS):
            cs = slice(hd * HEAD, (hd + 1) * HEAD)
            r = _dot(jnp.concatenate([ck["w_k"][rs, cs], ck["qg"][rs, cs]], axis=0), sdelta_ref[s, hd])
            u_h.append(ck["w_v"][rs, cs] - r[:n_tok])
            oi_h.append(r[n_tok:])
        u_rows.append(jnp.concatenate(u_h, axis=1))
        oi_rows.append(jnp.concatenate(oi_h, axis=1))
    u = jnp.concatenate(u_rows, axis=0)
    o = jnp.concatenate(oi_rows, axis=0) + _intra_output(ck["qkd"], u)

    lane_seq = lax.broadcasted_iota(jnp.int32, (HEAD, rows), 1) // n_tok
    for hd in range(N_HEADS):
        cs = slice(hd * HEAD, (hd + 1) * HEAD)
        kt_t = ck["kt"][:, cs].T
        lhs = jnp.concatenate([jnp.where(lane_seq == s, kt_t, 0.0) for s in range(n_seq)], axis=0)
        upd = _dot(lhs, u[:, cs])
        for s in range(n_seq):
            gl = ck["gl"][s * n_tok:s * n_tok + 1, cs]
            delta_out_ref[s, hd] = gl * sdelta_ref[s, hd] + upd[s * HEAD:(s + 1) * HEAD]

    gate = jnp.broadcast_to(mod3[:, :, 2 * D_MODEL:], (n_seq, n_tok, D_MODEL)).reshape(rows, D_MODEL)
    y = _back(x3.reshape(rows, D_MODEL), o, f, gate, refs)
    y_ref[...] = y.reshape(n_seq, n_tok, D_MODEL)


def _sample_call(x, mod, projs, spool, sconv, sdelta, weights, pos0):
    bsz, n_tok, _ = x.shape
    n_seq = SAMPLE_SEQS
    assert n_seq * n_tok == CHUNK and bsz % n_seq == 0
    kern = functools.partial(_sample_kernel, n_seq=n_seq, n_tok=n_tok, pos0=pos0)
    seq_block = lambda shape: pl.BlockSpec((n_seq,) + shape, lambda i: (i,) + (0,) * len(shape))
    row_state = lambda n_rows, width: pl.BlockSpec((n_rows, n_seq, width), lambda i: (0, i, 0))
    return pl.pallas_call(
        kern,
        grid=(bsz // n_seq,),
        in_specs=[seq_block((n_tok, D_MODEL)), seq_block((1, 3 * D_MODEL)),
                  pl.BlockSpec((n_seq * n_tok, SEC_WIDTH), lambda i: (i, 0)),
                  row_state(POOL_BUF, D_MODEL), row_state(CONV_W - 1, 3 * D_MODEL),
                  seq_block((N_HEADS, HEAD, HEAD))] + [_const_spec(w.shape) for w in weights],
        out_specs=[seq_block((n_tok, D_MODEL)), row_state(POOL_BUF, D_MODEL),
                   row_state(CONV_W - 1, 3 * D_MODEL), seq_block((N_HEADS, HEAD, HEAD))],
        out_shape=[jax.ShapeDtypeStruct((bsz, n_tok, D_MODEL), _F32),
                   jax.ShapeDtypeStruct((POOL_BUF, bsz, D_MODEL), _F32),
                   jax.ShapeDtypeStruct((CONV_W - 1, bsz, 3 * D_MODEL), _F32),
                   jax.ShapeDtypeStruct((bsz, N_HEADS, HEAD, HEAD), _F32)],
        scratch_shapes=[pltpu.VMEM((n_seq, POOL_PAD + n_tok, D_MODEL), _F32),
                        pltpu.VMEM((n_seq, CONV_PAD + n_tok, 3 * D_MODEL), _F32)],
        compiler_params=pltpu.CompilerParams(dimension_semantics=("arbitrary",),
                                             vmem_limit_bytes=VMEM_LIMIT_BYTES),
        name="sample_layer",
    )(x, mod, projs, spool, sconv, sdelta, *weights)


def _head_expansion_matrix():
    e = np.zeros((HEAD, 3072), np.float32)
    for piece in range(3):
        for hd in range(N_HEADS):
            rb, rg = piece * 2 * N_HEADS + hd, piece * 2 * N_HEADS + N_HEADS + hd
            e[rb, hd * HEAD:(hd + 1) * HEAD] = 1.0
            e[rb, 1024 + hd * CHUNK:1024 + (hd + 1) * CHUNK] = 1.0
            e[rg, 1536 + hd * HEAD:1536 + (hd + 1) * HEAD] = 1.0
            e[rg, 2560 + hd * CHUNK:2560 + (hd + 1) * CHUNK] = 1.0
    return jnp.asarray(e, _BF)


def _layer_weights(w_in, conv_w, a_log, dt_bias, head_norm_w, pool_w, pool_scale, p_a, p_b, w_out, ln_g, ln_b):
    lane_pad = lambda a: jnp.zeros((1, HEAD), _F32).at[0, N_HEADS:2 * N_HEADS].set(a.reshape(N_HEADS))
    return (jnp.transpose(w_in).astype(_BF), conv_w,
            lane_pad(a_log), lane_pad(dt_bias), head_norm_w.reshape(1, HEAD), pool_w.astype(_BF),
            pool_scale.reshape(1, D_MODEL), p_a.astype(_BF), p_b.astype(_BF), w_out.astype(_BF),
            ln_g.reshape(1, D_MODEL), ln_b.reshape(1, D_MODEL), _head_expansion_matrix())


def kernel(x_prompt, x_sample, state_pool, state_conv, state_delta, c_prompt, c_sample, w_ada, b_ada, w_in, conv_w, a_log, dt_bias, head_norm_w, pool_w, pool_scale, p_a, p_b, w_out, ln_g, ln_b):
    assert w_in.shape[0] == 1, "single-layer trunk"
    bp, bs = x_prompt.shape[0], x_sample.shape[0]
    drop = lambda a: a.reshape(a.shape[1:])
    mod = _ada_call(jnp.concatenate([c_prompt, c_sample], axis=0), drop(w_ada), b_ada)
    mod = mod.reshape(bp + bs, 1, 3 * D_MODEL)
    weights = _layer_weights(drop(w_in), drop(conv_w), a_log, dt_bias, head_norm_w, drop(pool_w), pool_scale,
                             drop(p_a), drop(p_b), drop(w_out), ln_g, ln_b)
    rows_major = lambda a: jnp.transpose(a, (1, 0, 2))
    y_p, pool_p, conv_p, delta_p, proj_s = _prompt_call(x_prompt, mod[:bp], x_sample, mod[bp:], weights,
                                                        PROMPT_SEQS, PROMPT_TOKENS)
    y_s, pool_s, conv_s, delta_s = _sample_call(x_sample, mod[bp:], proj_s, rows_major(drop(state_pool)),
                                                rows_major(drop(state_conv)), drop(state_delta),
                                                weights[_N_PROJ_WEIGHTS:], PAST_LEN)
    lift = lambda a: a.reshape((1,) + a.shape)
    return (y_p, y_s, lift(rows_major(pool_p)), lift(rows_major(conv_p)), lift(delta_p),
            lift(rows_major(pool_s)), lift(rows_major(conv_s)), lift(delta_s))
```

```python
import functools

import numpy as np
import jax
import jax.numpy as jnp
from jax import lax
from jax.experimental import pallas as pl
from jax.experimental.pallas import tpu as pltpu

D_MODEL = 1024
N_HEADS = 8
HEAD = 128
N_PAIRS = N_HEADS // 2
POOL_WINDOWS = (2, 4, 8, 16)
POOL_GROUP = 256
POOL_BUF = 15
POOL_PAD = 16
CONV_W = 4
CONV_PAD = 8
CHUNK = 64
PAST_LEN = 16384
DEEPNORM_ALPHA = 2.0 ** 0.25
LN_EPS = 1e-5
RMS_EPS = 1e-6
L2_EPS = 1e-6
NEG_BIG = -1e30

OFF_MAIN_END = 6 * D_MODEL
OFF_GATE = OFF_MAIN_END + 2 * N_HEADS

SEC = dict(u_a=(0, D_MODEL), z_a=(D_MODEL, 2 * D_MODEL), qkv=(2 * D_MODEL, 5 * D_MODEL),
           z_b=(5 * D_MODEL, 6 * D_MODEL), ba=(OFF_MAIN_END, OFF_MAIN_END + 128),
           ga=(OFF_MAIN_END + 128, OFF_MAIN_END + 128 + D_MODEL),
           gb=(OFF_MAIN_END + 128 + D_MODEL, OFF_MAIN_END + 128 + 2 * D_MODEL))
SEC_WIDTH = OFF_MAIN_END + 128 + 2 * D_MODEL

PROMPT_SEQS = 4
SAMPLE_RIDE = 2
PROMPT_TOKENS = CHUNK
SAMPLE_SEQS = 8
ADA_COLS = 1024
VMEM_LIMIT_BYTES = 58 * 1024 * 1024

_BF = jnp.bfloat16
_F32 = jnp.float32


def _dot(a, b):
    return jnp.dot(a.astype(_BF), b.astype(_BF), preferred_element_type=_F32)


def _dot_nt(a, b):
    return lax.dot_general(a.astype(_BF), b.astype(_BF), (((1,), (1,)), ((), ())),
                           preferred_element_type=_F32)


def _sigmoid(x):
    return 0.5 + 0.5 * jnp.tanh(0.5 * x)


def _silu(x):
    hx = 0.5 * x
    return hx + hx * jnp.tanh(hx)


def _softplus(x):
    return jnp.maximum(x, 0.0) + jnp.log1p(jnp.exp(-jnp.abs(x)))


def _lane_block_diag(x, width):
    nblk = x.shape[1] // width
    lane = lax.broadcasted_iota(jnp.int32, x.shape, 1)
    return jnp.concatenate([jnp.where(lane // width == i, x, 0.0) for i in range(nblk)], axis=0)


def _pair_block_diag(x):
    lane = lax.broadcasted_iota(jnp.int32, x.shape, 1) % (2 * HEAD)
    return jnp.concatenate([jnp.where(lane < HEAD, x, 0.0), jnp.where(lane >= HEAD, x, 0.0)], axis=0)


def _segment_cumsum(x, seg):
    row = lax.broadcasted_iota(jnp.int32, x.shape, 0) % seg
    shift = 1
    while shift < seg:
        x = x + jnp.where(row >= shift, pltpu.roll(x, shift, axis=0), 0.0)
        shift *= 2
    return x


def _expand_heads(narrow, e3_ref):
    lane = lax.broadcasted_iota(jnp.int32, narrow.shape, 1)
    x = jnp.where(lane < 2 * N_HEADS, narrow, 0.0)
    hi = x.astype(_BF).astype(_F32)
    rem = x - hi
    mid = rem.astype(_BF).astype(_F32)
    lo = (rem - mid).astype(_BF).astype(_F32)
    packed = hi + pltpu.roll(mid, 2 * N_HEADS, axis=1) + pltpu.roll(lo, 4 * N_HEADS, axis=1)
    wide = jnp.dot(packed.astype(_BF), e3_ref[...], preferred_element_type=_F32)
    return wide[:, 0:1024], wide[:, 1024:1536], wide[:, 1536:2560], wide[:, 2560:3072]


def _per_head_rsqrt_scale(x, eps, mean, post):
    outs = []
    for h in range(N_HEADS):
        xh = x[:, h * HEAD:(h + 1) * HEAD]
        ss = jnp.sum(xh * xh, axis=-1, keepdims=True)
        if mean:
            ss = ss * (1.0 / HEAD)
        outs.append(xh * (lax.rsqrt(ss + eps) * post))
    return jnp.concatenate(outs, axis=1)


def _gate_scalars(ba, alog_ref, dtb_ref, seg):
    beta = _sigmoid(ba)
    g = -jnp.exp(alog_ref[...]) * _softplus(ba + dtb_ref[...])
    gc = _segment_cumsum(g, seg)
    lane = lax.broadcasted_iota(jnp.int32, ba.shape, 1)
    return jnp.where(lane < N_HEADS, beta, gc)


def _chunk_masks(n_tok):
    row = lax.broadcasted_iota(jnp.int32, (CHUNK, 4 * HEAD), 0)
    col = lax.broadcasted_iota(jnp.int32, (CHUNK, 4 * HEAD), 1) % CHUNK
    same_seq = (row // n_tok) == (col // n_tok)
    return same_seq & (row >= col), same_seq & (row > col), row == col


def _chunks_intra(q, k, v, beta_e, beta_5, gc_e, gc_5, n_tok):
    assert n_tok >= 4 and n_tok & (n_tok - 1) == 0 and CHUNK % n_tok == 0
    n_chunks = q.shape[0] // CHUNK
    incl, strict, eye5 = _chunk_masks(n_tok)
    rows = [slice(c * CHUNK, (c + 1) * CHUNK) for c in range(n_chunks)]
    pair = [slice(p * 2 * HEAD, (p + 1) * 2 * HEAD) for p in range(N_PAIRS)]

    kk, qk = [], []
    for rs in rows:
        kk_parts, qk_parts = [], []
        for sl in pair:
            kp, qp = k[rs, sl], q[rs, sl]
            g2 = _dot_nt(jnp.concatenate([kp, qp], axis=0), _pair_block_diag(kp))
            kk_parts.append(g2[:CHUNK])
            qk_parts.append(g2[CHUNK:])
        kk.append(jnp.concatenate(kk_parts, axis=1))
        qk.append(jnp.concatenate(qk_parts, axis=1))

    dec, glast = [], []
    for rs in rows:
        g5 = gc_5[rs]
        gc_row = jnp.sum(jnp.where(eye5, g5, 0.0), axis=0, keepdims=True)
        dec.append(jnp.exp(jnp.where(incl, g5 - gc_row, NEG_BIG)))
        ge = gc_e[rs]
        glast.append(jnp.concatenate(
            [jnp.broadcast_to(ge[(s + 1) * n_tok - 1:(s + 1) * n_tok, :], (n_tok, D_MODEL))
             for s in range(CHUNK // n_tok)], axis=0))

    prob = [(c, grp) for c in range(n_chunks) for grp in range(2)]
    m, s, pw = {}, {}, {}
    for c, grp in prob:
        neg_l = jnp.where(strict[:, :256], -(beta_5[rows[c], grp * 256:(grp + 1) * 256]
                                              * dec[c][:, grp * 256:(grp + 1) * 256]
                                              * kk[c][:, grp * 256:(grp + 1) * 256]), 0.0)
        m[c, grp] = neg_l
        s[c, grp] = eye5[:, :256].astype(_F32) + neg_l
    for key in prob:
        pw[key] = _dot(m[key], _lane_block_diag(m[key], CHUNK))
    for _ in range(n_tok.bit_length() - 3):
        for key in prob:
            r = _dot(jnp.concatenate([pw[key], s[key]], axis=0), _lane_block_diag(pw[key], CHUNK))
            pw[key] = r[:CHUNK]
            s[key] = s[key] + r[CHUNK:]
    for key in prob:
        s[key] = s[key] + _dot(s[key], _lane_block_diag(pw[key], CHUNK))

    out = []
    for c, rs in enumerate(rows):
        gam = jnp.exp(gc_e[rs])
        be = beta_e[rs]
        bv = be * v[rs]
        gbk = be * gam * k[rs]
        wv_parts, wk_parts = [], []
        for p, sl in enumerate(pair):
            t_pair = s[c, p // 2][:, (p % 2) * HEAD:(p % 2 + 1) * HEAD]
            sol = _dot(t_pair, _pair_block_diag(jnp.concatenate([bv[:, sl], gbk[:, sl]], axis=1)))
            wv_parts.append(sol[:, :2 * HEAD])
            wk_parts.append(sol[:, 2 * HEAD:])
        out.append(dict(w_v=jnp.concatenate(wv_parts, axis=1), w_k=jnp.concatenate(wk_parts, axis=1),
                        qg=q[rs] * gam, kt=k[rs] * jnp.exp(glast[c] - gc_e[rs]),
                        qkd=qk[c] * dec[c], gl=jnp.exp(glast[c])))
    return out


def _intra_output(qkd, u):
    outs = []
    for p in range(N_PAIRS):
        up = u[:, p * 2 * HEAD:(p + 1) * 2 * HEAD]
        outs.append(_dot(qkd[:, p * HEAD:(p + 1) * HEAD], _pair_block_diag(up)))
    return jnp.concatenate(outs, axis=1)


def _state_to_rows(state_ref, ext_ref, row0, n_rows, n_seq):
    for s in range(n_seq):
        for r in range(n_rows):
            ext_ref[s, row0 + r:row0 + r + 1, :] = state_ref[r, s:s + 1, :]


def _rows_to_state(ext_ref, row0, n_rows, state_ref, seq0, n_seq):
    for s in range(n_seq):
        for r in range(n_rows):
            state_ref[r, seq0 + s:seq0 + s + 1, :] = ext_ref[s, row0 + r:row0 + r + 1, :]


def _modulated(x3, mod3):
    shift, scale = mod3[:, :, 0:D_MODEL], mod3[:, :, D_MODEL:2 * D_MODEL]
    return (x3 * (1.0 + scale) + shift).reshape(x3.shape[0] * x3.shape[1], D_MODEL).astype(_BF)


def _front(sec, pos, refs, extp_ref, extc_ref, n_seq, n_tok):
    rows = n_seq * n_tok
    extc_ref[:, CONV_PAD:CONV_PAD + n_tok, :] = sec("qkv").reshape(n_seq, n_tok, 3 * D_MODEL)

    seg_c = CONV_PAD + n_tok
    convw_ref = refs["convw"]
    e = extc_ref[...].reshape(n_seq * seg_c, 3 * D_MODEL)
    acc = e * convw_ref[0:1, :]
    for i in range(1, CONV_W):
        acc = pltpu.roll(acc, 1, axis=0) + e * convw_ref[i:i + 1, :]
    qkv = _silu(acc.reshape(n_seq, seg_c, 3 * D_MODEL)[:, CONV_PAD:, :].reshape(rows, 3 * D_MODEL))
    q = _per_head_rsqrt_scale(qkv[:, 0:D_MODEL], L2_EPS, False, HEAD ** -0.5)
    k = _per_head_rsqrt_scale(qkv[:, D_MODEL:2 * D_MODEL], L2_EPS, False, 1.0)
    v = qkv[:, 2 * D_MODEL:]

    extp_ref[:, POOL_PAD:POOL_PAD + n_tok, :] = sec("u_a").reshape(n_seq, n_tok, D_MODEL)
    z_a, ba, z_b, ga, gb = sec("z_a"), sec("ba"), sec("z_b"), sec("ga"), sec("gb")

    seg_p = POOL_PAD + n_tok
    pooled = []
    for gi, w in enumerate(POOL_WINDOWS):
        e = extp_ref[:, :, gi * POOL_GROUP:(gi + 1) * POOL_GROUP].reshape(n_seq * seg_p, POOL_GROUP)
        acc, span = e, 1
        while span < w:
            acc = acc + pltpu.roll(acc, span, axis=0)
            span *= 2
        take = lambda a: a.reshape(n_seq, seg_p, POOL_GROUP)[:, POOL_PAD:, :].reshape(rows, POOL_GROUP)
        inv_cnt = 1.0 / jnp.minimum(pos + 1, w).astype(_F32)
        pooled.append(take(acc) * inv_cnt - take(e))
    mixed = jnp.concatenate([_dot(pp, refs["poolw"][gi]) for gi, pp in enumerate(pooled)], axis=1)
    y_a = (mixed * refs["pscale"][...] * _silu(z_a)).astype(_BF)
    a_proj = _dot(y_a, refs["pa"][...])

    beta_e, beta_5, gc_e, gc_5 = _expand_heads(_gate_scalars(ba, refs["alog"], refs["dtb"], n_tok), refs["e3"])
    return dict(q=q, k=k, v=v, beta_e=beta_e, beta_5=beta_5, gc_e=gc_e, gc_5=gc_5,
                a_proj=a_proj, z_b=z_b, ga=ga, gb=gb)


def _back(x, o, f, gate, refs):
    o = _per_head_rsqrt_scale(o, RMS_EPS, True, refs["hnw"][...])
    y_b = (o * _silu(f["z_b"])).astype(_BF)
    merged = _sigmoid(f["ga"]) * f["a_proj"] + _sigmoid(f["gb"]) * _dot(y_b, refs["pb"][...])
    sub = (1.0 + gate) * _dot(merged, refs["wout"][...])
    r = DEEPNORM_ALPHA * x + sub
    mu = jnp.mean(r, axis=-1, keepdims=True)
    rc = r - mu
    var = jnp.mean(rc * rc, axis=-1, keepdims=True)
    return rc * lax.rsqrt(var + LN_EPS) * refs["lng"][...] + refs["lnb"][...]


_WEIGHT_NAMES = ("w_t", "convw", "alog", "dtb", "hnw", "poolw", "pscale", "pa", "pb",
                 "wout", "lng", "lnb", "e3")
_N_PROJ_WEIGHTS = 1
_SAMPLE_WEIGHT_NAMES = _WEIGHT_NAMES[_N_PROJ_WEIGHTS:]


def _ada_kernel(c_ref, w_ref, b_ref, o_ref):
    o_ref[...] = _dot(_silu(c_ref[...]), w_ref[...]) + b_ref[...]


def _ada_call(c_all, w_ada, b_ada):
    n = c_all.shape[0]
    return pl.pallas_call(
        _ada_kernel,
        grid=(3 * D_MODEL // ADA_COLS,),
        in_specs=[pl.BlockSpec((n, D_MODEL), lambda j: (0, 0)),
                  pl.BlockSpec((D_MODEL, ADA_COLS), lambda j: (0, j)),
                  pl.BlockSpec((1, ADA_COLS), lambda j: (0, j))],
        out_specs=pl.BlockSpec((n, ADA_COLS), lambda j: (0, j)),
        out_shape=jax.ShapeDtypeStruct((n, 3 * D_MODEL), _F32),
        compiler_params=pltpu.CompilerParams(dimension_semantics=("arbitrary",)),
        name="adaln_mod",
    )(c_all, w_ada, b_ada)


def _prompt_kernel(x_ref, mod_ref, xs_ref, mods_ref, *rest, n_seq, n_tok, n_steps, n_groups):
    refs = dict(zip(_WEIGHT_NAMES, rest[:len(_WEIGHT_NAMES)]))
    (y_ref, pool_out_ref, conv_out_ref, delta_out_ref, projs_ref,
     extp_ref, extc_ref, sbd_ref) = rest[len(_WEIGHT_NAMES):]
    t = pl.program_id(1)
    rows = n_seq * n_tok

    @pl.when(t == 0)
    def _():
        extp_ref[:, 0:POOL_PAD, :] = jnp.zeros((n_seq, POOL_PAD, D_MODEL), _F32)
        extc_ref[:, 0:CONV_PAD, :] = jnp.zeros((n_seq, CONV_PAD, 3 * D_MODEL), _F32)
        sbd_ref[...] = jnp.zeros(sbd_ref.shape, _F32)

    x3 = x_ref[...]
    mod3 = mod_ref[...]
    pos = t * n_tok + lax.broadcasted_iota(jnp.int32, (rows, 1), 0) % n_tok
    h = jnp.concatenate([_modulated(x3, mod3), _modulated(xs_ref[...], mods_ref[...])], axis=0)
    w_rows = dict(SEC, ba=(OFF_MAIN_END, OFF_MAIN_END + HEAD), ga=(OFF_GATE, OFF_GATE + D_MODEL),
                  gb=(OFF_GATE + D_MODEL, OFF_GATE + 2 * D_MODEL))
    w_t_ref = refs["w_t"]

    def sec(name):
        r0, r1 = w_rows[name]
        full = lax.dot_general(h, w_t_ref[r0:r1, :], (((1,), (1,)), ((), ())), preferred_element_type=_F32)
        projs_ref[:, SEC[name][0]:SEC[name][1]] = full[rows:]
        return full[:rows]

    f = _front(sec, pos, refs, extp_ref, extc_ref, n_seq, n_tok)
    chunks = _chunks_intra(f["q"], f["k"], f["v"], f["beta_e"], f["beta_5"], f["gc_e"], f["gc_5"], n_tok)

    pair = [slice(p * 2 * HEAD, (p + 1) * 2 * HEAD) for p in range(N_PAIRS)]
    row_bd = lax.broadcasted_iota(jnp.int32, (2 * HEAD, 2 * HEAD), 0) // HEAD
    col_bd = lax.broadcasted_iota(jnp.int32, (2 * HEAD, 2 * HEAD), 1) // HEAD
    same_head = row_bd == col_bd
    us, ois = [], []
    for c, ck in enumerate(chunks):
        u_parts, oi_parts = [], []
        for p, sl in enumerate(pair):
            r = _dot(jnp.concatenate([ck["w_k"][:, sl], ck["qg"][:, sl]], axis=0), sbd_ref[c * N_PAIRS + p])
            u_parts.append(ck["w_v"][:, sl] - r[:CHUNK])
            oi_parts.append(r[CHUNK:])
        us.append(jnp.concatenate(u_parts, axis=1))
        ois.append(jnp.concatenate(oi_parts, axis=1))
    o = jnp.concatenate([ois[c] + _intra_output(ck["qkd"], us[c]) for c, ck in enumerate(chunks)], axis=0)
    for c, ck in enumerate(chunks):
        for p, sl in enumerate(pair):
            upd = _dot(ck["kt"][:, sl].T, us[c][:, sl])
            i = c * N_PAIRS + p
            sbd_ref[i] = ck["gl"][0:1, sl] * sbd_ref[i] + jnp.where(same_head, upd, 0.0)

    gate = jnp.broadcast_to(mod3[:, :, 2 * D_MODEL:], (n_seq, n_tok, D_MODEL)).reshape(rows, D_MODEL)
    y = _back(x3.reshape(rows, D_MODEL), o, f, gate, refs)
    y_ref[...] = y.reshape(n_seq, n_tok, D_MODEL)

    @pl.when(t == n_steps - 1)
    def _():
        for s in range(n_seq):
            for hd in range(N_HEADS):
                o0 = (hd % 2) * HEAD
                delta_out_ref[s, hd] = sbd_ref[s * N_PAIRS + hd // 2, o0:o0 + HEAD, o0:o0 + HEAD]

    for grp in range(n_groups):
        @pl.when((t == n_steps - 1) & (pl.program_id(0) == grp))
        def _():
            _rows_to_state(extp_ref, POOL_PAD + n_tok - POOL_BUF, POOL_BUF, pool_out_ref, grp * n_seq, n_seq)
            _rows_to_state(extc_ref, CONV_PAD + n_tok - (CONV_W - 1), CONV_W - 1, conv_out_ref, grp * n_seq, n_seq)

    @pl.when(t < n_steps - 1)
    def _():
        extp_ref[:, 0:POOL_PAD, :] = extp_ref[:, n_tok:n_tok + POOL_PAD, :]
        extc_ref[:, 0:CONV_PAD, :] = extc_ref[:, n_tok:n_tok + CONV_PAD, :]


def _const_spec(shape):
    nd = len(shape)
    return pl.BlockSpec(shape, lambda *_: (0,) * nd, pipeline_mode=pl.Buffered(1))


def _weight_specs(weights):
    return [_const_spec(w.shape) for w in weights]


def _prompt_call(x, mod, xs, mods, weights, n_seq, n_tok):
    bsz, seq, _ = x.shape
    n_steps = seq // n_tok
    n_sample, s_tok, _ = xs.shape
    assert n_tok == CHUNK and bsz % n_seq == 0 and seq % n_tok == 0
    assert n_sample == SAMPLE_RIDE * (bsz // n_seq) * n_steps
    ride = lambda shape: pl.BlockSpec((SAMPLE_RIDE,) + shape, lambda g, t: (g * n_steps + t,) + (0,) * len(shape))
    kern = functools.partial(_prompt_kernel, n_seq=n_seq, n_tok=n_tok, n_steps=n_steps, n_groups=bsz // n_seq)
    seq_block = lambda shape: pl.BlockSpec((n_seq,) + shape, lambda g, t: (g,) + (0,) * len(shape))
    return pl.pallas_call(
        kern,
        grid=(bsz // n_seq, n_steps),
        in_specs=[pl.BlockSpec((n_seq, n_tok, D_MODEL), lambda g, t: (g, t, 0)),
                  seq_block((1, 3 * D_MODEL)), ride((s_tok, D_MODEL)), ride((1, 3 * D_MODEL))]
                 + _weight_specs(weights),
        out_specs=[pl.BlockSpec((n_seq, n_tok, D_MODEL), lambda g, t: (g, t, 0)),
                   pl.BlockSpec((POOL_BUF, bsz, D_MODEL), lambda g, t: (0, 0, 0)),
                   pl.BlockSpec((CONV_W - 1, bsz, 3 * D_MODEL), lambda g, t: (0, 0, 0)),
                   seq_block((N_HEADS, HEAD, HEAD)),
                   pl.BlockSpec((SAMPLE_RIDE * s_tok, SEC_WIDTH), lambda g, t: (g * n_steps + t, 0))],
        out_shape=[jax.ShapeDtypeStruct((bsz, seq, D_MODEL), _F32),
                   jax.ShapeDtypeStruct((POOL_BUF, bsz, D_MODEL), _F32),
                   jax.ShapeDtypeStruct((CONV_W - 1, bsz, 3 * D_MODEL), _F32),
                   jax.ShapeDtypeStruct((bsz, N_HEADS, HEAD, HEAD), _F32),
                   jax.ShapeDtypeStruct((n_sample * s_tok, SEC_WIDTH), _F32)],
        scratch_shapes=[pltpu.VMEM((n_seq, POOL_PAD + n_tok, D_MODEL), _F32),
                        pltpu.VMEM((n_seq, CONV_PAD + n_tok, 3 * D_MODEL), _F32),
                        pltpu.VMEM((n_seq * N_PAIRS, 2 * HEAD, 2 * HEAD), _F32)],
        compiler_params=pltpu.CompilerParams(dimension_semantics=("arbitrary", "arbitrary"),
                                             vmem_limit_bytes=VMEM_LIMIT_BYTES),
        name="prompt_layer",
    )(x, mod, xs, mods, *weights)


def _sample_kernel(x_ref, mod_ref, projs_ref, spool_ref, sconv_ref, sdelta_ref, *rest, n_seq, n_tok, pos0):
    refs = dict(zip(_SAMPLE_WEIGHT_NAMES, rest[:len(_SAMPLE_WEIGHT_NAMES)]))
    (y_ref, pool_out_ref, conv_out_ref, delta_out_ref, extp_ref, extc_ref) = rest[len(_SAMPLE_WEIGHT_NAMES):]
    rows = n_seq * n_tok

    _state_to_rows(spool_ref, extp_ref, POOL_PAD - POOL_BUF, POOL_BUF, n_seq)
    extp_ref[:, 0:POOL_PAD - POOL_BUF, :] = jnp.zeros((n_seq, POOL_PAD - POOL_BUF, D_MODEL), _F32)
    _state_to_rows(sconv_ref, extc_ref, CONV_PAD - (CONV_W - 1), CONV_W - 1, n_seq)
    extc_ref[:, 0:CONV_PAD - (CONV_W - 1), :] = jnp.zeros((n_seq, CONV_PAD - (CONV_W - 1), 3 * D_MODEL), _F32)

    x3 = x_ref[...]
    mod3 = mod_ref[...]
    pos = pos0 + lax.broadcasted_iota(jnp.int32, (rows, 1), 0) % n_tok
    f = _front(lambda name: projs_ref[:, SEC[name][0]:SEC[name][1]], pos, refs, extp_ref, extc_ref, n_seq, n_tok)
    _rows_to_state(extp_ref, POOL_PAD + n_tok - POOL_BUF, POOL_BUF, pool_out_ref, 0, n_seq)
    _rows_to_state(extc_ref, CONV_PAD + n_tok - (CONV_W - 1), CONV_W - 1, conv_out_ref, 0, n_seq)
    (ck,) = _chunks_intra(f["q"], f["k"], f["v"], f["beta_e"], f["beta_5"], f["gc_e"], f["gc_5"], n_tok)

    u_rows, oi_rows = [], []
    for s in range(n_seq):
        rs = slice(s * n_tok, (s + 1) * n_tok)
        u_h, oi_h = [], []
        for hd in range(N_HEADS):
            cs = slice(hd * HEAD, (hd + 1) * HEAD)
            r = _dot(jnp.concatenate([ck["w_k"][rs, cs], ck["qg"][rs, cs]], axis=0), sdelta_ref[s, hd])
            u_h.append(ck["w_v"][rs, cs] - r[:n_tok])
            oi_h.append(r[n_tok:])
        u_rows.append(jnp.concatenate(u_h, axis=1))
        oi_rows.append(jnp.concatenate(oi_h, axis=1))
    u = jnp.concatenate(u_rows, axis=0)
    o = jnp.concatenate(oi_rows, axis=0) + _intra_output(ck["qkd"], u)

    lane_seq = lax.broadcasted_iota(jnp.int32, (HEAD, rows), 1) // n_tok
    for hd in range(N_HEADS):
        cs = slice(hd * HEAD, (hd + 1) * HEAD)
        kt_t = ck["kt"][:, cs].T
        lhs = jnp.concatenate([jnp.where(lane_seq == s, kt_t, 0.0) for s in range(n_seq)], axis=0)
        upd = _dot(lhs, u[:, cs])
        for s in range(n_seq):
            gl = ck["gl"][s * n_tok:s * n_tok + 1, cs]
            delta_out_ref[s, hd] = gl * sdelta_ref[s, hd] + upd[s * HEAD:(s + 1) * HEAD]

    gate = jnp.broadcast_to(mod3[:, :, 2 * D_MODEL:], (n_seq, n_tok, D_MODEL)).reshape(rows, D_MODEL)
    y = _back(x3.reshape(rows, D_MODEL), o, f, gate, refs)
    y_ref[...] = y.reshape(n_seq, n_tok, D_MODEL)


def _sample_call(x, mod, projs, spool, sconv, sdelta, weights, pos0):
    bsz, n_tok, _ = x.shape
    n_seq = SAMPLE_SEQS
    assert n_seq * n_tok == CHUNK and bsz % n_seq == 0
    kern = functools.partial(_sample_kernel, n_seq=n_seq, n_tok=n_tok, pos0=pos0)
    seq_block = lambda shape: pl.BlockSpec((n_seq,) + shape, lambda i: (i,) + (0,) * len(shape))
    row_state = lambda n_rows, width: pl.BlockSpec((n_rows, n_seq, width), lambda i: (0, i, 0))
    return pl.pallas_call(
        kern,
        grid=(bsz // n_seq,),
        in_specs=[seq_block((n_tok, D_MODEL)), seq_block((1, 3 * D_MODEL)),
                  pl.BlockSpec((n_seq * n_tok, SEC_WIDTH), lambda i: (i, 0)),
                  row_state(POOL_BUF, D_MODEL), row_state(CONV_W - 1, 3 * D_MODEL),
                  seq_block((N_HEADS, HEAD, HEAD))] + [_const_spec(w.shape) for w in weights],
        out_specs=[seq_block((n_tok, D_MODEL)), row_state(POOL_BUF, D_MODEL),
                   row_state(CONV_W - 1, 3 * D_MODEL), seq_block((N_HEADS, HEAD, HEAD))],
        out_shape=[jax.ShapeDtypeStruct((bsz, n_tok, D_MODEL), _F32),
                   jax.ShapeDtypeStruct((POOL_BUF, bsz, D_MODEL), _F32),
                   jax.ShapeDtypeStruct((CONV_W - 1, bsz, 3 * D_MODEL), _F32),
                   jax.ShapeDtypeStruct((bsz, N_HEADS, HEAD, HEAD), _F32)],
        scratch_shapes=[pltpu.VMEM((n_seq, POOL_PAD + n_tok, D_MODEL), _F32),
                        pltpu.VMEM((n_seq, CONV_PAD + n_tok, 3 * D_MODEL), _F32)],
        compiler_params=pltpu.CompilerParams(dimension_semantics=("arbitrary",),
                                             vmem_limit_bytes=VMEM_LIMIT_BYTES),
        name="sample_layer",
    )(x, mod, projs, spool, sconv, sdelta, *weights)


def _head_expansion_matrix():
    e = np.zeros((HEAD, 3072), np.float32)
    for piece in range(3):
        for hd in range(N_HEADS):
            rb, rg = piece * 2 * N_HEADS + hd, piece * 2 * N_HEADS + N_HEADS + hd
            e[rb, hd * HEAD:(hd + 1) * HEAD] = 1.0
            e[rb, 1024 + hd * CHUNK:1024 + (hd + 1) * CHUNK] = 1.0
            e[rg, 1536 + hd * HEAD:1536 + (hd + 1) * HEAD] = 1.0
            e[rg, 2560 + hd * CHUNK:2560 + (hd + 1) * CHUNK] = 1.0
    return jnp.asarray(e, _BF)


def _layer_weights(w_in, conv_w, a_log, dt_bias, head_norm_w, pool_w, pool_scale, p_a, p_b, w_out, ln_g, ln_b):
    lane_pad = lambda a: jnp.zeros((1, HEAD), _F32).at[0, N_HEADS:2 * N_HEADS].set(a.reshape(N_HEADS))
    return (jnp.transpose(w_in).astype(_BF), conv_w,
            lane_pad(a_log), lane_pad(dt_bias), head_norm_w.reshape(1, HEAD), pool_w.astype(_BF),
            pool_scale.reshape(1, D_MODEL), p_a.astype(_BF), p_b.astype(_BF), w_out.astype(_BF),
            ln_g.reshape(1, D_MODEL), ln_b.reshape(1, D_MODEL), _head_expansion_matrix())


def kernel(x_prompt, x_sample, state_pool, state_conv, state_delta, c_prompt, c_sample, w_ada, b_ada, w_in, conv_w, a_log, dt_bias, head_norm_w, pool_w, pool_scale, p_a, p_b, w_out, ln_g, ln_b):
    assert w_in.shape[0] == 1, "single-layer trunk"
    bp, bs = x_prompt.shape[0], x_sample.shape[0]
    drop = lambda a: a.reshape(a.shape[1:])
    mod = _ada_call(jnp.concatenate([c_prompt, c_sample], axis=0), drop(w_ada), b_ada)
    mod = mod.reshape(bp + bs, 1, 3 * D_MODEL)
    weights = _layer_weights(drop(w_in), drop(conv_w), a_log, dt_bias, head_norm_w, drop(pool_w), pool_scale,
                             drop(p_a), drop(p_b), drop(w_out), ln_g, ln_b)
    rows_major = lambda a: jnp.transpose(a, (1, 0, 2))
    y_p, pool_p, conv_p, delta_p, proj_s = _prompt_call(x_prompt, mod[:bp], x_sample, mod[bp:], weights,
                                                        PROMPT_SEQS, PROMPT_TOKENS)
    y_s, pool_s, conv_s, delta_s = _sample_call(x_sample, mod[bp:], proj_s, rows_major(drop(state_pool)),
                                                rows_major(drop(state_conv)), drop(state_delta),
                                                weights[_N_PROJ_WEIGHTS:], PAST_LEN)
    lift = lambda a: a.reshape((1,) + a.shape)
    return (y_p, y_s, lift(rows_major(pool_p)), lift(rows_major(conv_p)), lift(delta_p),
            lift(rows_major(pool_s)), lift(rows_major(conv_s)), lift(delta_s))
```

```python
import functools

import numpy as np
import jax
import jax.numpy as jnp
from jax import lax
from jax.experimental import pallas as pl
from jax.experimental.pallas import tpu as pltpu

D_MODEL = 1024
N_HEADS = 8
HEAD = 128
N_PAIRS = N_HEADS // 2
POOL_WINDOWS = (2, 4, 8, 16)
POOL_GROUP = 256
POOL_BUF = 15
POOL_PAD = 16
CONV_W = 4
CONV_PAD = 8
CHUNK = 64
INV_BASE = 8
PAST_LEN = 16384
DEEPNORM_ALPHA = 2.0 ** 0.25
LN_EPS = 1e-5
RMS_EPS = 1e-6
L2_EPS = 1e-6
NEG_BIG = -1e30

OFF_MAIN_END = 6 * D_MODEL
OFF_GATE = OFF_MAIN_END + 2 * N_HEADS

SEC = dict(u_a=(0, D_MODEL), z_a=(D_MODEL, 2 * D_MODEL), qkv=(2 * D_MODEL, 5 * D_MODEL),
           z_b=(5 * D_MODEL, 6 * D_MODEL), ba=(OFF_MAIN_END, OFF_MAIN_END + 128),
           ga=(OFF_MAIN_END + 128, OFF_MAIN_END + 128 + D_MODEL),
           gb=(OFF_MAIN_END + 128 + D_MODEL, OFF_MAIN_END + 128 + 2 * D_MODEL))
SEC_WIDTH = OFF_MAIN_END + 128 + 2 * D_MODEL

PROMPT_SEQS = 4
SAMPLE_RIDE = 2
PROMPT_TOKENS = CHUNK
SAMPLE_SEQS = 8
ADA_COLS = 1024
VMEM_LIMIT_BYTES = 58 * 1024 * 1024

_BF = jnp.bfloat16
_F32 = jnp.float32


def _dot(a, b):
    return jnp.dot(a.astype(_BF), b.astype(_BF), preferred_element_type=_F32)


def _dot_nt(a, b):
    return lax.dot_general(a.astype(_BF), b.astype(_BF), (((1,), (1,)), ((), ())),
                           preferred_element_type=_F32)


def _sigmoid(x):
    return 0.5 + 0.5 * jnp.tanh(0.5 * x)


def _silu(x):
    hx = 0.5 * x
    return hx + hx * jnp.tanh(hx)


def _softplus(x):
    return jnp.maximum(x, 0.0) + jnp.log1p(jnp.exp(-jnp.abs(x)))


def _lane_block_diag(x, width):
    nblk = x.shape[1] // width
    lane = lax.broadcasted_iota(jnp.int32, x.shape, 1)
    return jnp.concatenate([jnp.where(lane // width == i, x, 0.0) for i in range(nblk)], axis=0)


def _pair_block_diag(x):
    lane = lax.broadcasted_iota(jnp.int32, x.shape, 1) % (2 * HEAD)
    return jnp.concatenate([jnp.where(lane < HEAD, x, 0.0), jnp.where(lane >= HEAD, x, 0.0)], axis=0)


def _segment_cumsum(x, seg):
    row = lax.broadcasted_iota(jnp.int32, x.shape, 0) % seg
    shift = 1
    while shift < seg:
        x = x + jnp.where(row >= shift, pltpu.roll(x, shift, axis=0), 0.0)
        shift *= 2
    return x


def _expand_heads(narrow, e3_ref):
    lane = lax.broadcasted_iota(jnp.int32, narrow.shape, 1)
    x = jnp.where(lane < 2 * N_HEADS, narrow, 0.0)
    hi = x.astype(_BF).astype(_F32)
    rem = x - hi
    mid = rem.astype(_BF).astype(_F32)
    lo = (rem - mid).astype(_BF).astype(_F32)
    packed = hi + pltpu.roll(mid, 2 * N_HEADS, axis=1) + pltpu.roll(lo, 4 * N_HEADS, axis=1)
    wide = jnp.dot(packed.astype(_BF), e3_ref[...], preferred_element_type=_F32)
    return wide[:, 0:1024], wide[:, 1024:1536], wide[:, 1536:2560], wide[:, 2560:3072]


def _per_head_rsqrt_scale(x, eps, mean, post):
    outs = []
    for h in range(N_HEADS):
        xh = x[:, h * HEAD:(h + 1) * HEAD]
        ss = jnp.sum(xh * xh, axis=-1, keepdims=True)
        if mean:
            ss = ss * (1.0 / HEAD)
        outs.append(xh * (lax.rsqrt(ss + eps) * post))
    return jnp.concatenate(outs, axis=1)


def _gate_scalars(ba, alog_ref, dtb_ref, seg):
    beta = _sigmoid(ba)
    g = -jnp.exp(alog_ref[...]) * _softplus(ba + dtb_ref[...])
    gc = _segment_cumsum(g, seg)
    lane = lax.broadcasted_iota(jnp.int32, ba.shape, 1)
    return jnp.where(lane < N_HEADS, beta, gc)


def _chunk_masks(n_tok):
    row = lax.broadcasted_iota(jnp.int32, (CHUNK, 4 * HEAD), 0)
    col = lax.broadcasted_iota(jnp.int32, (CHUNK, 4 * HEAD), 1) % CHUNK
    same_seq = (row // n_tok) == (col // n_tok)
    return same_seq & (row >= col), same_seq & (row > col), row == col


def _chunks_intra(q, k, v, beta_e, beta_5, gc_e, gc_5, n_tok):
    assert n_tok >= INV_BASE and n_tok & (n_tok - 1) == 0 and CHUNK % n_tok == 0
    n_chunks = q.shape[0] // CHUNK
    incl, strict, eye5 = _chunk_masks(n_tok)
    rows = [slice(c * CHUNK, (c + 1) * CHUNK) for c in range(n_chunks)]
    pair = [slice(p * 2 * HEAD, (p + 1) * 2 * HEAD) for p in range(N_PAIRS)]

    kk, qk = [], []
    for rs in rows:
        kk_parts, qk_parts = [], []
        for sl in pair:
            kp, qp = k[rs, sl], q[rs, sl]
            g2 = _dot_nt(jnp.concatenate([kp, qp], axis=0), _pair_block_diag(kp))
            kk_parts.append(g2[:CHUNK])
            qk_parts.append(g2[CHUNK:])
        kk.append(jnp.concatenate(kk_parts, axis=1))
        qk.append(jnp.concatenate(qk_parts, axis=1))

    dec, glast = [], []
    for rs in rows:
        g5 = gc_5[rs]
        gc_row = jnp.sum(jnp.where(eye5, g5, 0.0), axis=0, keepdims=True)
        dec.append(jnp.exp(jnp.where(incl, g5 - gc_row, NEG_BIG)))
        ge = gc_e[rs]
        glast.append(jnp.concatenate(
            [jnp.broadcast_to(ge[(s + 1) * n_tok - 1:(s + 1) * n_tok, :], (n_tok, D_MODEL))
             for s in range(CHUNK // n_tok)], axis=0))

    row4 = lax.broadcasted_iota(jnp.int32, (CHUNK, 2 * N_PAIRS * CHUNK // 2), 0)
    col4 = lax.broadcasted_iota(jnp.int32, (CHUNK, 2 * N_PAIRS * CHUNK // 2), 1) % CHUNK
    base_blk = (row4 // INV_BASE) == (col4 // INV_BASE)
    prob = [(c, grp) for c in range(n_chunks) for grp in range(2)]
    neg_l, t_inv, pw = {}, {}, {}
    for c, grp in prob:
        gs = slice(grp * 256, (grp + 1) * 256)
        neg_l[c, grp] = jnp.where(strict[:, :256], -(beta_5[rows[c], gs] * dec[c][:, gs] * kk[c][:, gs]), 0.0)
    for key in prob:
        m0 = jnp.where(base_blk, neg_l[key], 0.0)
        t_inv[key] = eye5[:, :256].astype(_F32) + m0
        pw[key] = _dot(m0, _lane_block_diag(m0, CHUNK))
    for key in prob:
        r = _dot(jnp.concatenate([pw[key], t_inv[key]], axis=0), _lane_block_diag(pw[key], CHUNK))
        pw[key] = r[:CHUNK]
        t_inv[key] = t_inv[key] + r[CHUNK:]
    for key in prob:
        t_inv[key] = t_inv[key] + _dot(t_inv[key], _lane_block_diag(pw[key], CHUNK))
    blk = INV_BASE
    while blk < n_tok:
        below = ((row4 // (2 * blk)) == (col4 // (2 * blk))) & ((row4 // blk) != (col4 // blk))
        for key in prob:
            pw[key] = _dot(jnp.where(below, neg_l[key], 0.0), _lane_block_diag(t_inv[key], CHUNK))
        for key in prob:
            t_inv[key] = t_inv[key] + _dot(t_inv[key], _lane_block_diag(pw[key], CHUNK))
        blk *= 2
    s = t_inv

    out = []
    for c, rs in enumerate(rows):
        gam = jnp.exp(gc_e[rs])
        be = beta_e[rs]
        bv = be * v[rs]
        gbk = be * gam * k[rs]
        wv_parts, wk_parts = [], []
        for p, sl in enumerate(pair):
            t_pair = s[c, p // 2][:, (p % 2) * HEAD:(p % 2 + 1) * HEAD]
            sol = _dot(t_pair, _pair_block_diag(jnp.concatenate([bv[:, sl], gbk[:, sl]], axis=1)))
            wv_parts.append(sol[:, :2 * HEAD])
            wk_parts.append(sol[:, 2 * HEAD:])
        out.append(dict(w_v=jnp.concatenate(wv_parts, axis=1), w_k=jnp.concatenate(wk_parts, axis=1),
                        qg=q[rs] * gam, kt=k[rs] * jnp.exp(glast[c] - gc_e[rs]),
                        qkd=qk[c] * dec[c], gl=jnp.exp(glast[c])))
    return out


def _intra_output(qkd, u):
    outs = []
    for p in range(N_PAIRS):
        up = u[:, p * 2 * HEAD:(p + 1) * 2 * HEAD]
        outs.append(_dot(qkd[:, p * HEAD:(p + 1) * HEAD], _pair_block_diag(up)))
    return jnp.concatenate(outs, axis=1)


def _state_to_rows(state_ref, ext_ref, row0, n_rows, n_seq):
    for s in range(n_seq):
        for r in range(n_rows):
            ext_ref[s, row0 + r:row0 + r + 1, :] = state_ref[r, s:s + 1, :]


def _rows_to_state(ext_ref, row0, n_rows, state_ref, seq0, n_seq):
    for s in range(n_seq):
        for r in range(n_rows):
            state_ref[r, seq0 + s:seq0 + s + 1, :] = ext_ref[s, row0 + r:row0 + r + 1, :]


def _modulated(x3, mod3):
    shift, scale = mod3[:, :, 0:D_MODEL], mod3[:, :, D_MODEL:2 * D_MODEL]
    return (x3 * (1.0 + scale) + shift).reshape(x3.shape[0] * x3.shape[1], D_MODEL).astype(_BF)


def _front(sec, pos, refs, extp_ref, extc_ref, n_seq, n_tok):
    rows = n_seq * n_tok
    extc_ref[:, CONV_PAD:CONV_PAD + n_tok, :] = sec("qkv").reshape(n_seq, n_tok, 3 * D_MODEL)

    seg_c = CONV_PAD + n_tok
    convw_ref = refs["convw"]
    e = extc_ref[...].reshape(n_seq * seg_c, 3 * D_MODEL)
    acc = e * convw_ref[0:1, :]
    for i in range(1, CONV_W):
        acc = pltpu.roll(acc, 1, axis=0) + e * convw_ref[i:i + 1, :]
    qkv = _silu(acc.reshape(n_seq, seg_c, 3 * D_MODEL)[:, CONV_PAD:, :].reshape(rows, 3 * D_MODEL))
    q = _per_head_rsqrt_scale(qkv[:, 0:D_MODEL], L2_EPS, False, HEAD ** -0.5)
    k = _per_head_rsqrt_scale(qkv[:, D_MODEL:2 * D_MODEL], L2_EPS, False, 1.0)
    v = qkv[:, 2 * D_MODEL:]

    extp_ref[:, POOL_PAD:POOL_PAD + n_tok, :] = sec("u_a").reshape(n_seq, n_tok, D_MODEL)
    z_a, ba, z_b, ga, gb = sec("z_a"), sec("ba"), sec("z_b"), sec("ga"), sec("gb")

    seg_p = POOL_PAD + n_tok
    pooled = []
    for gi, w in enumerate(POOL_WINDOWS):
        e = extp_ref[:, :, gi * POOL_GROUP:(gi + 1) * POOL_GROUP].reshape(n_seq * seg_p, POOL_GROUP)
        acc, span = e, 1
        while span < w:
            acc = acc + pltpu.roll(acc, span, axis=0)
            span *= 2
        take = lambda a: a.reshape(n_seq, seg_p, POOL_GROUP)[:, POOL_PAD:, :].reshape(rows, POOL_GROUP)
        inv_cnt = 1.0 / jnp.minimum(pos + 1, w).astype(_F32)
        pooled.append(take(acc) * inv_cnt - take(e))
    mixed = jnp.concatenate([_dot(pp, refs["poolw"][gi]) for gi, pp in enumerate(pooled)], axis=1)
    y_a = (mixed * refs["pscale"][...] * _silu(z_a)).astype(_BF)
    a_proj = _dot(y_a, refs["pa"][...])

    beta_e, beta_5, gc_e, gc_5 = _expand_heads(_gate_scalars(ba, refs["alog"], refs["dtb"], n_tok), refs["e3"])
    return dict(q=q, k=k, v=v, beta_e=beta_e, beta_5=beta_5, gc_e=gc_e, gc_5=gc_5,
                a_proj=a_proj, z_b=z_b, ga=ga, gb=gb)


def _back(x, o, f, gate, refs):
    o = _per_head_rsqrt_scale(o, RMS_EPS, True, refs["hnw"][...])
    y_b = (o * _silu(f["z_b"])).astype(_BF)
    merged = _sigmoid(f["ga"]) * f["a_proj"] + _sigmoid(f["gb"]) * _dot(y_b, refs["pb"][...])
    sub = (1.0 + gate) * _dot(merged, refs["wout"][...])
    r = DEEPNORM_ALPHA * x + sub
    mu = jnp.mean(r, axis=-1, keepdims=True)
    rc = r - mu
    var = jnp.mean(rc * rc, axis=-1, keepdims=True)
    return rc * lax.rsqrt(var + LN_EPS) * refs["lng"][...] + refs["lnb"][...]


_WEIGHT_NAMES = ("w_t", "convw", "alog", "dtb", "hnw", "poolw", "pscale", "pa", "pb",
                 "wout", "lng", "lnb", "e3")
_N_PROJ_WEIGHTS = 1
_SAMPLE_WEIGHT_NAMES = _WEIGHT_NAMES[_N_PROJ_WEIGHTS:]


def _ada_kernel(c_ref, w_ref, b_ref, o_ref):
    o_ref[...] = _dot(_silu(c_ref[...]), w_ref[...]) + b_ref[...]


def _ada_call(c_all, w_ada, b_ada):
    n = c_all.shape[0]
    return pl.pallas_call(
        _ada_kernel,
        grid=(3 * D_MODEL // ADA_COLS,),
        in_specs=[pl.BlockSpec((n, D_MODEL), lambda j: (0, 0)),
                  pl.BlockSpec((D_MODEL, ADA_COLS), lambda j: (0, j)),
                  pl.BlockSpec((1, ADA_COLS), lambda j: (0, j))],
        out_specs=pl.BlockSpec((n, ADA_COLS), lambda j: (0, j)),
        out_shape=jax.ShapeDtypeStruct((n, 3 * D_MODEL), _F32),
        compiler_params=pltpu.CompilerParams(dimension_semantics=("arbitrary",)),
        name="adaln_mod",
    )(c_all, w_ada, b_ada)


def _prompt_kernel(x_ref, mod_ref, xs_ref, mods_ref, *rest, n_seq, n_tok, n_steps, n_groups):
    refs = dict(zip(_WEIGHT_NAMES, rest[:len(_WEIGHT_NAMES)]))
    (y_ref, pool_out_ref, conv_out_ref, delta_out_ref, projs_ref,
     extp_ref, extc_ref, sbd_ref) = rest[len(_WEIGHT_NAMES):]
    t = pl.program_id(1)
    rows = n_seq * n_tok

    @pl.when(t == 0)
    def _():
        extp_ref[:, 0:POOL_PAD, :] = jnp.zeros((n_seq, POOL_PAD, D_MODEL), _F32)
        extc_ref[:, 0:CONV_PAD, :] = jnp.zeros((n_seq, CONV_PAD, 3 * D_MODEL), _F32)
        sbd_ref[...] = jnp.zeros(sbd_ref.shape, _F32)

    x3 = x_ref[...]
    mod3 = mod_ref[...]
    pos = t * n_tok + lax.broadcasted_iota(jnp.int32, (rows, 1), 0) % n_tok
    h = jnp.concatenate([_modulated(x3, mod3), _modulated(xs_ref[...], mods_ref[...])], axis=0)
    w_rows = dict(SEC, ba=(OFF_MAIN_END, OFF_MAIN_END + HEAD), ga=(OFF_GATE, OFF_GATE + D_MODEL),
                  gb=(OFF_GATE + D_MODEL, OFF_GATE + 2 * D_MODEL))
    w_t_ref = refs["w_t"]

    def sec(name):
        r0, r1 = w_rows[name]
        full = lax.dot_general(h, w_t_ref[r0:r1, :], (((1,), (1,)), ((), ())), preferred_element_type=_F32)
        projs_ref[:, SEC[name][0]:SEC[name][1]] = full[rows:]
        return full[:rows]

    f = _front(sec, pos, refs, extp_ref, extc_ref, n_seq, n_tok)
    chunks = _chunks_intra(f["q"], f["k"], f["v"], f["beta_e"], f["beta_5"], f["gc_e"], f["gc_5"], n_tok)

    pair = [slice(p * 2 * HEAD, (p + 1) * 2 * HEAD) for p in range(N_PAIRS)]
    row_bd = lax.broadcasted_iota(jnp.int32, (2 * HEAD, 2 * HEAD), 0) // HEAD
    col_bd = lax.broadcasted_iota(jnp.int32, (2 * HEAD, 2 * HEAD), 1) // HEAD
    same_head = row_bd == col_bd
    us, ois = [], []
    for c, ck in enumerate(chunks):
        u_parts, oi_parts = [], []
        for p, sl in enumerate(pair):
            r = _dot(jnp.concatenate([ck["w_k"][:, sl], ck["qg"][:, sl]], axis=0), sbd_ref[c * N_PAIRS + p])
            u_parts.append(ck["w_v"][:, sl] - r[:CHUNK])
            oi_parts.append(r[CHUNK:])
        us.append(jnp.concatenate(u_parts, axis=1))
        ois.append(jnp.concatenate(oi_parts, axis=1))
    o = jnp.concatenate([ois[c] + _intra_output(ck["qkd"], us[c]) for c, ck in enumerate(chunks)], axis=0)
    for c, ck in enumerate(chunks):
        for p, sl in enumerate(pair):
            upd = _dot(ck["kt"][:, sl].T, us[c][:, sl])
            i = c * N_PAIRS + p
            sbd_ref[i] = ck["gl"][0:1, sl] * sbd_ref[i] + jnp.where(same_head, upd, 0.0)

    gate = jnp.broadcast_to(mod3[:, :, 2 * D_MODEL:], (n_seq, n_tok, D_MODEL)).reshape(rows, D_MODEL)
    y = _back(x3.reshape(rows, D_MODEL), o, f, gate, refs)
    y_ref[...] = y.reshape(n_seq, n_tok, D_MODEL)

    @pl.when(t == n_steps - 1)
    def _():
        for s in range(n_seq):
            for hd in range(N_HEADS):
                o0 = (hd % 2) * HEAD
                delta_out_ref[s, hd] = sbd_ref[s * N_PAIRS + hd // 2, o0:o0 + HEAD, o0:o0 + HEAD]

    for grp in range(n_groups):
        @pl.when((t == n_steps - 1) & (pl.program_id(0) == grp))
        def _():
            _rows_to_state(extp_ref, POOL_PAD + n_tok - POOL_BUF, POOL_BUF, pool_out_ref, grp * n_seq, n_seq)
            _rows_to_state(extc_ref, CONV_PAD + n_tok - (CONV_W - 1), CONV_W - 1, conv_out_ref, grp * n_seq, n_seq)

    @pl.when(t < n_steps - 1)
    def _():
        extp_ref[:, 0:POOL_PAD, :] = extp_ref[:, n_tok:n_tok + POOL_PAD, :]
        extc_ref[:, 0:CONV_PAD, :] = extc_ref[:, n_tok:n_tok + CONV_PAD, :]


def _const_spec(shape):
    nd = len(shape)
    return pl.BlockSpec(shape, lambda *_: (0,) * nd, pipeline_mode=pl.Buffered(1))


def _weight_specs(weights):
    return [_const_spec(w.shape) for w in weights]


def _prompt_call(x, mod, xs, mods, weights, n_seq, n_tok):
    bsz, seq, _ = x.shape
    n_steps = seq // n_tok
    n_sample, s_tok, _ = xs.shape
    assert n_tok == CHUNK and bsz % n_seq == 0 and seq % n_tok == 0
    assert n_sample == SAMPLE_RIDE * (bsz // n_seq) * n_steps
    ride = lambda shape: pl.BlockSpec((SAMPLE_RIDE,) + shape, lambda g, t: (g * n_steps + t,) + (0,) * len(shape))
    kern = functools.partial(_prompt_kernel, n_seq=n_seq, n_tok=n_tok, n_steps=n_steps, n_groups=bsz // n_seq)
    seq_block = lambda shape: pl.BlockSpec((n_seq,) + shape, lambda g, t: (g,) + (0,) * len(shape))
    return pl.pallas_call(
        kern,
        grid=(bsz // n_seq, n_steps),
        in_specs=[pl.BlockSpec((n_seq, n_tok, D_MODEL), lambda g, t: (g, t, 0)),
                  seq_block((1, 3 * D_MODEL)), ride((s_tok, D_MODEL)), ride((1, 3 * D_MODEL))]
                 + _weight_specs(weights),
        out_specs=[pl.BlockSpec((n_seq, n_tok, D_MODEL), lambda g, t: (g, t, 0)),
                   pl.BlockSpec((POOL_BUF, bsz, D_MODEL), lambda g, t: (0, 0, 0)),
                   pl.BlockSpec((CONV_W - 1, bsz, 3 * D_MODEL), lambda g, t: (0, 0, 0)),
                   seq_block((N_HEADS, HEAD, HEAD)),
                   pl.BlockSpec((SAMPLE_RIDE * s_tok, SEC_WIDTH), lambda g, t: (g * n_steps + t, 0))],
        out_shape=[jax.ShapeDtypeStruct((bsz, seq, D_MODEL), _F32),
                   jax.ShapeDtypeStruct((POOL_BUF, bsz, D_MODEL), _F32),
                   jax.ShapeDtypeStruct((CONV_W - 1, bsz, 3 * D_MODEL), _F32),
                   jax.ShapeDtypeStruct((bsz, N_HEADS, HEAD, HEAD), _F32),
                   jax.ShapeDtypeStruct((n_sample * s_tok, SEC_WIDTH), _F32)],
        scratch_shapes=[pltpu.VMEM((n_seq, POOL_PAD + n_tok, D_MODEL), _F32),
                        pltpu.VMEM((n_seq, CONV_PAD + n_tok, 3 * D_MODEL), _F32),
                        pltpu.VMEM((n_seq * N_PAIRS, 2 * HEAD, 2 * HEAD), _F32)],
        compiler_params=pltpu.CompilerParams(dimension_semantics=("arbitrary", "arbitrary"),
                                             vmem_limit_bytes=VMEM_LIMIT_BYTES),
        name="prompt_layer",
    )(x, mod, xs, mods, *weights)


def _sample_kernel(x_ref, mod_ref, projs_ref, spool_ref, sconv_ref, sdelta_ref, *rest, n_seq, n_tok, pos0):
    refs = dict(zip(_SAMPLE_WEIGHT_NAMES, rest[:len(_SAMPLE_WEIGHT_NAMES)]))
    (y_ref, pool_out_ref, conv_out_ref, delta_out_ref, extp_ref, extc_ref) = rest[len(_SAMPLE_WEIGHT_NAMES):]
    rows = n_seq * n_tok

    _state_to_rows(spool_ref, extp_ref, POOL_PAD - POOL_BUF, POOL_BUF, n_seq)
    extp_ref[:, 0:POOL_PAD - POOL_BUF, :] = jnp.zeros((n_seq, POOL_PAD - POOL_BUF, D_MODEL), _F32)
    _state_to_rows(sconv_ref, extc_ref, CONV_PAD - (CONV_W - 1), CONV_W - 1, n_seq)
    extc_ref[:, 0:CONV_PAD - (CONV_W - 1), :] = jnp.zeros((n_seq, CONV_PAD - (CONV_W - 1), 3 * D_MODEL), _F32)

    x3 = x_ref[...]
    mod3 = mod_ref[...]
    pos = pos0 + lax.broadcasted_iota(jnp.int32, (rows, 1), 0) % n_tok
    f = _front(lambda name: projs_ref[:, SEC[name][0]:SEC[name][1]], pos, refs, extp_ref, extc_ref, n_seq, n_tok)
    _rows_to_state(extp_ref, POOL_PAD + n_tok - POOL_BUF, POOL_BUF, pool_out_ref, 0, n_seq)
    _rows_to_state(extc_ref, CONV_PAD + n_tok - (CONV_W - 1), CONV_W - 1, conv_out_ref, 0, n_seq)
    (ck,) = _chunks_intra(f["q"], f["k"], f["v"], f["beta_e"], f["beta_5"], f["gc_e"], f["gc_5"], n_tok)

    u_rows, oi_rows = [], []
    for s in range(n_seq):
        rs = slice(s * n_tok, (s + 1) * n_tok)
        u_h, oi_h = [], []
        for hd in range(N_HEADS):
            cs = slice(hd * HEAD, (hd + 1) * HEAD)
            r = _dot(jnp.concatenate([ck["w_k"][rs, cs], ck["qg"][rs, cs]], axis=0), sdelta_ref[s, hd])
            u_h.append(ck["w_v"][rs, cs] - r[:n_tok])
            oi_h.append(r[n_tok:])
        u_rows.append(jnp.concatenate(u_h, axis=1))
        oi_rows.append(jnp.concatenate(oi_h, axis=1))
    u = jnp.concatenate(u_rows, axis=0)
    o = jnp.concatenate(oi_rows, axis=0) + _intra_output(ck["qkd"], u)

    lane_seq = lax.broadcasted_iota(jnp.int32, (HEAD, rows), 1) // n_tok
    for hd in range(N_HEADS):
        cs = slice(hd * HEAD, (hd + 1) * HEAD)
        kt_t = ck["kt"][:, cs].T
        lhs = jnp.concatenate([jnp.where(lane_seq == s, kt_t, 0.0) for s in range(n_seq)], axis=0)
        upd = _dot(lhs, u[:, cs])
        for s in range(n_seq):
            gl = ck["gl"][s * n_tok:s * n_tok + 1, cs]
            delta_out_ref[s, hd] = gl * sdelta_ref[s, hd] + upd[s * HEAD:(s + 1) * HEAD]

    gate = jnp.broadcast_to(mod3[:, :, 2 * D_MODEL:], (n_seq, n_tok, D_MODEL)).reshape(rows, D_MODEL)
    y = _back(x3.reshape(rows, D_MODEL), o, f, gate, refs)
    y_ref[...] = y.reshape(n_seq, n_tok, D_MODEL)


def _sample_call(x, mod, projs, spool, sconv, sdelta, weights, pos0):
    bsz, n_tok, _ = x.shape
    n_seq = SAMPLE_SEQS
    assert n_seq * n_tok == CHUNK and bsz % n_seq == 0
    kern = functools.partial(_sample_kernel, n_seq=n_seq, n_tok=n_tok, pos0=pos0)
    seq_block = lambda shape: pl.BlockSpec((n_seq,) + shape, lambda i: (i,) + (0,) * len(shape))
    row_state = lambda n_rows, width: pl.BlockSpec((n_rows, n_seq, width), lambda i: (0, i, 0))
    return pl.pallas_call(
        kern,
        grid=(bsz // n_seq,),
        in_specs=[seq_block((n_tok, D_MODEL)), seq_block((1, 3 * D_MODEL)),
                  pl.BlockSpec((n_seq * n_tok, SEC_WIDTH), lambda i: (i, 0)),
                  row_state(POOL_BUF, D_MODEL), row_state(CONV_W - 1, 3 * D_MODEL),
                  seq_block((N_HEADS, HEAD, HEAD))] + [_const_spec(w.shape) for w in weights],
        out_specs=[seq_block((n_tok, D_MODEL)), row_state(POOL_BUF, D_MODEL),
                   row_state(CONV_W - 1, 3 * D_MODEL), seq_block((N_HEADS, HEAD, HEAD))],
        out_shape=[jax.ShapeDtypeStruct((bsz, n_tok, D_MODEL), _F32),
                   jax.ShapeDtypeStruct((POOL_BUF, bsz, D_MODEL), _F32),
                   jax.ShapeDtypeStruct((CONV_W - 1, bsz, 3 * D_MODEL), _F32),
                   jax.ShapeDtypeStruct((bsz, N_HEADS, HEAD, HEAD), _F32)],
        scratch_shapes=[pltpu.VMEM((n_seq, POOL_PAD + n_tok, D_MODEL), _F32),
                        pltpu.VMEM((n_seq, CONV_PAD + n_tok, 3 * D_MODEL), _F32)],
        compiler_params=pltpu.CompilerParams(dimension_semantics=("arbitrary",),
                                             vmem_limit_bytes=VMEM_LIMIT_BYTES),
        name="sample_layer",
    )(x, mod, projs, spool, sconv, sdelta, *weights)


def _head_expansion_matrix():
    e = np.zeros((HEAD, 3072), np.float32)
    for piece in range(3):
        for hd in range(N_HEADS):
            rb, rg = piece * 2 * N_HEADS + hd, piece * 2 * N_HEADS + N_HEADS + hd
            e[rb, hd * HEAD:(hd + 1) * HEAD] = 1.0
            e[rb, 1024 + hd * CHUNK:1024 + (hd + 1) * CHUNK] = 1.0
            e[rg, 1536 + hd * HEAD:1536 + (hd + 1) * HEAD] = 1.0
            e[rg, 2560 + hd * CHUNK:2560 + (hd + 1) * CHUNK] = 1.0
    return jnp.asarray(e, _BF)


def _layer_weights(w_in, conv_w, a_log, dt_bias, head_norm_w, pool_w, pool_scale, p_a, p_b, w_out, ln_g, ln_b):
    lane_pad = lambda a: jnp.zeros((1, HEAD), _F32).at[0, N_HEADS:2 * N_HEADS].set(a.reshape(N_HEADS))
    return (jnp.transpose(w_in).astype(_BF), conv_w,
            lane_pad(a_log), lane_pad(dt_bias), head_norm_w.reshape(1, HEAD), pool_w.astype(_BF),
            pool_scale.reshape(1, D_MODEL), p_a.astype(_BF), p_b.astype(_BF), w_out.astype(_BF),
            ln_g.reshape(1, D_MODEL), ln_b.reshape(1, D_MODEL), _head_expansion_matrix())


def kernel(x_prompt, x_sample, state_pool, state_conv, state_delta, c_prompt, c_sample, w_ada, b_ada, w_in, conv_w, a_log, dt_bias, head_norm_w, pool_w, pool_scale, p_a, p_b, w_out, ln_g, ln_b):
    assert w_in.shape[0] == 1, "single-layer trunk"
    bp, bs = x_prompt.shape[0], x_sample.shape[0]
    drop = lambda a: a.reshape(a.shape[1:])
    mod = _ada_call(jnp.concatenate([c_prompt, c_sample], axis=0), drop(w_ada), b_ada)
    mod = mod.reshape(bp + bs, 1, 3 * D_MODEL)
    weights = _layer_weights(drop(w_in), drop(conv_w), a_log, dt_bias, head_norm_w, drop(pool_w), pool_scale,
                             drop(p_a), drop(p_b), drop(w_out), ln_g, ln_b)
    rows_major = lambda a: jnp.transpose(a, (1, 0, 2))
    y_p, pool_p, conv_p, delta_p, proj_s = _prompt_call(x_prompt, mod[:bp], x_sample, mod[bp:], weights,
                                                        PROMPT_SEQS, PROMPT_TOKENS)
    y_s, pool_s, conv_s, delta_s = _sample_call(x_sample, mod[bp:], proj_s, rows_major(drop(state_pool)),
                                                rows_major(drop(state_conv)), drop(state_delta),
                                                weights[_N_PROJ_WEIGHTS:], PAST_LEN)
    lift = lambda a: a.reshape((1,) + a.shape)
    return (y_p, y_s, lift(rows_major(pool_p)), lift(rows_major(conv_p)), lift(delta_p),
            lift(rows_major(pool_s)), lift(rows_major(conv_s)), lift(delta_s))
```

```python
import functools

import numpy as np
import jax
import jax.numpy as jnp
from jax import lax
from jax.experimental import pallas as pl
from jax.experimental.pallas import tpu as pltpu

D_MODEL = 1024
N_HEADS = 8
HEAD = 128
N_PAIRS = N_HEADS // 2
POOL_WINDOWS = (2, 4, 8, 16)
POOL_GROUP = 256
POOL_BUF = 15
POOL_PAD = 16
CONV_W = 4
CONV_PAD = 8
CHUNK = 64
INV_BASE = 8
PAST_LEN = 16384
DEEPNORM_ALPHA = 2.0 ** 0.25
LN_EPS = 1e-5
RMS_EPS = 1e-6
L2_EPS = 1e-6
NEG_BIG = -1e30

OFF_MAIN_END = 6 * D_MODEL
OFF_GATE = OFF_MAIN_END + 2 * N_HEADS

SEC = dict(u_a=(0, D_MODEL), z_a=(D_MODEL, 2 * D_MODEL), qkv=(2 * D_MODEL, 5 * D_MODEL),
           z_b=(5 * D_MODEL, 6 * D_MODEL), ba=(OFF_MAIN_END, OFF_MAIN_END + 128),
           ga=(OFF_MAIN_END + 128, OFF_MAIN_END + 128 + D_MODEL),
           gb=(OFF_MAIN_END + 128 + D_MODEL, OFF_MAIN_END + 128 + 2 * D_MODEL))
SEC_WIDTH = OFF_MAIN_END + 128 + 2 * D_MODEL

PROMPT_SEQS = 4
SAMPLE_RIDE = 2
PROMPT_TOKENS = CHUNK
SAMPLE_SEQS = 16
ADA_COLS = 1024
VMEM_LIMIT_BYTES = 58 * 1024 * 1024

_BF = jnp.bfloat16
_F32 = jnp.float32


def _dot(a, b):
    return jnp.dot(a.astype(_BF), b.astype(_BF), preferred_element_type=_F32)


def _dot_nt(a, b):
    return lax.dot_general(a.astype(_BF), b.astype(_BF), (((1,), (1,)), ((), ())),
                           preferred_element_type=_F32)


def _sigmoid(x):
    return 0.5 + 0.5 * jnp.tanh(0.5 * x)


def _silu(x):
    hx = 0.5 * x
    return hx + hx * jnp.tanh(hx)


def _softplus(x):
    return jnp.maximum(x, 0.0) + jnp.log1p(jnp.exp(-jnp.abs(x)))


def _lane_block_diag(x, width):
    nblk = x.shape[1] // width
    lane = lax.broadcasted_iota(jnp.int32, x.shape, 1)
    return jnp.concatenate([jnp.where(lane // width == i, x, 0.0) for i in range(nblk)], axis=0)


def _pair_block_diag(x):
    lane = lax.broadcasted_iota(jnp.int32, x.shape, 1) % (2 * HEAD)
    return jnp.concatenate([jnp.where(lane < HEAD, x, 0.0), jnp.where(lane >= HEAD, x, 0.0)], axis=0)


def _segment_cumsum(x, seg):
    row = lax.broadcasted_iota(jnp.int32, x.shape, 0) % seg
    shift = 1
    while shift < seg:
        x = x + jnp.where(row >= shift, pltpu.roll(x, shift, axis=0), 0.0)
        shift *= 2
    return x


def _expand_heads(narrow, e3_ref):
    lane = lax.broadcasted_iota(jnp.int32, narrow.shape, 1)
    x = jnp.where(lane < 2 * N_HEADS, narrow, 0.0)
    hi = x.astype(_BF).astype(_F32)
    rem = x - hi
    mid = rem.astype(_BF).astype(_F32)
    lo = (rem - mid).astype(_BF).astype(_F32)
    packed = hi + pltpu.roll(mid, 2 * N_HEADS, axis=1) + pltpu.roll(lo, 4 * N_HEADS, axis=1)
    wide = jnp.dot(packed.astype(_BF), e3_ref[...], preferred_element_type=_F32)
    return wide[:, 0:1024], wide[:, 1024:1536], wide[:, 1536:2560], wide[:, 2560:3072]


def _per_head_rsqrt_scale(x, eps, mean, post):
    outs = []
    for h in range(N_HEADS):
        xh = x[:, h * HEAD:(h + 1) * HEAD]
        ss = jnp.sum(xh * xh, axis=-1, keepdims=True)
        if mean:
            ss = ss * (1.0 / HEAD)
        outs.append(xh * (lax.rsqrt(ss + eps) * post))
    return jnp.concatenate(outs, axis=1)


def _gate_scalars(ba, alog_ref, dtb_ref, seg):
    beta = _sigmoid(ba)
    g = -jnp.exp(alog_ref[...]) * _softplus(ba + dtb_ref[...])
    gc = _segment_cumsum(g, seg)
    lane = lax.broadcasted_iota(jnp.int32, ba.shape, 1)
    return jnp.where(lane < N_HEADS, beta, gc)


def _chunk_masks(n_tok):
    row = lax.broadcasted_iota(jnp.int32, (CHUNK, 4 * HEAD), 0)
    col = lax.broadcasted_iota(jnp.int32, (CHUNK, 4 * HEAD), 1) % CHUNK
    same_seq = (row // n_tok) == (col // n_tok)
    return same_seq & (row >= col), same_seq & (row > col), row == col


def _chunks_intra(q, k, v, beta_e, beta_5, gc_e, gc_5, n_tok):
    assert n_tok >= INV_BASE and n_tok & (n_tok - 1) == 0 and CHUNK % n_tok == 0
    n_chunks = q.shape[0] // CHUNK
    incl, strict, eye5 = _chunk_masks(n_tok)
    rows = [slice(c * CHUNK, (c + 1) * CHUNK) for c in range(n_chunks)]
    pair = [slice(p * 2 * HEAD, (p + 1) * 2 * HEAD) for p in range(N_PAIRS)]

    kk, qk = [], []
    for rs in rows:
        kk_parts, qk_parts = [], []
        for sl in pair:
            kp, qp = k[rs, sl], q[rs, sl]
            g2 = _dot_nt(jnp.concatenate([kp, qp], axis=0), _pair_block_diag(kp))
            kk_parts.append(g2[:CHUNK])
            qk_parts.append(g2[CHUNK:])
        kk.append(jnp.concatenate(kk_parts, axis=1))
        qk.append(jnp.concatenate(qk_parts, axis=1))

    dec, glast = [], []
    for rs in rows:
        g5 = gc_5[rs]
        gc_row = jnp.sum(jnp.where(eye5, g5, 0.0), axis=0, keepdims=True)
        dec.append(jnp.exp(jnp.where(incl, g5 - gc_row, NEG_BIG)))
        ge = gc_e[rs]
        glast.append(jnp.concatenate(
            [jnp.broadcast_to(ge[(s + 1) * n_tok - 1:(s + 1) * n_tok, :], (n_tok, D_MODEL))
             for s in range(CHUNK // n_tok)], axis=0))

    row4 = lax.broadcasted_iota(jnp.int32, (CHUNK, 2 * N_PAIRS * CHUNK // 2), 0)
    col4 = lax.broadcasted_iota(jnp.int32, (CHUNK, 2 * N_PAIRS * CHUNK // 2), 1) % CHUNK
    base_blk = (row4 // INV_BASE) == (col4 // INV_BASE)
    prob = [(c, grp) for c in range(n_chunks) for grp in range(2)]
    neg_l, t_inv, pw = {}, {}, {}
    for c, grp in prob:
        gs = slice(grp * 256, (grp + 1) * 256)
        neg_l[c, grp] = jnp.where(strict[:, :256], -(beta_5[rows[c], gs] * dec[c][:, gs] * kk[c][:, gs]), 0.0)
    for key in prob:
        m0 = jnp.where(base_blk, neg_l[key], 0.0)
        t_inv[key] = eye5[:, :256].astype(_F32) + m0
        pw[key] = _dot(m0, _lane_block_diag(m0, CHUNK))
    for key in prob:
        r = _dot(jnp.concatenate([pw[key], t_inv[key]], axis=0), _lane_block_diag(pw[key], CHUNK))
        pw[key] = r[:CHUNK]
        t_inv[key] = t_inv[key] + r[CHUNK:]
    for key in prob:
        t_inv[key] = t_inv[key] + _dot(t_inv[key], _lane_block_diag(pw[key], CHUNK))
    blk = INV_BASE
    while blk < n_tok:
        below = ((row4 // (2 * blk)) == (col4 // (2 * blk))) & ((row4 // blk) != (col4 // blk))
        for key in prob:
            pw[key] = _dot(jnp.where(below, neg_l[key], 0.0), _lane_block_diag(t_inv[key], CHUNK))
        for key in prob:
            t_inv[key] = t_inv[key] + _dot(t_inv[key], _lane_block_diag(pw[key], CHUNK))
        blk *= 2
    s = t_inv

    out = []
    for c, rs in enumerate(rows):
        gam = jnp.exp(gc_e[rs])
        be = beta_e[rs]
        bv = be * v[rs]
        gbk = be * gam * k[rs]
        wv_parts, wk_parts = [], []
        for p, sl in enumerate(pair):
            t_pair = s[c, p // 2][:, (p % 2) * HEAD:(p % 2 + 1) * HEAD]
            sol = _dot(t_pair, _pair_block_diag(jnp.concatenate([bv[:, sl], gbk[:, sl]], axis=1)))
            wv_parts.append(sol[:, :2 * HEAD])
            wk_parts.append(sol[:, 2 * HEAD:])
        out.append(dict(w_v=jnp.concatenate(wv_parts, axis=1), w_k=jnp.concatenate(wk_parts, axis=1),
                        qg=q[rs] * gam, kt=k[rs] * jnp.exp(glast[c] - gc_e[rs]),
                        qkd=qk[c] * dec[c], gl=jnp.exp(glast[c])))
    return out


def _intra_output(qkd, u):
    outs = []
    for p in range(N_PAIRS):
        up = u[:, p * 2 * HEAD:(p + 1) * 2 * HEAD]
        outs.append(_dot(qkd[:, p * HEAD:(p + 1) * HEAD], _pair_block_diag(up)))
    return jnp.concatenate(outs, axis=1)


def _state_to_rows(state_ref, ext_ref, row0, n_rows, n_seq):
    for s in range(n_seq):
        for r in range(n_rows):
            ext_ref[s, row0 + r:row0 + r + 1, :] = state_ref[r, s:s + 1, :]


def _rows_to_state(ext_ref, row0, n_rows, state_ref, seq0, n_seq):
    for s in range(n_seq):
        for r in range(n_rows):
            state_ref[r, seq0 + s:seq0 + s + 1, :] = ext_ref[s, row0 + r:row0 + r + 1, :]


def _modulated(x3, mod3):
    shift, scale = mod3[:, :, 0:D_MODEL], mod3[:, :, D_MODEL:2 * D_MODEL]
    return (x3 * (1.0 + scale) + shift).reshape(x3.shape[0] * x3.shape[1], D_MODEL).astype(_BF)


def _front(sec, pos, refs, extp_ref, extc_ref, n_seq, n_tok):
    rows = n_seq * n_tok
    extc_ref[:, CONV_PAD:CONV_PAD + n_tok, :] = sec("qkv").reshape(n_seq, n_tok, 3 * D_MODEL)

    seg_c = CONV_PAD + n_tok
    convw_ref = refs["convw"]
    e = extc_ref[...].reshape(n_seq * seg_c, 3 * D_MODEL)
    acc = e * convw_ref[0:1, :]
    for i in range(1, CONV_W):
        acc = pltpu.roll(acc, 1, axis=0) + e * convw_ref[i:i + 1, :]
    qkv = _silu(acc.reshape(n_seq, seg_c, 3 * D_MODEL)[:, CONV_PAD:, :].reshape(rows, 3 * D_MODEL))
    q = _per_head_rsqrt_scale(qkv[:, 0:D_MODEL], L2_EPS, False, HEAD ** -0.5)
    k = _per_head_rsqrt_scale(qkv[:, D_MODEL:2 * D_MODEL], L2_EPS, False, 1.0)
    v = qkv[:, 2 * D_MODEL:]

    extp_ref[:, POOL_PAD:POOL_PAD + n_tok, :] = sec("u_a").reshape(n_seq, n_tok, D_MODEL)
    z_a, ba, z_b, ga, gb = sec("z_a"), sec("ba"), sec("z_b"), sec("ga"), sec("gb")

    seg_p = POOL_PAD + n_tok
    pooled = []
    for gi, w in enumerate(POOL_WINDOWS):
        e = extp_ref[:, :, gi * POOL_GROUP:(gi + 1) * POOL_GROUP].reshape(n_seq * seg_p, POOL_GROUP)
        acc, span = e, 1
        while span < w:
            acc = acc + pltpu.roll(acc, span, axis=0)
            span *= 2
        take = lambda a: a.reshape(n_seq, seg_p, POOL_GROUP)[:, POOL_PAD:, :].reshape(rows, POOL_GROUP)
        inv_cnt = 1.0 / jnp.minimum(pos + 1, w).astype(_F32)
        pooled.append(take(acc) * inv_cnt - take(e))
    mixed = jnp.concatenate([_dot(pp, refs["poolw"][gi]) for gi, pp in enumerate(pooled)], axis=1)
    y_a = (mixed * refs["pscale"][...] * _silu(z_a)).astype(_BF)
    a_proj = _dot(y_a, refs["pa"][...])

    beta_e, beta_5, gc_e, gc_5 = _expand_heads(_gate_scalars(ba, refs["alog"], refs["dtb"], n_tok), refs["e3"])
    return dict(q=q, k=k, v=v, beta_e=beta_e, beta_5=beta_5, gc_e=gc_e, gc_5=gc_5,
                a_proj=a_proj, z_b=z_b, ga=ga, gb=gb)


def _back(x, o, f, gate, refs):
    o = _per_head_rsqrt_scale(o, RMS_EPS, True, refs["hnw"][...])
    y_b = (o * _silu(f["z_b"])).astype(_BF)
    merged = _sigmoid(f["ga"]) * f["a_proj"] + _sigmoid(f["gb"]) * _dot(y_b, refs["pb"][...])
    sub = (1.0 + gate) * _dot(merged, refs["wout"][...])
    r = DEEPNORM_ALPHA * x + sub
    mu = jnp.mean(r, axis=-1, keepdims=True)
    rc = r - mu
    var = jnp.mean(rc * rc, axis=-1, keepdims=True)
    return rc * lax.rsqrt(var + LN_EPS) * refs["lng"][...] + refs["lnb"][...]


_WEIGHT_NAMES = ("w_t", "convw", "alog", "dtb", "hnw", "poolw", "pscale", "pa", "pb",
                 "wout", "lng", "lnb", "e3")
_N_PROJ_WEIGHTS = 1
_SAMPLE_WEIGHT_NAMES = _WEIGHT_NAMES[_N_PROJ_WEIGHTS:]


def _ada_kernel(c_ref, w_ref, b_ref, o_ref):
    o_ref[...] = _dot(_silu(c_ref[...]), w_ref[...]) + b_ref[...]


def _ada_call(c_all, w_ada, b_ada):
    n = c_all.shape[0]
    return pl.pallas_call(
        _ada_kernel,
        grid=(3 * D_MODEL // ADA_COLS,),
        in_specs=[pl.BlockSpec((n, D_MODEL), lambda j: (0, 0)),
                  pl.BlockSpec((D_MODEL, ADA_COLS), lambda j: (0, j)),
                  pl.BlockSpec((1, ADA_COLS), lambda j: (0, j))],
        out_specs=pl.BlockSpec((n, ADA_COLS), lambda j: (0, j)),
        out_shape=jax.ShapeDtypeStruct((n, 3 * D_MODEL), _F32),
        compiler_params=pltpu.CompilerParams(dimension_semantics=("arbitrary",)),
        name="adaln_mod",
    )(c_all, w_ada, b_ada)


def _prompt_kernel(x_ref, mod_ref, xs_ref, mods_ref, *rest, n_seq, n_tok, n_steps, n_groups):
    refs = dict(zip(_WEIGHT_NAMES, rest[:len(_WEIGHT_NAMES)]))
    (y_ref, pool_out_ref, conv_out_ref, delta_out_ref, projs_ref,
     extp_ref, extc_ref, sbd_ref) = rest[len(_WEIGHT_NAMES):]
    t = pl.program_id(1)
    rows = n_seq * n_tok

    @pl.when(t == 0)
    def _():
        extp_ref[:, 0:POOL_PAD, :] = jnp.zeros((n_seq, POOL_PAD, D_MODEL), _F32)
        extc_ref[:, 0:CONV_PAD, :] = jnp.zeros((n_seq, CONV_PAD, 3 * D_MODEL), _F32)
        sbd_ref[...] = jnp.zeros(sbd_ref.shape, _F32)

    x3 = x_ref[...]
    mod3 = mod_ref[...]
    pos = t * n_tok + lax.broadcasted_iota(jnp.int32, (rows, 1), 0) % n_tok
    h = jnp.concatenate([_modulated(x3, mod3), _modulated(xs_ref[...], mods_ref[...])], axis=0)
    w_rows = dict(SEC, ba=(OFF_MAIN_END, OFF_MAIN_END + HEAD), ga=(OFF_GATE, OFF_GATE + D_MODEL),
                  gb=(OFF_GATE + D_MODEL, OFF_GATE + 2 * D_MODEL))
    w_t_ref = refs["w_t"]

    def sec(name):
        r0, r1 = w_rows[name]
        full = lax.dot_general(h, w_t_ref[r0:r1, :], (((1,), (1,)), ((), ())), preferred_element_type=_F32)
        projs_ref[:, SEC[name][0]:SEC[name][1]] = full[rows:]
        return full[:rows]

    f = _front(sec, pos, refs, extp_ref, extc_ref, n_seq, n_tok)
    chunks = _chunks_intra(f["q"], f["k"], f["v"], f["beta_e"], f["beta_5"], f["gc_e"], f["gc_5"], n_tok)

    pair = [slice(p * 2 * HEAD, (p + 1) * 2 * HEAD) for p in range(N_PAIRS)]
    row_bd = lax.broadcasted_iota(jnp.int32, (2 * HEAD, 2 * HEAD), 0) // HEAD
    col_bd = lax.broadcasted_iota(jnp.int32, (2 * HEAD, 2 * HEAD), 1) // HEAD
    same_head = row_bd == col_bd
    us, ois = [], []
    for c, ck in enumerate(chunks):
        u_parts, oi_parts = [], []
        for p, sl in enumerate(pair):
            r = _dot(jnp.concatenate([ck["w_k"][:, sl], ck["qg"][:, sl]], axis=0), sbd_ref[c * N_PAIRS + p])
            u_parts.append(ck["w_v"][:, sl] - r[:CHUNK])
            oi_parts.append(r[CHUNK:])
        us.append(jnp.concatenate(u_parts, axis=1))
        ois.append(jnp.concatenate(oi_parts, axis=1))
    o = jnp.concatenate([ois[c] + _intra_output(ck["qkd"], us[c]) for c, ck in enumerate(chunks)], axis=0)
    for c, ck in enumerate(chunks):
        for p, sl in enumerate(pair):
            upd = _dot(ck["kt"][:, sl].T, us[c][:, sl])
            i = c * N_PAIRS + p
            sbd_ref[i] = ck["gl"][0:1, sl] * sbd_ref[i] + jnp.where(same_head, upd, 0.0)

    gate = jnp.broadcast_to(mod3[:, :, 2 * D_MODEL:], (n_seq, n_tok, D_MODEL)).reshape(rows, D_MODEL)
    y = _back(x3.reshape(rows, D_MODEL), o, f, gate, refs)
    y_ref[...] = y.reshape(n_seq, n_tok, D_MODEL)

    @pl.when(t == n_steps - 1)
    def _():
        for s in range(n_seq):
            for hd in range(N_HEADS):
                o0 = (hd % 2) * HEAD
                delta_out_ref[s, hd] = sbd_ref[s * N_PAIRS + hd // 2, o0:o0 + HEAD, o0:o0 + HEAD]

    for grp in range(n_groups):
        @pl.when((t == n_steps - 1) & (pl.program_id(0) == grp))
        def _():
            _rows_to_state(extp_ref, POOL_PAD + n_tok - POOL_BUF, POOL_BUF, pool_out_ref, grp * n_seq, n_seq)
            _rows_to_state(extc_ref, CONV_PAD + n_tok - (CONV_W - 1), CONV_W - 1, conv_out_ref, grp * n_seq, n_seq)

    @pl.when(t < n_steps - 1)
    def _():
        extp_ref[:, 0:POOL_PAD, :] = extp_ref[:, n_tok:n_tok + POOL_PAD, :]
        extc_ref[:, 0:CONV_PAD, :] = extc_ref[:, n_tok:n_tok + CONV_PAD, :]


def _const_spec(shape):
    nd = len(shape)
    return pl.BlockSpec(shape, lambda *_: (0,) * nd, pipeline_mode=pl.Buffered(1))


def _weight_specs(weights):
    return [_const_spec(w.shape) for w in weights]


def _prompt_call(x, mod, xs, mods, weights, n_seq, n_tok):
    bsz, seq, _ = x.shape
    n_steps = seq // n_tok
    n_sample, s_tok, _ = xs.shape
    assert n_tok == CHUNK and bsz % n_seq == 0 and seq % n_tok == 0
    assert n_sample == SAMPLE_RIDE * (bsz // n_seq) * n_steps
    ride = lambda shape: pl.BlockSpec((SAMPLE_RIDE,) + shape, lambda g, t: (g * n_steps + t,) + (0,) * len(shape))
    kern = functools.partial(_prompt_kernel, n_seq=n_seq, n_tok=n_tok, n_steps=n_steps, n_groups=bsz // n_seq)
    seq_block = lambda shape: pl.BlockSpec((n_seq,) + shape, lambda g, t: (g,) + (0,) * len(shape))
    return pl.pallas_call(
        kern,
        grid=(bsz // n_seq, n_steps),
        in_specs=[pl.BlockSpec((n_seq, n_tok, D_MODEL), lambda g, t: (g, t, 0)),
                  seq_block((1, 3 * D_MODEL)), ride((s_tok, D_MODEL)), ride((1, 3 * D_MODEL))]
                 + _weight_specs(weights),
        out_specs=[pl.BlockSpec((n_seq, n_tok, D_MODEL), lambda g, t: (g, t, 0)),
                   pl.BlockSpec((POOL_BUF, bsz, D_MODEL), lambda g, t: (0, 0, 0)),
                   pl.BlockSpec((CONV_W - 1, bsz, 3 * D_MODEL), lambda g, t: (0, 0, 0)),
                   seq_block((N_HEADS, HEAD, HEAD)),
                   pl.BlockSpec((SAMPLE_RIDE * s_tok, SEC_WIDTH), lambda g, t: (g * n_steps + t, 0))],
        out_shape=[jax.ShapeDtypeStruct((bsz, seq, D_MODEL), _F32),
                   jax.ShapeDtypeStruct((POOL_BUF, bsz, D_MODEL), _F32),
                   jax.ShapeDtypeStruct((CONV_W - 1, bsz, 3 * D_MODEL), _F32),
                   jax.ShapeDtypeStruct((bsz, N_HEADS, HEAD, HEAD), _F32),
                   jax.ShapeDtypeStruct((n_sample * s_tok, SEC_WIDTH), _F32)],
        scratch_shapes=[pltpu.VMEM((n_seq, POOL_PAD + n_tok, D_MODEL), _F32),
                        pltpu.VMEM((n_seq, CONV_PAD + n_tok, 3 * D_MODEL), _F32),
                        pltpu.VMEM((n_seq * N_PAIRS, 2 * HEAD, 2 * HEAD), _F32)],
        compiler_params=pltpu.CompilerParams(dimension_semantics=("arbitrary", "arbitrary"),
                                             vmem_limit_bytes=VMEM_LIMIT_BYTES),
        name="prompt_layer",
    )(x, mod, xs, mods, *weights)


def _sample_kernel(x_ref, mod_ref, projs_ref, spool_ref, sconv_ref, sdelta_hbm, *rest, n_seq, n_tok, n_steps, pos0):
    refs = dict(zip(_SAMPLE_WEIGHT_NAMES, rest[:len(_SAMPLE_WEIGHT_NAMES)]))
    (y_ref, pool_out_ref, conv_out_ref, delta_out_hbm,
     extp_ref, extc_ref, sin_ref, sout_ref, in_sem, out_sem) = rest[len(_SAMPLE_WEIGHT_NAMES):]
    i = pl.program_id(0)
    rows = n_seq * n_tok
    per_chunk = CHUNK // n_tok
    n_chunks = n_seq // per_chunk

    def state_in(step, c):
        first = (step * n_chunks + c) * per_chunk
        return pltpu.make_async_copy(sdelta_hbm.at[pl.ds(first, per_chunk)], sin_ref.at[c], in_sem.at[c])

    def state_out(step, c):
        first = (step * n_chunks + c) * per_chunk
        return pltpu.make_async_copy(sout_ref.at[c], delta_out_hbm.at[pl.ds(first, per_chunk)], out_sem.at[c])

    @pl.when(i == 0)
    def _():
        for c in range(n_chunks):
            state_in(0, c).start()

    _state_to_rows(spool_ref, extp_ref, POOL_PAD - POOL_BUF, POOL_BUF, n_seq)
    extp_ref[:, 0:POOL_PAD - POOL_BUF, :] = jnp.zeros((n_seq, POOL_PAD - POOL_BUF, D_MODEL), _F32)
    _state_to_rows(sconv_ref, extc_ref, CONV_PAD - (CONV_W - 1), CONV_W - 1, n_seq)
    extc_ref[:, 0:CONV_PAD - (CONV_W - 1), :] = jnp.zeros((n_seq, CONV_PAD - (CONV_W - 1), 3 * D_MODEL), _F32)

    x3 = x_ref[...]
    mod3 = mod_ref[...]
    pos = pos0 + lax.broadcasted_iota(jnp.int32, (rows, 1), 0) % n_tok
    f = _front(lambda name: projs_ref[:, SEC[name][0]:SEC[name][1]], pos, refs, extp_ref, extc_ref, n_seq, n_tok)
    _rows_to_state(extp_ref, POOL_PAD + n_tok - POOL_BUF, POOL_BUF, pool_out_ref, 0, n_seq)
    _rows_to_state(extc_ref, CONV_PAD + n_tok - (CONV_W - 1), CONV_W - 1, conv_out_ref, 0, n_seq)
    chunks = _chunks_intra(f["q"], f["k"], f["v"], f["beta_e"], f["beta_5"], f["gc_e"], f["gc_5"], n_tok)

    lane_seq = lax.broadcasted_iota(jnp.int32, (HEAD, CHUNK), 1) // n_tok
    os_ = []
    for c, ck in enumerate(chunks):
        state_in(i, c).wait()

        @pl.when(i > 0)
        def _():
            state_out(i - 1, c).wait()

        u_rows, oi_rows = [], []
        for sl_ in range(per_chunk):
            rs = slice(sl_ * n_tok, (sl_ + 1) * n_tok)
            u_h, oi_h = [], []
            for hd in range(N_HEADS):
                cs = slice(hd * HEAD, (hd + 1) * HEAD)
                r = _dot(jnp.concatenate([ck["w_k"][rs, cs], ck["qg"][rs, cs]], axis=0), sin_ref[c, sl_, hd])
                u_h.append(ck["w_v"][rs, cs] - r[:n_tok])
                oi_h.append(r[n_tok:])
            u_rows.append(jnp.concatenate(u_h, axis=1))
            oi_rows.append(jnp.concatenate(oi_h, axis=1))
        u = jnp.concatenate(u_rows, axis=0)
        os_.append(jnp.concatenate(oi_rows, axis=0) + _intra_output(ck["qkd"], u))
        for hd in range(N_HEADS):
            cs = slice(hd * HEAD, (hd + 1) * HEAD)
            kt_t = ck["kt"][:, cs].T
            lhs = jnp.concatenate([jnp.where(lane_seq == sl_, kt_t, 0.0) for sl_ in range(per_chunk)], axis=0)
            upd = _dot(lhs, u[:, cs])
            for sl_ in range(per_chunk):
                gl = ck["gl"][sl_ * n_tok:sl_ * n_tok + 1, cs]
                sout_ref[c, sl_, hd] = gl * sin_ref[c, sl_, hd] + upd[sl_ * HEAD:(sl_ + 1) * HEAD]
        state_out(i, c).start()

        @pl.when(i + 1 < n_steps)
        def _():
            state_in(i + 1, c).start()
    o = jnp.concatenate(os_, axis=0)

    gate = jnp.broadcast_to(mod3[:, :, 2 * D_MODEL:], (n_seq, n_tok, D_MODEL)).reshape(rows, D_MODEL)
    y = _back(x3.reshape(rows, D_MODEL), o, f, gate, refs)
    y_ref[...] = y.reshape(n_seq, n_tok, D_MODEL)

    @pl.when(i == n_steps - 1)
    def _():
        for c in range(n_chunks):
            state_out(i, c).wait()


def _sample_call(x, mod, projs, spool, sconv, sdelta, weights, pos0):
    bsz, n_tok, _ = x.shape
    n_seq = SAMPLE_SEQS
    assert (n_seq * n_tok) % CHUNK == 0 and bsz % n_seq == 0
    n_steps = bsz // n_seq
    n_chunks = n_seq * n_tok // CHUNK
    kern = functools.partial(_sample_kernel, n_seq=n_seq, n_tok=n_tok, n_steps=n_steps, pos0=pos0)
    seq_block = lambda shape: pl.BlockSpec((n_seq,) + shape, lambda i: (i,) + (0,) * len(shape))
    row_state = lambda n_rows, width: pl.BlockSpec((n_rows, n_seq, width), lambda i: (0, i, 0))
    slot = (n_chunks, CHUNK // n_tok, N_HEADS, HEAD, HEAD)
    return pl.pallas_call(
        kern,
        grid=(n_steps,),
        in_specs=[seq_block((n_tok, D_MODEL)), seq_block((1, 3 * D_MODEL)),
                  pl.BlockSpec((n_seq * n_tok, SEC_WIDTH), lambda i: (i, 0)),
                  row_state(POOL_BUF, D_MODEL), row_state(CONV_W - 1, 3 * D_MODEL),
                  pl.BlockSpec(memory_space=pl.ANY)] + [_const_spec(w.shape) for w in weights],
        out_specs=[seq_block((n_tok, D_MODEL)), row_state(POOL_BUF, D_MODEL),
                   row_state(CONV_W - 1, 3 * D_MODEL), pl.BlockSpec(memory_space=pl.ANY)],
        out_shape=[jax.ShapeDtypeStruct((bsz, n_tok, D_MODEL), _F32),
                   jax.ShapeDtypeStruct((POOL_BUF, bsz, D_MODEL), _F32),
                   jax.ShapeDtypeStruct((CONV_W - 1, bsz, 3 * D_MODEL), _F32),
                   jax.ShapeDtypeStruct((bsz, N_HEADS, HEAD, HEAD), _F32)],
        scratch_shapes=[pltpu.VMEM((n_seq, POOL_PAD + n_tok, D_MODEL), _F32),
                        pltpu.VMEM((n_seq, CONV_PAD + n_tok, 3 * D_MODEL), _F32),
                        pltpu.VMEM(slot, _F32), pltpu.VMEM(slot, _F32),
                        pltpu.SemaphoreType.DMA((n_chunks,)), pltpu.SemaphoreType.DMA((n_chunks,))],
        compiler_params=pltpu.CompilerParams(dimension_semantics=("arbitrary",),
                                             vmem_limit_bytes=VMEM_LIMIT_BYTES),
        name="sample_layer",
    )(x, mod, projs, spool, sconv, sdelta, *weights)


def _head_expansion_matrix():
    e = np.zeros((HEAD, 3072), np.float32)
    for piece in range(3):
        for hd in range(N_HEADS):
            rb, rg = piece * 2 * N_HEADS + hd, piece * 2 * N_HEADS + N_HEADS + hd
            e[rb, hd * HEAD:(hd + 1) * HEAD] = 1.0
            e[rb, 1024 + hd * CHUNK:1024 + (hd + 1) * CHUNK] = 1.0
            e[rg, 1536 + hd * HEAD:1536 + (hd + 1) * HEAD] = 1.0
            e[rg, 2560 + hd * CHUNK:2560 + (hd + 1) * CHUNK] = 1.0
    return jnp.asarray(e, _BF)


def _layer_weights(w_in, conv_w, a_log, dt_bias, head_norm_w, pool_w, pool_scale, p_a, p_b, w_out, ln_g, ln_b):
    lane_pad = lambda a: jnp.zeros((1, HEAD), _F32).at[0, N_HEADS:2 * N_HEADS].set(a.reshape(N_HEADS))
    return (jnp.transpose(w_in).astype(_BF), conv_w,
            lane_pad(a_log), lane_pad(dt_bias), head_norm_w.reshape(1, HEAD), pool_w.astype(_BF),
            pool_scale.reshape(1, D_MODEL), p_a.astype(_BF), p_b.astype(_BF), w_out.astype(_BF),
            ln_g.reshape(1, D_MODEL), ln_b.reshape(1, D_MODEL), _head_expansion_matrix())


def kernel(x_prompt, x_sample, state_pool, state_conv, state_delta, c_prompt, c_sample, w_ada, b_ada, w_in, conv_w, a_log, dt_bias, head_norm_w, pool_w, pool_scale, p_a, p_b, w_out, ln_g, ln_b):
    assert w_in.shape[0] == 1, "single-layer trunk"
    bp, bs = x_prompt.shape[0], x_sample.shape[0]
    drop = lambda a: a.reshape(a.shape[1:])
    mod = _ada_call(jnp.concatenate([c_prompt, c_sample], axis=0), drop(w_ada), b_ada)
    mod = mod.reshape(bp + bs, 1, 3 * D_MODEL)
    weights = _layer_weights(drop(w_in), drop(conv_w), a_log, dt_bias, head_norm_w, drop(pool_w), pool_scale,
                             drop(p_a), drop(p_b), drop(w_out), ln_g, ln_b)
    rows_major = lambda a: jnp.transpose(a, (1, 0, 2))
    y_p, pool_p, conv_p, delta_p, proj_s = _prompt_call(x_prompt, mod[:bp], x_sample, mod[bp:], weights,
                                                        PROMPT_SEQS, PROMPT_TOKENS)
    y_s, pool_s, conv_s, delta_s = _sample_call(x_sample, mod[bp:], proj_s, rows_major(drop(state_pool)),
                                                rows_major(drop(state_conv)), drop(state_delta),
                                                weights[_N_PROJ_WEIGHTS:], PAST_LEN)
    lift = lambda a: a.reshape((1,) + a.shape)
    return (y_p, y_s, lift(rows_major(pool_p)), lift(rows_major(conv_p)), lift(delta_p),
            lift(rows_major(pool_s)), lift(rows_major(conv_s)), lift(delta_s))
```

```python
import functools

import numpy as np
import jax
import jax.numpy as jnp
from jax import lax
from jax.experimental import pallas as pl
from jax.experimental.pallas import tpu as pltpu

D_MODEL = 1024
N_HEADS = 8
HEAD = 128
N_PAIRS = N_HEADS // 2
POOL_WINDOWS = (2, 4, 8, 16)
POOL_GROUP = 256
POOL_BUF = 15
POOL_PAD = 16
CONV_W = 4
CONV_PAD = 8
CHUNK = 64
INV_GROUP = 128
INV_BASE = 8
PAST_LEN = 16384
DEEPNORM_ALPHA = 2.0 ** 0.25
LN_EPS = 1e-5
RMS_EPS = 1e-6
L2_EPS = 1e-6
NEG_BIG = -1e30

OFF_MAIN_END = 6 * D_MODEL
OFF_GATE = OFF_MAIN_END + 2 * N_HEADS

SEC = dict(u_a=(0, D_MODEL), z_a=(D_MODEL, 2 * D_MODEL), qkv=(2 * D_MODEL, 5 * D_MODEL),
           z_b=(5 * D_MODEL, 6 * D_MODEL), ba=(OFF_MAIN_END, OFF_MAIN_END + 128),
           ga=(OFF_MAIN_END + 128, OFF_MAIN_END + 128 + D_MODEL),
           gb=(OFF_MAIN_END + 128 + D_MODEL, OFF_MAIN_END + 128 + 2 * D_MODEL))
SEC_WIDTH = OFF_MAIN_END + 128 + 2 * D_MODEL

PROMPT_SEQS = 4
SAMPLE_RIDE = 2
PROMPT_TOKENS = CHUNK
SAMPLE_SEQS = 16
ADA_COLS = 1024
VMEM_LIMIT_BYTES = 58 * 1024 * 1024

_BF = jnp.bfloat16
_F32 = jnp.float32


def _dot(a, b):
    return jnp.dot(a.astype(_BF), b.astype(_BF), preferred_element_type=_F32)


def _dot_nt(a, b):
    return lax.dot_general(a.astype(_BF), b.astype(_BF), (((1,), (1,)), ((), ())),
                           preferred_element_type=_F32)


def _sigmoid(x):
    return 0.5 + 0.5 * jnp.tanh(0.5 * x)


def _silu(x):
    hx = 0.5 * x
    return hx + hx * jnp.tanh(hx)


def _softplus(x):
    return jnp.maximum(x, 0.0) + jnp.log1p(jnp.exp(-jnp.abs(x)))


def _lane_block_diag(x, width):
    nblk = x.shape[1] // width
    lane = lax.broadcasted_iota(jnp.int32, x.shape, 1)
    return jnp.concatenate([jnp.where(lane // width == i, x, 0.0) for i in range(nblk)], axis=0)


def _pair_block_diag(x):
    lane = lax.broadcasted_iota(jnp.int32, x.shape, 1) % (2 * HEAD)
    return jnp.concatenate([jnp.where(lane < HEAD, x, 0.0), jnp.where(lane >= HEAD, x, 0.0)], axis=0)


def _segment_cumsum(x, seg):
    row = lax.broadcasted_iota(jnp.int32, x.shape, 0) % seg
    shift = 1
    while shift < seg:
        x = x + jnp.where(row >= shift, pltpu.roll(x, shift, axis=0), 0.0)
        shift *= 2
    return x


def _expand_heads(narrow, e3_ref):
    lane = lax.broadcasted_iota(jnp.int32, narrow.shape, 1)
    x = jnp.where(lane < 2 * N_HEADS, narrow, 0.0)
    hi = x.astype(_BF).astype(_F32)
    rem = x - hi
    mid = rem.astype(_BF).astype(_F32)
    lo = (rem - mid).astype(_BF).astype(_F32)
    packed = hi + pltpu.roll(mid, 2 * N_HEADS, axis=1) + pltpu.roll(lo, 4 * N_HEADS, axis=1)
    wide = jnp.dot(packed.astype(_BF), e3_ref[...], preferred_element_type=_F32)
    return wide[:, 0:1024], wide[:, 1024:1536], wide[:, 1536:2560], wide[:, 2560:3072]


def _per_head_rsqrt_scale(x, eps, mean, post):
    outs = []
    for h in range(N_HEADS):
        xh = x[:, h * HEAD:(h + 1) * HEAD]
        ss = jnp.sum(xh * xh, axis=-1, keepdims=True)
        if mean:
            ss = ss * (1.0 / HEAD)
        outs.append(xh * (lax.rsqrt(ss + eps) * post))
    return jnp.concatenate(outs, axis=1)


def _gate_scalars(ba, alog_ref, dtb_ref, seg):
    beta = _sigmoid(ba)
    g = -jnp.exp(alog_ref[...]) * _softplus(ba + dtb_ref[...])
    gc = _segment_cumsum(g, seg)
    lane = lax.broadcasted_iota(jnp.int32, ba.shape, 1)
    return jnp.where(lane < N_HEADS, beta, gc)


def _chunk_masks(n_tok):
    row = lax.broadcasted_iota(jnp.int32, (CHUNK, 4 * HEAD), 0)
    col = lax.broadcasted_iota(jnp.int32, (CHUNK, 4 * HEAD), 1) % CHUNK
    same_seq = (row // n_tok) == (col // n_tok)
    return same_seq & (row >= col), same_seq & (row > col), row == col


def _chunks_intra(q, k, v, beta_e, beta_5, gc_e, gc_5, n_tok):
    assert n_tok >= INV_BASE and n_tok & (n_tok - 1) == 0 and CHUNK % n_tok == 0
    n_chunks = q.shape[0] // CHUNK
    incl, strict, eye5 = _chunk_masks(n_tok)
    rows = [slice(c * CHUNK, (c + 1) * CHUNK) for c in range(n_chunks)]
    pair = [slice(p * 2 * HEAD, (p + 1) * 2 * HEAD) for p in range(N_PAIRS)]

    kk, qk = [], []
    for rs in rows:
        kk_parts, qk_parts = [], []
        for sl in pair:
            kp, qp = k[rs, sl], q[rs, sl]
            g2 = _dot_nt(jnp.concatenate([kp, qp], axis=0), _pair_block_diag(kp))
            kk_parts.append(g2[:CHUNK])
            qk_parts.append(g2[CHUNK:])
        kk.append(jnp.concatenate(kk_parts, axis=1))
        qk.append(jnp.concatenate(qk_parts, axis=1))

    dec, glast = [], []
    for rs in rows:
        g5 = gc_5[rs]
        gc_row = jnp.sum(jnp.where(eye5, g5, 0.0), axis=0, keepdims=True)
        dec.append(jnp.exp(jnp.where(incl, g5 - gc_row, NEG_BIG)))
        ge = gc_e[rs]
        glast.append(jnp.concatenate(
            [jnp.broadcast_to(ge[(s + 1) * n_tok - 1:(s + 1) * n_tok, :], (n_tok, D_MODEL))
             for s in range(CHUNK // n_tok)], axis=0))

    row4 = lax.broadcasted_iota(jnp.int32, (CHUNK, INV_GROUP), 0)
    col4 = lax.broadcasted_iota(jnp.int32, (CHUNK, INV_GROUP), 1) % CHUNK
    base_blk = (row4 // INV_BASE) == (col4 // INV_BASE)
    prob = [(c, grp) for c in range(n_chunks) for grp in range(N_HEADS * CHUNK // INV_GROUP)]
    neg_l, t_inv, pw = {}, {}, {}
    for c, grp in prob:
        gs = slice(grp * INV_GROUP, (grp + 1) * INV_GROUP)
        neg_l[c, grp] = jnp.where(strict[:, :INV_GROUP],
                                  -(beta_5[rows[c], gs] * dec[c][:, gs] * kk[c][:, gs]), 0.0)
    for key in prob:
        m0 = jnp.where(base_blk, neg_l[key], 0.0)
        t_inv[key] = eye5[:, :INV_GROUP].astype(_F32) + m0
        pw[key] = _dot(m0, _lane_block_diag(m0, CHUNK))
    for key in prob:
        r = _dot(jnp.concatenate([pw[key], t_inv[key]], axis=0), _lane_block_diag(pw[key], CHUNK))
        pw[key] = r[:CHUNK]
        t_inv[key] = t_inv[key] + r[CHUNK:]
    for key in prob:
        t_inv[key] = t_inv[key] + _dot(t_inv[key], _lane_block_diag(pw[key], CHUNK))
    blk = INV_BASE
    while blk < n_tok:
        below = ((row4 // (2 * blk)) == (col4 // (2 * blk))) & ((row4 // blk) != (col4 // blk))
        for key in prob:
            pw[key] = _dot(jnp.where(below, neg_l[key], 0.0), _lane_block_diag(t_inv[key], CHUNK))
        for key in prob:
            t_inv[key] = t_inv[key] + _dot(t_inv[key], _lane_block_diag(pw[key], CHUNK))
        blk *= 2
    s = t_inv

    out = []
    for c, rs in enumerate(rows):
        gam = jnp.exp(gc_e[rs])
        be = beta_e[rs]
        bv = be * v[rs]
        gbk = be * gam * k[rs]
        wv_parts, wk_parts = [], []
        for p, sl in enumerate(pair):
            t_pair = s[c, p]
            sol = _dot(t_pair, _pair_block_diag(jnp.concatenate([bv[:, sl], gbk[:, sl]], axis=1)))
            wv_parts.append(sol[:, :2 * HEAD])
            wk_parts.append(sol[:, 2 * HEAD:])
        out.append(dict(w_v=jnp.concatenate(wv_parts, axis=1), w_k=jnp.concatenate(wk_parts, axis=1),
                        qg=q[rs] * gam, kt=k[rs] * jnp.exp(glast[c] - gc_e[rs]),
                        qkd=qk[c] * dec[c], gl=jnp.exp(glast[c])))
    return out


def _intra_output(qkd, u):
    outs = []
    for p in range(N_PAIRS):
        up = u[:, p * 2 * HEAD:(p + 1) * 2 * HEAD]
        outs.append(_dot(qkd[:, p * HEAD:(p + 1) * HEAD], _pair_block_diag(up)))
    return jnp.concatenate(outs, axis=1)


def _state_to_rows(state_ref, ext_ref, row0, n_rows, n_seq):
    for s in range(n_seq):
        for r in range(n_rows):
            ext_ref[s, row0 + r:row0 + r + 1, :] = state_ref[r, s:s + 1, :]


def _rows_to_state(ext_ref, row0, n_rows, state_ref, seq0, n_seq):
    for s in range(n_seq):
        for r in range(n_rows):
            state_ref[r, seq0 + s:seq0 + s + 1, :] = ext_ref[s, row0 + r:row0 + r + 1, :]


def _modulated(x3, mod3):
    shift, scale = mod3[:, :, 0:D_MODEL], mod3[:, :, D_MODEL:2 * D_MODEL]
    return (x3 * (1.0 + scale) + shift).reshape(x3.shape[0] * x3.shape[1], D_MODEL).astype(_BF)


def _front(sec, pos, refs, extp_ref, extc_ref, n_seq, n_tok):
    rows = n_seq * n_tok
    extc_ref[:, CONV_PAD:CONV_PAD + n_tok, :] = sec("qkv").reshape(n_seq, n_tok, 3 * D_MODEL)

    seg_c = CONV_PAD + n_tok
    convw_ref = refs["convw"]
    e = extc_ref[...].reshape(n_seq * seg_c, 3 * D_MODEL)
    acc = e * convw_ref[0:1, :]
    for i in range(1, CONV_W):
        acc = pltpu.roll(acc, 1, axis=0) + e * convw_ref[i:i + 1, :]
    qkv = _silu(acc.reshape(n_seq, seg_c, 3 * D_MODEL)[:, CONV_PAD:, :].reshape(rows, 3 * D_MODEL))
    q = _per_head_rsqrt_scale(qkv[:, 0:D_MODEL], L2_EPS, False, HEAD ** -0.5)
    k = _per_head_rsqrt_scale(qkv[:, D_MODEL:2 * D_MODEL], L2_EPS, False, 1.0)
    v = qkv[:, 2 * D_MODEL:]

    extp_ref[:, POOL_PAD:POOL_PAD + n_tok, :] = sec("u_a").reshape(n_seq, n_tok, D_MODEL)
    z_a, ba, z_b, ga, gb = sec("z_a"), sec("ba"), sec("z_b"), sec("ga"), sec("gb")

    seg_p = POOL_PAD + n_tok
    pooled = []
    for gi, w in enumerate(POOL_WINDOWS):
        e = extp_ref[:, :, gi * POOL_GROUP:(gi + 1) * POOL_GROUP].reshape(n_seq * seg_p, POOL_GROUP)
        acc, span = e, 1
        while span < w:
            acc = acc + pltpu.roll(acc, span, axis=0)
            span *= 2
        take = lambda a: a.reshape(n_seq, seg_p, POOL_GROUP)[:, POOL_PAD:, :].reshape(rows, POOL_GROUP)
        inv_cnt = 1.0 / jnp.minimum(pos + 1, w).astype(_F32)
        pooled.append(take(acc) * inv_cnt - take(e))
    mixed = jnp.concatenate([_dot(pp, refs["poolw"][gi]) for gi, pp in enumerate(pooled)], axis=1)
    y_a = (mixed * refs["pscale"][...] * _silu(z_a)).astype(_BF)
    a_proj = _dot(y_a, refs["pa"][...])

    beta_e, beta_5, gc_e, gc_5 = _expand_heads(_gate_scalars(ba, refs["alog"], refs["dtb"], n_tok), refs["e3"])
    return dict(q=q, k=k, v=v, beta_e=beta_e, beta_5=beta_5, gc_e=gc_e, gc_5=gc_5,
                a_proj=a_proj, z_b=z_b, ga=ga, gb=gb)


def _back(x, o, f, gate, refs):
    o = _per_head_rsqrt_scale(o, RMS_EPS, True, refs["hnw"][...])
    y_b = (o * _silu(f["z_b"])).astype(_BF)
    merged = _sigmoid(f["ga"]) * f["a_proj"] + _sigmoid(f["gb"]) * _dot(y_b, refs["pb"][...])
    sub = (1.0 + gate) * _dot(merged, refs["wout"][...])
    r = DEEPNORM_ALPHA * x + sub
    mu = jnp.mean(r, axis=-1, keepdims=True)
    rc = r - mu
    var = jnp.mean(rc * rc, axis=-1, keepdims=True)
    return rc * lax.rsqrt(var + LN_EPS) * refs["lng"][...] + refs["lnb"][...]


_WEIGHT_NAMES = ("w_t", "convw", "alog", "dtb", "hnw", "poolw", "pscale", "pa", "pb",
                 "wout", "lng", "lnb", "e3")
_N_PROJ_WEIGHTS = 1
_SAMPLE_WEIGHT_NAMES = _WEIGHT_NAMES[_N_PROJ_WEIGHTS:]


def _ada_kernel(c_ref, w_ref, b_ref, o_ref):
    o_ref[...] = _dot(_silu(c_ref[...]), w_ref[...]) + b_ref[...]


def _ada_call(c_all, w_ada, b_ada):
    n = c_all.shape[0]
    return pl.pallas_call(
        _ada_kernel,
        grid=(3 * D_MODEL // ADA_COLS,),
        in_specs=[pl.BlockSpec((n, D_MODEL), lambda j: (0, 0)),
                  pl.BlockSpec((D_MODEL, ADA_COLS), lambda j: (0, j)),
                  pl.BlockSpec((1, ADA_COLS), lambda j: (0, j))],
        out_specs=pl.BlockSpec((n, ADA_COLS), lambda j: (0, j)),
        out_shape=jax.ShapeDtypeStruct((n, 3 * D_MODEL), _F32),
        compiler_params=pltpu.CompilerParams(dimension_semantics=("arbitrary",)),
        name="adaln_mod",
    )(c_all, w_ada, b_ada)


def _prompt_kernel(x_ref, mod_ref, xs_ref, mods_ref, *rest, n_seq, n_tok, n_steps, n_groups):
    refs = dict(zip(_WEIGHT_NAMES, rest[:len(_WEIGHT_NAMES)]))
    (y_ref, pool_out_ref, conv_out_ref, delta_out_ref, projs_ref,
     extp_ref, extc_ref, sbd_ref) = rest[len(_WEIGHT_NAMES):]
    t = pl.program_id(1)
    rows = n_seq * n_tok

    @pl.when(t == 0)
    def _():
        extp_ref[:, 0:POOL_PAD, :] = jnp.zeros((n_seq, POOL_PAD, D_MODEL), _F32)
        extc_ref[:, 0:CONV_PAD, :] = jnp.zeros((n_seq, CONV_PAD, 3 * D_MODEL), _F32)
        sbd_ref[...] = jnp.zeros(sbd_ref.shape, _F32)

    x3 = x_ref[...]
    mod3 = mod_ref[...]
    pos = t * n_tok + lax.broadcasted_iota(jnp.int32, (rows, 1), 0) % n_tok
    h = jnp.concatenate([_modulated(x3, mod3), _modulated(xs_ref[...], mods_ref[...])], axis=0)
    w_rows = dict(SEC, ba=(OFF_MAIN_END, OFF_MAIN_END + HEAD), ga=(OFF_GATE, OFF_GATE + D_MODEL),
                  gb=(OFF_GATE + D_MODEL, OFF_GATE + 2 * D_MODEL))
    w_t_ref = refs["w_t"]

    def sec(name):
        r0, r1 = w_rows[name]
        full = lax.dot_general(h, w_t_ref[r0:r1, :], (((1,), (1,)), ((), ())), preferred_element_type=_F32)
        projs_ref[:, SEC[name][0]:SEC[name][1]] = full[rows:]
        return full[:rows]

    f = _front(sec, pos, refs, extp_ref, extc_ref, n_seq, n_tok)
    chunks = _chunks_intra(f["q"], f["k"], f["v"], f["beta_e"], f["beta_5"], f["gc_e"], f["gc_5"], n_tok)

    pair = [slice(p * 2 * HEAD, (p + 1) * 2 * HEAD) for p in range(N_PAIRS)]
    row_bd = lax.broadcasted_iota(jnp.int32, (2 * HEAD, 2 * HEAD), 0) // HEAD
    col_bd = lax.broadcasted_iota(jnp.int32, (2 * HEAD, 2 * HEAD), 1) // HEAD
    same_head = row_bd == col_bd
    us, ois = [], []
    for c, ck in enumerate(chunks):
        u_parts, oi_parts = [], []
        for p, sl in enumerate(pair):
            r = _dot(jnp.concatenate([ck["w_k"][:, sl], ck["qg"][:, sl]], axis=0), sbd_ref[c * N_PAIRS + p])
            u_parts.append(ck["w_v"][:, sl] - r[:CHUNK])
            oi_parts.append(r[CHUNK:])
        us.append(jnp.concatenate(u_parts, axis=1))
        ois.append(jnp.concatenate(oi_parts, axis=1))
    o = jnp.concatenate([ois[c] + _intra_output(ck["qkd"], us[c]) for c, ck in enumerate(chunks)], axis=0)
    for c, ck in enumerate(chunks):
        for p, sl in enumerate(pair):
            upd = _dot(ck["kt"][:, sl].T, us[c][:, sl])
            i = c * N_PAIRS + p
            sbd_ref[i] = ck["gl"][0:1, sl] * sbd_ref[i] + jnp.where(same_head, upd, 0.0)

    gate = jnp.broadcast_to(mod3[:, :, 2 * D_MODEL:], (n_seq, n_tok, D_MODEL)).reshape(rows, D_MODEL)
    y = _back(x3.reshape(rows, D_MODEL), o, f, gate, refs)
    y_ref[...] = y.reshape(n_seq, n_tok, D_MODEL)

    @pl.when(t == n_steps - 1)
    def _():
        for s in range(n_seq):
            for hd in range(N_HEADS):
                o0 = (hd % 2) * HEAD
                delta_out_ref[s, hd] = sbd_ref[s * N_PAIRS + hd // 2, o0:o0 + HEAD, o0:o0 + HEAD]

    for grp in range(n_groups):
        @pl.when((t == n_steps - 1) & (pl.program_id(0) == grp))
        def _():
            _rows_to_state(extp_ref, POOL_PAD + n_tok - POOL_BUF, POOL_BUF, pool_out_ref, grp * n_seq, n_seq)
            _rows_to_state(extc_ref, CONV_PAD + n_tok - (CONV_W - 1), CONV_W - 1, conv_out_ref, grp * n_seq, n_seq)

    @pl.when(t < n_steps - 1)
    def _():
        extp_ref[:, 0:POOL_PAD, :] = extp_ref[:, n_tok:n_tok + POOL_PAD, :]
        extc_ref[:, 0:CONV_PAD, :] = extc_ref[:, n_tok:n_tok + CONV_PAD, :]


def _const_spec(shape):
    nd = len(shape)
    return pl.BlockSpec(shape, lambda *_: (0,) * nd, pipeline_mode=pl.Buffered(1))


def _weight_specs(weights):
    return [_const_spec(w.shape) for w in weights]


def _prompt_call(x, mod, xs, mods, weights, n_seq, n_tok):
    bsz, seq, _ = x.shape
    n_steps = seq // n_tok
    n_sample, s_tok, _ = xs.shape
    assert n_tok == CHUNK and bsz % n_seq == 0 and seq % n_tok == 0
    assert n_sample == SAMPLE_RIDE * (bsz // n_seq) * n_steps
    ride = lambda shape: pl.BlockSpec((SAMPLE_RIDE,) + shape, lambda g, t: (g * n_steps + t,) + (0,) * len(shape))
    kern = functools.partial(_prompt_kernel, n_seq=n_seq, n_tok=n_tok, n_steps=n_steps, n_groups=bsz // n_seq)
    seq_block = lambda shape: pl.BlockSpec((n_seq,) + shape, lambda g, t: (g,) + (0,) * len(shape))
    return pl.pallas_call(
        kern,
        grid=(bsz // n_seq, n_steps),
        in_specs=[pl.BlockSpec((n_seq, n_tok, D_MODEL), lambda g, t: (g, t, 0)),
                  seq_block((1, 3 * D_MODEL)), ride((s_tok, D_MODEL)), ride((1, 3 * D_MODEL))]
                 + _weight_specs(weights),
        out_specs=[pl.BlockSpec((n_seq, n_tok, D_MODEL), lambda g, t: (g, t, 0)),
                   pl.BlockSpec((POOL_BUF, bsz, D_MODEL), lambda g, t: (0, 0, 0)),
                   pl.BlockSpec((CONV_W - 1, bsz, 3 * D_MODEL), lambda g, t: (0, 0, 0)),
                   seq_block((N_HEADS, HEAD, HEAD)),
                   pl.BlockSpec((SAMPLE_RIDE * s_tok, SEC_WIDTH), lambda g, t: (g * n_steps + t, 0))],
        out_shape=[jax.ShapeDtypeStruct((bsz, seq, D_MODEL), _F32),
                   jax.ShapeDtypeStruct((POOL_BUF, bsz, D_MODEL), _F32),
                   jax.ShapeDtypeStruct((CONV_W - 1, bsz, 3 * D_MODEL), _F32),
                   jax.ShapeDtypeStruct((bsz, N_HEADS, HEAD, HEAD), _F32),
                   jax.ShapeDtypeStruct((n_sample * s_tok, SEC_WIDTH), _F32)],
        scratch_shapes=[pltpu.VMEM((n_seq, POOL_PAD + n_tok, D_MODEL), _F32),
                        pltpu.VMEM((n_seq, CONV_PAD + n_tok, 3 * D_MODEL), _F32),
                        pltpu.VMEM((n_seq * N_PAIRS, 2 * HEAD, 2 * HEAD), _F32)],
        compiler_params=pltpu.CompilerParams(dimension_semantics=("arbitrary", "arbitrary"),
                                             vmem_limit_bytes=VMEM_LIMIT_BYTES),
        name="prompt_layer",
    )(x, mod, xs, mods, *weights)


def _sample_kernel(x_ref, mod_ref, projs_ref, spool_ref, sconv_ref, sdelta_hbm, *rest, n_seq, n_tok, n_steps, pos0):
    refs = dict(zip(_SAMPLE_WEIGHT_NAMES, rest[:len(_SAMPLE_WEIGHT_NAMES)]))
    (y_ref, pool_out_ref, conv_out_ref, delta_out_hbm,
     extp_ref, extc_ref, sin_ref, sout_ref, in_sem, out_sem) = rest[len(_SAMPLE_WEIGHT_NAMES):]
    i = pl.program_id(0)
    rows = n_seq * n_tok
    per_chunk = CHUNK // n_tok
    n_chunks = n_seq // per_chunk

    def state_in(step, c):
        first = (step * n_chunks + c) * per_chunk
        return pltpu.make_async_copy(sdelta_hbm.at[pl.ds(first, per_chunk)], sin_ref.at[c], in_sem.at[c])

    def state_out(step, c):
        first = (step * n_chunks + c) * per_chunk
        return pltpu.make_async_copy(sout_ref.at[c], delta_out_hbm.at[pl.ds(first, per_chunk)], out_sem.at[c])

    @pl.when(i == 0)
    def _():
        for c in range(n_chunks):
            state_in(0, c).start()

    _state_to_rows(spool_ref, extp_ref, POOL_PAD - POOL_BUF, POOL_BUF, n_seq)
    extp_ref[:, 0:POOL_PAD - POOL_BUF, :] = jnp.zeros((n_seq, POOL_PAD - POOL_BUF, D_MODEL), _F32)
    _state_to_rows(sconv_ref, extc_ref, CONV_PAD - (CONV_W - 1), CONV_W - 1, n_seq)
    extc_ref[:, 0:CONV_PAD - (CONV_W - 1), :] = jnp.zeros((n_seq, CONV_PAD - (CONV_W - 1), 3 * D_MODEL), _F32)

    x3 = x_ref[...]
    mod3 = mod_ref[...]
    pos = pos0 + lax.broadcasted_iota(jnp.int32, (rows, 1), 0) % n_tok
    f = _front(lambda name: projs_ref[:, SEC[name][0]:SEC[name][1]], pos, refs, extp_ref, extc_ref, n_seq, n_tok)
    _rows_to_state(extp_ref, POOL_PAD + n_tok - POOL_BUF, POOL_BUF, pool_out_ref, 0, n_seq)
    _rows_to_state(extc_ref, CONV_PAD + n_tok - (CONV_W - 1), CONV_W - 1, conv_out_ref, 0, n_seq)
    chunks = _chunks_intra(f["q"], f["k"], f["v"], f["beta_e"], f["beta_5"], f["gc_e"], f["gc_5"], n_tok)

    lane_seq = lax.broadcasted_iota(jnp.int32, (HEAD, CHUNK), 1) // n_tok
    os_ = []
    for c, ck in enumerate(chunks):
        state_in(i, c).wait()

        @pl.when(i > 0)
        def _():
            state_out(i - 1, c).wait()

        u_rows, oi_rows = [], []
        for sl_ in range(per_chunk):
            rs = slice(sl_ * n_tok, (sl_ + 1) * n_tok)
            u_h, oi_h = [], []
            for hd in range(N_HEADS):
                cs = slice(hd * HEAD, (hd + 1) * HEAD)
                r = _dot(jnp.concatenate([ck["w_k"][rs, cs], ck["qg"][rs, cs]], axis=0), sin_ref[c, sl_, hd])
                u_h.append(ck["w_v"][rs, cs] - r[:n_tok])
                oi_h.append(r[n_tok:])
            u_rows.append(jnp.concatenate(u_h, axis=1))
            oi_rows.append(jnp.concatenate(oi_h, axis=1))
        u = jnp.concatenate(u_rows, axis=0)
        os_.append(jnp.concatenate(oi_rows, axis=0) + _intra_output(ck["qkd"], u))
        for hd in range(N_HEADS):
            cs = slice(hd * HEAD, (hd + 1) * HEAD)
            kt_t = ck["kt"][:, cs].T
            lhs = jnp.concatenate([jnp.where(lane_seq == sl_, kt_t, 0.0) for sl_ in range(per_chunk)], axis=0)
            upd = _dot(lhs, u[:, cs])
            for sl_ in range(per_chunk):
                gl = ck["gl"][sl_ * n_tok:sl_ * n_tok + 1, cs]
                sout_ref[c, sl_, hd] = gl * sin_ref[c, sl_, hd] + upd[sl_ * HEAD:(sl_ + 1) * HEAD]
        state_out(i, c).start()

        @pl.when(i + 1 < n_steps)
        def _():
            state_in(i + 1, c).start()
    o = jnp.concatenate(os_, axis=0)

    gate = jnp.broadcast_to(mod3[:, :, 2 * D_MODEL:], (n_seq, n_tok, D_MODEL)).reshape(rows, D_MODEL)
    y = _back(x3.reshape(rows, D_MODEL), o, f, gate, refs)
    y_ref[...] = y.reshape(n_seq, n_tok, D_MODEL)

    @pl.when(i == n_steps - 1)
    def _():
        for c in range(n_chunks):
            state_out(i, c).wait()


def _sample_call(x, mod, projs, spool, sconv, sdelta, weights, pos0):
    bsz, n_tok, _ = x.shape
    n_seq = SAMPLE_SEQS
    assert (n_seq * n_tok) % CHUNK == 0 and bsz % n_seq == 0
    n_steps = bsz // n_seq
    n_chunks = n_seq * n_tok // CHUNK
    kern = functools.partial(_sample_kernel, n_seq=n_seq, n_tok=n_tok, n_steps=n_steps, pos0=pos0)
    seq_block = lambda shape: pl.BlockSpec((n_seq,) + shape, lambda i: (i,) + (0,) * len(shape))
    row_state = lambda n_rows, width: pl.BlockSpec((n_rows, n_seq, width), lambda i: (0, i, 0))
    slot = (n_chunks, CHUNK // n_tok, N_HEADS, HEAD, HEAD)
    return pl.pallas_call(
        kern,
        grid=(n_steps,),
        in_specs=[seq_block((n_tok, D_MODEL)), seq_block((1, 3 * D_MODEL)),
                  pl.BlockSpec((n_seq * n_tok, SEC_WIDTH), lambda i: (i, 0)),
                  row_state(POOL_BUF, D_MODEL), row_state(CONV_W - 1, 3 * D_MODEL),
                  pl.BlockSpec(memory_space=pl.ANY)] + [_const_spec(w.shape) for w in weights],
        out_specs=[seq_block((n_tok, D_MODEL)), row_state(POOL_BUF, D_MODEL),
                   row_state(CONV_W - 1, 3 * D_MODEL), pl.BlockSpec(memory_space=pl.ANY)],
        out_shape=[jax.ShapeDtypeStruct((bsz, n_tok, D_MODEL), _F32),
                   jax.ShapeDtypeStruct((POOL_BUF, bsz, D_MODEL), _F32),
                   jax.ShapeDtypeStruct((CONV_W - 1, bsz, 3 * D_MODEL), _F32),
                   jax.ShapeDtypeStruct((bsz, N_HEADS, HEAD, HEAD), _F32)],
        scratch_shapes=[pltpu.VMEM((n_seq, POOL_PAD + n_tok, D_MODEL), _F32),
                        pltpu.VMEM((n_seq, CONV_PAD + n_tok, 3 * D_MODEL), _F32),
                        pltpu.VMEM(slot, _F32), pltpu.VMEM(slot, _F32),
                        pltpu.SemaphoreType.DMA((n_chunks,)), pltpu.SemaphoreType.DMA((n_chunks,))],
        compiler_params=pltpu.CompilerParams(dimension_semantics=("arbitrary",),
                                             vmem_limit_bytes=VMEM_LIMIT_BYTES),
        name="sample_layer",
    )(x, mod, projs, spool, sconv, sdelta, *weights)


def _head_expansion_matrix():
    e = np.zeros((HEAD, 3072), np.float32)
    for piece in range(3):
        for hd in range(N_HEADS):
            rb, rg = piece * 2 * N_HEADS + hd, piece * 2 * N_HEADS + N_HEADS + hd
            e[rb, hd * HEAD:(hd + 1) * HEAD] = 1.0
            e[rb, 1024 + hd * CHUNK:1024 + (hd + 1) * CHUNK] = 1.0
            e[rg, 1536 + hd * HEAD:1536 + (hd + 1) * HEAD] = 1.0
            e[rg, 2560 + hd * CHUNK:2560 + (hd + 1) * CHUNK] = 1.0
    return jnp.asarray(e, _BF)


def _layer_weights(w_in, conv_w, a_log, dt_bias, head_norm_w, pool_w, pool_scale, p_a, p_b, w_out, ln_g, ln_b):
    lane_pad = lambda a: jnp.zeros((1, HEAD), _F32).at[0, N_HEADS:2 * N_HEADS].set(a.reshape(N_HEADS))
    return (jnp.transpose(w_in).astype(_BF), conv_w,
            lane_pad(a_log), lane_pad(dt_bias), head_norm_w.reshape(1, HEAD), pool_w.astype(_BF),
            pool_scale.reshape(1, D_MODEL), p_a.astype(_BF), p_b.astype(_BF), w_out.astype(_BF),
            ln_g.reshape(1, D_MODEL), ln_b.reshape(1, D_MODEL), _head_expansion_matrix())


def kernel(x_prompt, x_sample, state_pool, state_conv, state_delta, c_prompt, c_sample, w_ada, b_ada, w_in, conv_w, a_log, dt_bias, head_norm_w, pool_w, pool_scale, p_a, p_b, w_out, ln_g, ln_b):
    assert w_in.shape[0] == 1, "single-layer trunk"
    bp, bs = x_prompt.shape[0], x_sample.shape[0]
    drop = lambda a: a.reshape(a.shape[1:])
    mod = _ada_call(jnp.concatenate([c_prompt, c_sample], axis=0), drop(w_ada), b_ada)
    mod = mod.reshape(bp + bs, 1, 3 * D_MODEL)
    weights = _layer_weights(drop(w_in), drop(conv_w), a_log, dt_bias, head_norm_w, drop(pool_w), pool_scale,
                             drop(p_a), drop(p_b), drop(w_out), ln_g, ln_b)
    rows_major = lambda a: jnp.transpose(a, (1, 0, 2))
    y_p, pool_p, conv_p, delta_p, proj_s = _prompt_call(x_prompt, mod[:bp], x_sample, mod[bp:], weights,
                                                        PROMPT_SEQS, PROMPT_TOKENS)
    y_s, pool_s, conv_s, delta_s = _sample_call(x_sample, mod[bp:], proj_s, rows_major(drop(state_pool)),
                                                rows_major(drop(state_conv)), drop(state_delta),
                                                weights[_N_PROJ_WEIGHTS:], PAST_LEN)
    lift = lambda a: a.reshape((1,) + a.shape)
    return (y_p, y_s, lift(rows_major(pool_p)), lift(rows_major(conv_p)), lift(delta_p),
            lift(rows_major(pool_s)), lift(rows_major(conv_s)), lift(delta_s))
```

```python
import functools

import numpy as np
import jax
import jax.numpy as jnp
from jax import lax
from jax.experimental import pallas as pl
from jax.experimental.pallas import tpu as pltpu

D_MODEL = 1024
N_HEADS = 8
HEAD = 128
N_PAIRS = N_HEADS // 2
POOL_WINDOWS = (2, 4, 8, 16)
POOL_GROUP = 256
POOL_BUF = 15
POOL_PAD = 16
CONV_W = 4
CONV_PAD = 8
CHUNK = 64
INV_GROUP = 128
INV_BASE = 8
PAST_LEN = 16384
DEEPNORM_ALPHA = 2.0 ** 0.25
LN_EPS = 1e-5
RMS_EPS = 1e-6
L2_EPS = 1e-6
NEG_BIG = -1e30

OFF_MAIN_END = 6 * D_MODEL
OFF_GATE = OFF_MAIN_END + 2 * N_HEADS

SEC = dict(u_a=(0, D_MODEL), z_a=(D_MODEL, 2 * D_MODEL), qkv=(2 * D_MODEL, 5 * D_MODEL),
           z_b=(5 * D_MODEL, 6 * D_MODEL), ba=(OFF_MAIN_END, OFF_MAIN_END + 128),
           ga=(OFF_MAIN_END + 128, OFF_MAIN_END + 128 + D_MODEL),
           gb=(OFF_MAIN_END + 128 + D_MODEL, OFF_MAIN_END + 128 + 2 * D_MODEL))
SEC_WIDTH = OFF_MAIN_END + 128 + 2 * D_MODEL

PROMPT_SEQS = 4
SAMPLE_RIDE = 2
PROMPT_TOKENS = CHUNK
SAMPLE_SEQS = 16
ADA_COLS = 1024
W_SLAB = 432
VMEM_LIMIT_BYTES =58 * 1024 * 1024

_BF = jnp.bfloat16
_F32 = jnp.float32


def _dot(a, b):
    return jnp.dot(a.astype(_BF), b.astype(_BF), preferred_element_type=_F32)


def _dot_nt(a, b):
    return lax.dot_general(a.astype(_BF), b.astype(_BF), (((1,), (1,)), ((), ())),
                           preferred_element_type=_F32)


def _sigmoid(x):
    return 0.5 + 0.5 * jnp.tanh(0.5 * x)


def _silu(x):
    hx = 0.5 * x
    return hx + hx * jnp.tanh(hx)


def _softplus(x):
    return jnp.maximum(x, 0.0) + jnp.log1p(jnp.exp(-jnp.abs(x)))


def _lane_block_diag(x, width):
    nblk = x.shape[1] // width
    lane = lax.broadcasted_iota(jnp.int32, x.shape, 1)
    return jnp.concatenate([jnp.where(lane // width == i, x, 0.0) for i in range(nblk)], axis=0)


def _pair_block_diag(x):
    lane = lax.broadcasted_iota(jnp.int32, x.shape, 1) % (2 * HEAD)
    return jnp.concatenate([jnp.where(lane < HEAD, x, 0.0), jnp.where(lane >= HEAD, x, 0.0)], axis=0)


def _segment_cumsum(x, seg):
    row = lax.broadcasted_iota(jnp.int32, x.shape, 0) % seg
    shift = 1
    while shift < seg:
        x = x + jnp.where(row >= shift, pltpu.roll(x, shift, axis=0), 0.0)
        shift *= 2
    return x


def _expand_heads(narrow, e3_ref):
    lane = lax.broadcasted_iota(jnp.int32, narrow.shape, 1)
    x = jnp.where(lane < 2 * N_HEADS, narrow, 0.0)
    hi = x.astype(_BF).astype(_F32)
    rem = x - hi
    mid = rem.astype(_BF).astype(_F32)
    lo = (rem - mid).astype(_BF).astype(_F32)
    packed = hi + pltpu.roll(mid, 2 * N_HEADS, axis=1) + pltpu.roll(lo, 4 * N_HEADS, axis=1)
    wide = jnp.dot(packed.astype(_BF), e3_ref[...], preferred_element_type=_F32)
    return wide[:, 0:1024], wide[:, 1024:1536], wide[:, 1536:2560], wide[:, 2560:3072]


def _per_head_rsqrt_scale(x, eps, mean, post):
    outs = []
    for h in range(N_HEADS):
        xh = x[:, h * HEAD:(h + 1) * HEAD]
        ss = jnp.sum(xh * xh, axis=-1, keepdims=True)
        if mean:
            ss = ss * (1.0 / HEAD)
        outs.append(xh * (lax.rsqrt(ss + eps) * post))
    return jnp.concatenate(outs, axis=1)


def _gate_scalars(ba, alog_ref, dtb_ref, seg):
    beta = _sigmoid(ba)
    g = -jnp.exp(alog_ref[...]) * _softplus(ba + dtb_ref[...])
    gc = _segment_cumsum(g, seg)
    lane = lax.broadcasted_iota(jnp.int32, ba.shape, 1)
    return jnp.where(lane < N_HEADS, beta, gc)


def _chunk_masks(n_tok):
    row = lax.broadcasted_iota(jnp.int32, (CHUNK, 4 * HEAD), 0)
    col = lax.broadcasted_iota(jnp.int32, (CHUNK, 4 * HEAD), 1) % CHUNK
    same_seq = (row // n_tok) == (col // n_tok)
    return same_seq & (row >= col), same_seq & (row > col), row == col


def _chunks_intra(q, k, v, beta_e, beta_5, gc_e, gc_5, n_tok):
    assert n_tok >= INV_BASE and n_tok & (n_tok - 1) == 0 and CHUNK % n_tok == 0
    n_chunks = q.shape[0] // CHUNK
    incl, strict, eye5 = _chunk_masks(n_tok)
    rows = [slice(c * CHUNK, (c + 1) * CHUNK) for c in range(n_chunks)]
    pair = [slice(p * 2 * HEAD, (p + 1) * 2 * HEAD) for p in range(N_PAIRS)]

    kk, qk = [], []
    for rs in rows:
        kk_parts, qk_parts = [], []
        for sl in pair:
            kp, qp = k[rs, sl], q[rs, sl]
            g2 = _dot_nt(jnp.concatenate([kp, qp], axis=0), _pair_block_diag(kp))
            kk_parts.append(g2[:CHUNK])
            qk_parts.append(g2[CHUNK:])
        kk.append(jnp.concatenate(kk_parts, axis=1))
        qk.append(jnp.concatenate(qk_parts, axis=1))

    dec, glast = [], []
    for rs in rows:
        g5 = gc_5[rs]
        gc_row = jnp.sum(jnp.where(eye5, g5, 0.0), axis=0, keepdims=True)
        dec.append(jnp.exp(jnp.where(incl, g5 - gc_row, NEG_BIG)))
        ge = gc_e[rs]
        glast.append(jnp.concatenate(
            [jnp.broadcast_to(ge[(s + 1) * n_tok - 1:(s + 1) * n_tok, :], (n_tok, D_MODEL))
             for s in range(CHUNK // n_tok)], axis=0))

    row4 = lax.broadcasted_iota(jnp.int32, (CHUNK, INV_GROUP), 0)
    col4 = lax.broadcasted_iota(jnp.int32, (CHUNK, INV_GROUP), 1) % CHUNK
    base_blk = (row4 // INV_BASE) == (col4 // INV_BASE)
    prob = [(c, grp) for c in range(n_chunks) for grp in range(N_HEADS * CHUNK // INV_GROUP)]
    neg_l, t_inv, pw = {}, {}, {}
    for c, grp in prob:
        gs = slice(grp * INV_GROUP, (grp + 1) * INV_GROUP)
        neg_l[c, grp] = jnp.where(strict[:, :INV_GROUP],
                                  -(beta_5[rows[c], gs] * dec[c][:, gs] * kk[c][:, gs]), 0.0)
    for key in prob:
        m0 = jnp.where(base_blk, neg_l[key], 0.0)
        t_inv[key] = eye5[:, :INV_GROUP].astype(_F32) + m0
        pw[key] = _dot(m0, _lane_block_diag(m0, CHUNK))
    for key in prob:
        r = _dot(jnp.concatenate([pw[key], t_inv[key]], axis=0), _lane_block_diag(pw[key], CHUNK))
        pw[key] = r[:CHUNK]
        t_inv[key] = t_inv[key] + r[CHUNK:]
    for key in prob:
        t_inv[key] = t_inv[key] + _dot(t_inv[key], _lane_block_diag(pw[key], CHUNK))
    blk = INV_BASE
    while blk < n_tok:
        below = ((row4 // (2 * blk)) == (col4 // (2 * blk))) & ((row4 // blk) != (col4 // blk))
        for key in prob:
            pw[key] = _dot(jnp.where(below, neg_l[key], 0.0), _lane_block_diag(t_inv[key], CHUNK))
        for key in prob:
            t_inv[key] = t_inv[key] + _dot(t_inv[key], _lane_block_diag(pw[key], CHUNK))
        blk *= 2
    s = t_inv

    out = []
    for c, rs in enumerate(rows):
        gam = jnp.exp(gc_e[rs])
        be = beta_e[rs]
        bv = be * v[rs]
        gbk = be * gam * k[rs]
        wv_parts, wk_parts = [], []
        for p, sl in enumerate(pair):
            t_pair = s[c, p]
            sol = _dot(t_pair, _pair_block_diag(jnp.concatenate([bv[:, sl], gbk[:, sl]], axis=1)))
            wv_parts.append(sol[:, :2 * HEAD])
            wk_parts.append(sol[:, 2 * HEAD:])
        out.append(dict(w_v=jnp.concatenate(wv_parts, axis=1), w_k=jnp.concatenate(wk_parts, axis=1),
                        qg=q[rs] * gam, kt=k[rs] * jnp.exp(glast[c] - gc_e[rs]),
                        qkd=qk[c] * dec[c], gl=jnp.exp(glast[c])))
    return out


def _intra_output(qkd, u):
    outs = []
    for p in range(N_PAIRS):
        up = u[:, p * 2 * HEAD:(p + 1) * 2 * HEAD]
        outs.append(_dot(qkd[:, p * HEAD:(p + 1) * HEAD], _pair_block_diag(up)))
    return jnp.concatenate(outs, axis=1)


def _state_to_rows(state_ref, ext_ref, row0, n_rows, n_seq):
    for s in range(n_seq):
        for r in range(n_rows):
            ext_ref[s, row0 + r:row0 + r + 1, :] = state_ref[r, s:s + 1, :]


def _rows_to_state(ext_ref, row0, n_rows, state_ref, seq0, n_seq):
    for s in range(n_seq):
        for r in range(n_rows):
            state_ref[r, seq0 + s:seq0 + s + 1, :] = ext_ref[s, row0 + r:row0 + r + 1, :]


def _modulated(x3, mod3):
    shift, scale = mod3[:, :, 0:D_MODEL], mod3[:, :, D_MODEL:2 * D_MODEL]
    return (x3 * (1.0 + scale) + shift).reshape(x3.shape[0] * x3.shape[1], D_MODEL).astype(_BF)


def _front(sec, pos, refs, extp_ref, extc_ref, n_seq, n_tok):
    rows = n_seq * n_tok
    extc_ref[:, CONV_PAD:CONV_PAD + n_tok, :] = sec("qkv").reshape(n_seq, n_tok, 3 * D_MODEL)

    seg_c = CONV_PAD + n_tok
    convw_ref = refs["convw"]
    e = extc_ref[...].reshape(n_seq * seg_c, 3 * D_MODEL)
    acc = e * convw_ref[0:1, :]
    for i in range(1, CONV_W):
        acc = pltpu.roll(acc, 1, axis=0) + e * convw_ref[i:i + 1, :]
    qkv = _silu(acc.reshape(n_seq, seg_c, 3 * D_MODEL)[:, CONV_PAD:, :].reshape(rows, 3 * D_MODEL))
    q = _per_head_rsqrt_scale(qkv[:, 0:D_MODEL], L2_EPS, False, HEAD ** -0.5)
    k = _per_head_rsqrt_scale(qkv[:, D_MODEL:2 * D_MODEL], L2_EPS, False, 1.0)
    v = qkv[:, 2 * D_MODEL:]

    extp_ref[:, POOL_PAD:POOL_PAD + n_tok, :] = sec("u_a").reshape(n_seq, n_tok, D_MODEL)
    z_a, ba, z_b, ga, gb = sec("z_a"), sec("ba"), sec("z_b"), sec("ga"), sec("gb")

    seg_p = POOL_PAD + n_tok
    pooled = []
    for gi, w in enumerate(POOL_WINDOWS):
        e = extp_ref[:, :, gi * POOL_GROUP:(gi + 1) * POOL_GROUP].reshape(n_seq * seg_p, POOL_GROUP)
        acc, span = e, 1
        while span < w:
            acc = acc + pltpu.roll(acc, span, axis=0)
            span *= 2
        take = lambda a: a.reshape(n_seq, seg_p, POOL_GROUP)[:, POOL_PAD:, :].reshape(rows, POOL_GROUP)
        inv_cnt = 1.0 / jnp.minimum(pos + 1, w).astype(_F32)
        pooled.append(take(acc) * inv_cnt - take(e))
    mixed = jnp.concatenate([_dot(pp, refs["poolw"][gi]) for gi, pp in enumerate(pooled)], axis=1)
    y_a = (mixed * refs["pscale"][...] * _silu(z_a)).astype(_BF)
    a_proj = _dot(y_a, refs["pa"][...])

    beta_e, beta_5, gc_e, gc_5 = _expand_heads(_gate_scalars(ba, refs["alog"], refs["dtb"], n_tok), refs["e3"])
    return dict(q=q, k=k, v=v, beta_e=beta_e, beta_5=beta_5, gc_e=gc_e, gc_5=gc_5,
                a_proj=a_proj, z_b=z_b, ga=ga, gb=gb)


def _back(x, o, f, gate, refs):
    o = _per_head_rsqrt_scale(o, RMS_EPS, True, refs["hnw"][...])
    y_b = (o * _silu(f["z_b"])).astype(_BF)
    merged = _sigmoid(f["ga"]) * f["a_proj"] + _sigmoid(f["gb"]) * _dot(y_b, refs["pb"][...])
    sub = (1.0 + gate) * _dot(merged, refs["wout"][...])
    r = DEEPNORM_ALPHA * x + sub
    mu = jnp.mean(r, axis=-1, keepdims=True)
    rc = r - mu
    var = jnp.mean(rc * rc, axis=-1, keepdims=True)
    return rc * lax.rsqrt(var + LN_EPS) * refs["lng"][...] + refs["lnb"][...]


_WEIGHT_NAMES = ("w_t", "convw", "alog", "dtb", "hnw", "poolw", "pscale", "pa", "pb",
                 "wout", "lng", "lnb", "e3")
_N_PROJ_WEIGHTS = 1
_SAMPLE_WEIGHT_NAMES = _WEIGHT_NAMES[_N_PROJ_WEIGHTS:]


def _ada_kernel(c_ref, w_ref, b_ref, o_ref):
    o_ref[...] = _dot(_silu(c_ref[...]), w_ref[...]) + b_ref[...]


def _ada_call(c_all, w_ada, b_ada):
    n = c_all.shape[0]
    return pl.pallas_call(
        _ada_kernel,
        grid=(3 * D_MODEL // ADA_COLS,),
        in_specs=[pl.BlockSpec((n, D_MODEL), lambda j: (0, 0)),
                  pl.BlockSpec((D_MODEL, ADA_COLS), lambda j: (0, j)),
                  pl.BlockSpec((1, ADA_COLS), lambda j: (0, j))],
        out_specs=pl.BlockSpec((n, ADA_COLS), lambda j: (0, j)),
        out_shape=jax.ShapeDtypeStruct((n, 3 * D_MODEL), _F32),
        compiler_params=pltpu.CompilerParams(dimension_semantics=("arbitrary",)),
        name="adaln_mod",
    )(c_all, w_ada, b_ada)


def _prompt_kernel(x_ref, mod_ref, xs_ref, mods_ref, *rest, n_seq, n_tok, n_steps, n_groups):
    refs = dict(zip(_WEIGHT_NAMES, rest[:len(_WEIGHT_NAMES)]))
    (y_ref, pool_out_ref, conv_out_ref, delta_out_ref, projs_ref,
     extp_ref, extc_ref, sbd_ref, w_t_ref, stage_ref, stage_sem) = rest[len(_WEIGHT_NAMES):]
    t = pl.program_id(1)
    rows = n_seq * n_tok

    @pl.when((pl.program_id(0) == 0) & (t == 0))
    def _():
        n_slabs = w_t_ref.shape[0] // W_SLAB

        def slab_in(k):
            return pltpu.make_async_copy(refs["w_t"].at[pl.ds(k * W_SLAB, W_SLAB)], stage_ref.at[k % 2],
                                         stage_sem.at[k % 2])

        slab_in(0).start()
        slab_in(1).start()
        for k in range(n_slabs):
            slab_in(k).wait()
            w_t_ref[k * W_SLAB:(k + 1) * W_SLAB, :] = stage_ref[k % 2].astype(_BF)
            if k + 2 < n_slabs:
                slab_in(k + 2).start()

    @pl.when(t == 0)
    def _():
        extp_ref[:, 0:POOL_PAD, :] = jnp.zeros((n_seq, POOL_PAD, D_MODEL), _F32)
        extc_ref[:, 0:CONV_PAD, :] = jnp.zeros((n_seq, CONV_PAD, 3 * D_MODEL), _F32)
        sbd_ref[...] = jnp.zeros(sbd_ref.shape, _F32)

    x3 = x_ref[...]
    mod3 = mod_ref[...]
    pos = t * n_tok + lax.broadcasted_iota(jnp.int32, (rows, 1), 0) % n_tok
    h = jnp.concatenate([_modulated(x3, mod3), _modulated(xs_ref[...], mods_ref[...])], axis=0)
    w_rows = dict(SEC, ba=(OFF_MAIN_END, OFF_MAIN_END + HEAD), ga=(OFF_GATE, OFF_GATE + D_MODEL),
                  gb=(OFF_GATE + D_MODEL, OFF_GATE + 2 * D_MODEL))

    def sec(name):
        r0, r1 = w_rows[name]
        full = lax.dot_general(h, w_t_ref[r0:r1, :], (((1,), (1,)), ((), ())), preferred_element_type=_F32)
        projs_ref[:, SEC[name][0]:SEC[name][1]] = full[rows:]
        return full[:rows]

    f = _front(sec, pos, refs, extp_ref, extc_ref, n_seq, n_tok)
    chunks = _chunks_intra(f["q"], f["k"], f["v"], f["beta_e"], f["beta_5"], f["gc_e"], f["gc_5"], n_tok)

    pair = [slice(p * 2 * HEAD, (p + 1) * 2 * HEAD) for p in range(N_PAIRS)]
    row_bd = lax.broadcasted_iota(jnp.int32, (2 * HEAD, 2 * HEAD), 0) // HEAD
    col_bd = lax.broadcasted_iota(jnp.int32, (2 * HEAD, 2 * HEAD), 1) // HEAD
    same_head = row_bd == col_bd
    us, ois = [], []
    for c, ck in enumerate(chunks):
        u_parts, oi_parts = [], []
        for p, sl in enumerate(pair):
            r = _dot(jnp.concatenate([ck["w_k"][:, sl], ck["qg"][:, sl]], axis=0), sbd_ref[c * N_PAIRS + p])
            u_parts.append(ck["w_v"][:, sl] - r[:CHUNK])
            oi_parts.append(r[CHUNK:])
        us.append(jnp.concatenate(u_parts, axis=1))
        ois.append(jnp.concatenate(oi_parts, axis=1))
    o = jnp.concatenate([ois[c] + _intra_output(ck["qkd"], us[c]) for c, ck in enumerate(chunks)], axis=0)
    for c, ck in enumerate(chunks):
        for p, sl in enumerate(pair):
            upd = _dot(ck["kt"][:, sl].T, us[c][:, sl])
            i = c * N_PAIRS + p
            sbd_ref[i] = ck["gl"][0:1, sl] * sbd_ref[i] + jnp.where(same_head, upd, 0.0)

    gate = jnp.broadcast_to(mod3[:, :, 2 * D_MODEL:], (n_seq, n_tok, D_MODEL)).reshape(rows, D_MODEL)
    y = _back(x3.reshape(rows, D_MODEL), o, f, gate, refs)
    y_ref[...] = y.reshape(n_seq, n_tok, D_MODEL)

    @pl.when(t == n_steps - 1)
    def _():
        for s in range(n_seq):
            for hd in range(N_HEADS):
                o0 = (hd % 2) * HEAD
                delta_out_ref[s, hd] = sbd_ref[s * N_PAIRS + hd // 2, o0:o0 + HEAD, o0:o0 + HEAD]

    for grp in range(n_groups):
        @pl.when((t == n_steps - 1) & (pl.program_id(0) == grp))
        def _():
            _rows_to_state(extp_ref, POOL_PAD + n_tok - POOL_BUF, POOL_BUF, pool_out_ref, grp * n_seq, n_seq)
            _rows_to_state(extc_ref, CONV_PAD + n_tok - (CONV_W - 1), CONV_W - 1, conv_out_ref, grp * n_seq, n_seq)

    @pl.when(t < n_steps - 1)
    def _():
        extp_ref[:, 0:POOL_PAD, :] = extp_ref[:, n_tok:n_tok + POOL_PAD, :]
        extc_ref[:, 0:CONV_PAD, :] = extc_ref[:, n_tok:n_tok + CONV_PAD, :]


def _const_spec(shape):
    nd = len(shape)
    return pl.BlockSpec(shape, lambda *_: (0,) * nd, pipeline_mode=pl.Buffered(1))


def _weight_specs(weights):
    return [pl.BlockSpec(memory_space=pl.ANY)] + [_const_spec(w.shape) for w in weights[_N_PROJ_WEIGHTS:]]


def _prompt_call(x, mod, xs, mods, weights, n_seq, n_tok):
    bsz, seq, _ = x.shape
    n_steps = seq // n_tok
    n_sample, s_tok, _ = xs.shape
    assert n_tok == CHUNK and bsz % n_seq == 0 and seq % n_tok == 0
    w_t = weights[0]
    assert w_t.shape[0] % W_SLAB == 0 and w_t.shape[0] // W_SLAB >= 2
    assert n_sample == SAMPLE_RIDE * (bsz // n_seq) * n_steps
    ride = lambda shape: pl.BlockSpec((SAMPLE_RIDE,) + shape, lambda g, t: (g * n_steps + t,) + (0,) * len(shape))
    kern = functools.partial(_prompt_kernel, n_seq=n_seq, n_tok=n_tok, n_steps=n_steps, n_groups=bsz // n_seq)
    seq_block = lambda shape: pl.BlockSpec((n_seq,) + shape, lambda g, t: (g,) + (0,) * len(shape))
    return pl.pallas_call(
        kern,
        grid=(bsz // n_seq, n_steps),
        in_specs=[pl.BlockSpec((n_seq, n_tok, D_MODEL), lambda g, t: (g, t, 0)),
                  seq_block((1, 3 * D_MODEL)), ride((s_tok, D_MODEL)), ride((1, 3 * D_MODEL))]
                 + _weight_specs(weights),
        out_specs=[pl.BlockSpec((n_seq, n_tok, D_MODEL), lambda g, t: (g, t, 0)),
                   pl.BlockSpec((POOL_BUF, bsz, D_MODEL), lambda g, t: (0, 0, 0)),
                   pl.BlockSpec((CONV_W - 1, bsz, 3 * D_MODEL), lambda g, t: (0, 0, 0)),
                   seq_block((N_HEADS, HEAD, HEAD)),
                   pl.BlockSpec((SAMPLE_RIDE * s_tok, SEC_WIDTH), lambda g, t: (g * n_steps + t, 0))],
        out_shape=[jax.ShapeDtypeStruct((bsz, seq, D_MODEL), _F32),
                   jax.ShapeDtypeStruct((POOL_BUF, bsz, D_MODEL), _F32),
                   jax.ShapeDtypeStruct((CONV_W - 1, bsz, 3 * D_MODEL), _F32),
                   jax.ShapeDtypeStruct((bsz, N_HEADS, HEAD, HEAD), _F32),
                   jax.ShapeDtypeStruct((n_sample * s_tok, SEC_WIDTH), _F32)],
        scratch_shapes=[pltpu.VMEM((n_seq, POOL_PAD + n_tok, D_MODEL), _F32),
                        pltpu.VMEM((n_seq, CONV_PAD + n_tok, 3 * D_MODEL), _F32),
                        pltpu.VMEM((n_seq * N_PAIRS, 2 * HEAD, 2 * HEAD), _F32),
                        pltpu.VMEM(w_t.shape, _BF), pltpu.VMEM((2, W_SLAB, D_MODEL), _F32),
                        pltpu.SemaphoreType.DMA((2,))],
        compiler_params=pltpu.CompilerParams(dimension_semantics=("arbitrary", "arbitrary"),
                                             vmem_limit_bytes=VMEM_LIMIT_BYTES),
        name="prompt_layer",
    )(x, mod, xs, mods, *weights)


def _sample_kernel(x_ref, mod_ref, projs_ref, spool_ref, sconv_ref, sdelta_hbm, *rest, n_seq, n_tok, n_steps, pos0):
    refs = dict(zip(_SAMPLE_WEIGHT_NAMES, rest[:len(_SAMPLE_WEIGHT_NAMES)]))
    (y_ref, pool_out_ref, conv_out_ref, delta_out_hbm,
     extp_ref, extc_ref, sin_ref, sout_ref, in_sem, out_sem) = rest[len(_SAMPLE_WEIGHT_NAMES):]
    i = pl.program_id(0)
    rows = n_seq * n_tok
    per_chunk = CHUNK // n_tok
    n_chunks = n_seq // per_chunk

    def state_in(step, c):
        first = (step * n_chunks + c) * per_chunk
        return pltpu.make_async_copy(sdelta_hbm.at[pl.ds(first, per_chunk)], sin_ref.at[c], in_sem.at[c])

    def state_out(step, c):
        first = (step * n_chunks + c) * per_chunk
        return pltpu.make_async_copy(sout_ref.at[c], delta_out_hbm.at[pl.ds(first, per_chunk)], out_sem.at[c])

    @pl.when(i == 0)
    def _():
        for c in range(n_chunks):
            state_in(0, c).start()

    _state_to_rows(spool_ref, extp_ref, POOL_PAD - POOL_BUF, POOL_BUF, n_seq)
    extp_ref[:, 0:POOL_PAD - POOL_BUF, :] = jnp.zeros((n_seq, POOL_PAD - POOL_BUF, D_MODEL), _F32)
    _state_to_rows(sconv_ref, extc_ref, CONV_PAD - (CONV_W - 1), CONV_W - 1, n_seq)
    extc_ref[:, 0:CONV_PAD - (CONV_W - 1), :] = jnp.zeros((n_seq, CONV_PAD - (CONV_W - 1), 3 * D_MODEL), _F32)

    x3 = x_ref[...]
    mod3 = mod_ref[...]
    pos = pos0 + lax.broadcasted_iota(jnp.int32, (rows, 1), 0) % n_tok
    f = _front(lambda name: projs_ref[:, SEC[name][0]:SEC[name][1]], pos, refs, extp_ref, extc_ref, n_seq, n_tok)
    _rows_to_state(extp_ref, POOL_PAD + n_tok - POOL_BUF, POOL_BUF, pool_out_ref, 0, n_seq)
    _rows_to_state(extc_ref, CONV_PAD + n_tok - (CONV_W - 1), CONV_W - 1, conv_out_ref, 0, n_seq)
    chunks = _chunks_intra(f["q"], f["k"], f["v"], f["beta_e"], f["beta_5"], f["gc_e"], f["gc_5"], n_tok)

    lane_seq = lax.broadcasted_iota(jnp.int32, (HEAD, CHUNK), 1) // n_tok
    os_ = []
    for c, ck in enumerate(chunks):
        state_in(i, c).wait()

        @pl.when(i > 0)
        def _():
            state_out(i - 1, c).wait()

        u_rows, oi_rows = [], []
        for sl_ in range(per_chunk):
            rs = slice(sl_ * n_tok, (sl_ + 1) * n_tok)
            u_h, oi_h = [], []
            for hd in range(N_HEADS):
                cs = slice(hd * HEAD, (hd + 1) * HEAD)
                r = _dot(jnp.concatenate([ck["w_k"][rs, cs], ck["qg"][rs, cs]], axis=0), sin_ref[c, sl_, hd])
                u_h.append(ck["w_v"][rs, cs] - r[:n_tok])
                oi_h.append(r[n_tok:])
            u_rows.append(jnp.concatenate(u_h, axis=1))
            oi_rows.append(jnp.concatenate(oi_h, axis=1))
        u = jnp.concatenate(u_rows, axis=0)
        os_.append(jnp.concatenate(oi_rows, axis=0) + _intra_output(ck["qkd"], u))
        for hd in range(N_HEADS):
            cs = slice(hd * HEAD, (hd + 1) * HEAD)
            kt_t = ck["kt"][:, cs].T
            lhs = jnp.concatenate([jnp.where(lane_seq == sl_, kt_t, 0.0) for sl_ in range(per_chunk)], axis=0)
            upd = _dot(lhs, u[:, cs])
            for sl_ in range(per_chunk):
                gl = ck["gl"][sl_ * n_tok:sl_ * n_tok + 1, cs]
                sout_ref[c, sl_, hd] = gl * sin_ref[c, sl_, hd] + upd[sl_ * HEAD:(sl_ + 1) * HEAD]
        state_out(i, c).start()

        @pl.when(i + 1 < n_steps)
        def _():
            state_in(i + 1, c).start()
    o = jnp.concatenate(os_, axis=0)

    gate = jnp.broadcast_to(mod3[:, :, 2 * D_MODEL:], (n_seq, n_tok, D_MODEL)).reshape(rows, D_MODEL)
    y = _back(x3.reshape(rows, D_MODEL), o, f, gate, refs)
    y_ref[...] = y.reshape(n_seq, n_tok, D_MODEL)

    @pl.when(i == n_steps - 1)
    def _():
        for c in range(n_chunks):
            state_out(i, c).wait()


def _sample_call(x, mod, projs, spool, sconv, sdelta, weights, pos0):
    bsz, n_tok, _ = x.shape
    n_seq = SAMPLE_SEQS
    assert (n_seq * n_tok) % CHUNK == 0 and bsz % n_seq == 0
    n_steps = bsz // n_seq
    n_chunks = n_seq * n_tok // CHUNK
    kern = functools.partial(_sample_kernel, n_seq=n_seq, n_tok=n_tok, n_steps=n_steps, pos0=pos0)
    seq_block = lambda shape: pl.BlockSpec((n_seq,) + shape, lambda i: (i,) + (0,) * len(shape))
    row_state = lambda n_rows, width: pl.BlockSpec((n_rows, n_seq, width), lambda i: (0, i, 0))
    slot = (n_chunks, CHUNK // n_tok, N_HEADS, HEAD, HEAD)
    return pl.pallas_call(
        kern,
        grid=(n_steps,),
        in_specs=[seq_block((n_tok, D_MODEL)), seq_block((1, 3 * D_MODEL)),
                  pl.BlockSpec((n_seq * n_tok, SEC_WIDTH), lambda i: (i, 0)),
                  row_state(POOL_BUF, D_MODEL), row_state(CONV_W - 1, 3 * D_MODEL),
                  pl.BlockSpec(memory_space=pl.ANY)] + [_const_spec(w.shape) for w in weights],
        out_specs=[seq_block((n_tok, D_MODEL)), row_state(POOL_BUF, D_MODEL),
                   row_state(CONV_W - 1, 3 * D_MODEL), pl.BlockSpec(memory_space=pl.ANY)],
        out_shape=[jax.ShapeDtypeStruct((bsz, n_tok, D_MODEL), _F32),
                   jax.ShapeDtypeStruct((POOL_BUF, bsz, D_MODEL), _F32),
                   jax.ShapeDtypeStruct((CONV_W - 1, bsz, 3 * D_MODEL), _F32),
                   jax.ShapeDtypeStruct((bsz, N_HEADS, HEAD, HEAD), _F32)],
        scratch_shapes=[pltpu.VMEM((n_seq, POOL_PAD + n_tok, D_MODEL), _F32),
                        pltpu.VMEM((n_seq, CONV_PAD + n_tok, 3 * D_MODEL), _F32),
                        pltpu.VMEM(slot, _F32), pltpu.VMEM(slot, _F32),
                        pltpu.SemaphoreType.DMA((n_chunks,)), pltpu.SemaphoreType.DMA((n_chunks,))],
        compiler_params=pltpu.CompilerParams(dimension_semantics=("arbitrary",),
                                             vmem_limit_bytes=VMEM_LIMIT_BYTES),
        name="sample_layer",
    )(x, mod, projs, spool, sconv, sdelta, *weights)


def _head_expansion_matrix():
    e = np.zeros((HEAD, 3072), np.float32)
    for piece in range(3):
        for hd in range(N_HEADS):
            rb, rg = piece * 2 * N_HEADS + hd, piece * 2 * N_HEADS + N_HEADS + hd
            e[rb, hd * HEAD:(hd + 1) * HEAD] = 1.0
            e[rb, 1024 + hd * CHUNK:1024 + (hd + 1) * CHUNK] = 1.0
            e[rg, 1536 + hd * HEAD:1536 + (hd + 1) * HEAD] = 1.0
            e[rg, 2560 + hd * CHUNK:2560 + (hd + 1) * CHUNK] = 1.0
    return jnp.asarray(e, _BF)


def _layer_weights(w_in, conv_w, a_log, dt_bias, head_norm_w, pool_w, pool_scale, p_a, p_b, w_out, ln_g, ln_b):
    lane_pad = lambda a: jnp.zeros((1, HEAD), _F32).at[0, N_HEADS:2 * N_HEADS].set(a.reshape(N_HEADS))
    return (jnp.transpose(w_in), conv_w,
            lane_pad(a_log), lane_pad(dt_bias), head_norm_w.reshape(1, HEAD), pool_w.astype(_BF),
            pool_scale.reshape(1, D_MODEL), p_a.astype(_BF), p_b.astype(_BF), w_out.astype(_BF),
            ln_g.reshape(1, D_MODEL), ln_b.reshape(1, D_MODEL), _head_expansion_matrix())


def kernel(x_prompt, x_sample, state_pool, state_conv, state_delta, c_prompt, c_sample, w_ada, b_ada, w_in, conv_w, a_log, dt_bias, head_norm_w, pool_w, pool_scale, p_a, p_b, w_out, ln_g, ln_b):
    assert w_in.shape[0] == 1, "single-layer trunk"
    bp, bs = x_prompt.shape[0], x_sample.shape[0]
    drop = lambda a: a.reshape(a.shape[1:])
    mod = _ada_call(jnp.concatenate([c_prompt, c_sample], axis=0), drop(w_ada), b_ada)
    mod = mod.reshape(bp + bs, 1, 3 * D_MODEL)
    weights = _layer_weights(drop(w_in), drop(conv_w), a_log, dt_bias, head_norm_w, drop(pool_w), pool_scale,
                             drop(p_a), drop(p_b), drop(w_out), ln_g, ln_b)
    rows_major = lambda a: jnp.transpose(a, (1, 0, 2))
    y_p, pool_p, conv_p, delta_p, proj_s = _prompt_call(x_prompt, mod[:bp], x_sample, mod[bp:], weights,
                                                        PROMPT_SEQS, PROMPT_TOKENS)
    y_s, pool_s, conv_s, delta_s = _sample_call(x_sample, mod[bp:], proj_s, rows_major(drop(state_pool)),
                                                rows_major(drop(state_conv)), drop(state_delta),
                                                weights[_N_PROJ_WEIGHTS:], PAST_LEN)
    lift = lambda a: a.reshape((1,) + a.shape)
    return (y_p, y_s, lift(rows_major(pool_p)), lift(rows_major(conv_p)), lift(delta_p),
            lift(rows_major(pool_s)), lift(rows_major(conv_s)), lift(delta_s))
```

```python
import functools

import numpy as np
import jax
import jax.numpy as jnp
from jax import lax
from jax.experimental import pallas as pl
from jax.experimental.pallas import tpu as pltpu

D_MODEL = 1024
N_HEADS = 8
HEAD = 128
N_PAIRS = N_HEADS // 2
POOL_WINDOWS = (2, 4, 8, 16)
POOL_GROUP = 256
POOL_BUF = 15
POOL_PAD = 16
CONV_W = 4
CONV_PAD = 8
CHUNK = 64
INV_GROUP = 128
INV_BASE = 8
PAST_LEN = 16384
DEEPNORM_ALPHA = 2.0 ** 0.25
LN_EPS = 1e-5
RMS_EPS = 1e-6
L2_EPS = 1e-6
NEG_BIG = -1e30

OFF_MAIN_END = 6 * D_MODEL
OFF_GATE = OFF_MAIN_END + 2 * N_HEADS

SEC = dict(u_a=(0, D_MODEL), z_a=(D_MODEL, 2 * D_MODEL), qkv=(2 * D_MODEL, 5 * D_MODEL),
           z_b=(5 * D_MODEL, 6 * D_MODEL), ba=(OFF_MAIN_END, OFF_MAIN_END + 128),
           ga=(OFF_MAIN_END + 128, OFF_MAIN_END + 128 + D_MODEL),
           gb=(OFF_MAIN_END + 128 + D_MODEL, OFF_MAIN_END + 128 + 2 * D_MODEL))
SEC_WIDTH = OFF_MAIN_END + 128 + 2 * D_MODEL

PROMPT_SEQS = 4
SAMPLE_RIDE = 2
PROMPT_TOKENS = CHUNK
SAMPLE_SEQS = 16
ADA_COLS = 1024
W_SLAB = 432
VMEM_LIMIT_BYTES =58 * 1024 * 1024

_BF = jnp.bfloat16
_F32 = jnp.float32


def _dot(a, b):
    return jnp.dot(a.astype(_BF), b.astype(_BF), preferred_element_type=_F32)


def _dot_nt(a, b):
    return lax.dot_general(a.astype(_BF), b.astype(_BF), (((1,), (1,)), ((), ())),
                           preferred_element_type=_F32)


def _sigmoid(x):
    return 0.5 + 0.5 * jnp.tanh(0.5 * x)


def _silu(x):
    hx = 0.5 * x
    return hx + hx * jnp.tanh(hx)


def _softplus(x):
    return jnp.maximum(x, 0.0) + jnp.log1p(jnp.exp(-jnp.abs(x)))


def _lane_block_diag(x, width):
    nblk = x.shape[1] // width
    lane = lax.broadcasted_iota(jnp.int32, x.shape, 1)
    return jnp.concatenate([jnp.where(lane // width == i, x, 0.0) for i in range(nblk)], axis=0)


def _pair_block_diag(x):
    lane = lax.broadcasted_iota(jnp.int32, x.shape, 1) % (2 * HEAD)
    return jnp.concatenate([jnp.where(lane < HEAD, x, 0.0), jnp.where(lane >= HEAD, x, 0.0)], axis=0)


def _segment_cumsum(x, seg):
    row = lax.broadcasted_iota(jnp.int32, x.shape, 0) % seg
    shift = 1
    while shift < seg:
        x = x + jnp.where(row >= shift, pltpu.roll(x, shift, axis=0), 0.0)
        shift *= 2
    return x


def _expand_heads(narrow, e3_ref):
    lane = lax.broadcasted_iota(jnp.int32, narrow.shape, 1)
    x = jnp.where(lane < 2 * N_HEADS, narrow, 0.0)
    hi = x.astype(_BF).astype(_F32)
    rem = x - hi
    mid = rem.astype(_BF).astype(_F32)
    lo = (rem - mid).astype(_BF).astype(_F32)
    packed = hi + pltpu.roll(mid, 2 * N_HEADS, axis=1) + pltpu.roll(lo, 4 * N_HEADS, axis=1)
    wide = jnp.dot(packed.astype(_BF), e3_ref[...], preferred_element_type=_F32)
    return wide[:, 0:1024], wide[:, 1024:1536], wide[:, 1536:2560], wide[:, 2560:3072]


def _per_head_rsqrt_scale(x, eps, mean, post):
    outs = []
    for h in range(N_HEADS):
        xh = x[:, h * HEAD:(h + 1) * HEAD]
        ss = jnp.sum(xh * xh, axis=-1, keepdims=True)
        if mean:
            ss = ss * (1.0 / HEAD)
        outs.append(xh * (lax.rsqrt(ss + eps) * post))
    return jnp.concatenate(outs, axis=1)


def _gate_scalars(ba, gatep_ref, seg):
    beta = _sigmoid(ba)
    g = -jnp.exp(gatep_ref[0:1, :]) * _softplus(ba + gatep_ref[1:2, :])
    gc = _segment_cumsum(g, seg)
    lane = lax.broadcasted_iota(jnp.int32, ba.shape, 1)
    return jnp.where(lane < N_HEADS, beta, gc)


def _chunk_masks(n_tok):
    row = lax.broadcasted_iota(jnp.int32, (CHUNK, 4 * HEAD), 0)
    col = lax.broadcasted_iota(jnp.int32, (CHUNK, 4 * HEAD), 1) % CHUNK
    same_seq = (row // n_tok) == (col // n_tok)
    return same_seq & (row >= col), same_seq & (row > col), row == col


def _chunks_intra(q, k, v, beta_e, beta_5, gc_e, gc_5, n_tok):
    assert n_tok >= INV_BASE and n_tok & (n_tok - 1) == 0 and CHUNK % n_tok == 0
    n_chunks = q.shape[0] // CHUNK
    incl, strict, eye5 = _chunk_masks(n_tok)
    rows = [slice(c * CHUNK, (c + 1) * CHUNK) for c in range(n_chunks)]
    pair = [slice(p * 2 * HEAD, (p + 1) * 2 * HEAD) for p in range(N_PAIRS)]

    kk, qk = [], []
    for rs in rows:
        kk_parts, qk_parts = [], []
        for sl in pair:
            kp, qp = k[rs, sl], q[rs, sl]
            g2 = _dot_nt(jnp.concatenate([kp, qp], axis=0), _pair_block_diag(kp))
            kk_parts.append(g2[:CHUNK])
            qk_parts.append(g2[CHUNK:])
        kk.append(jnp.concatenate(kk_parts, axis=1))
        qk.append(jnp.concatenate(qk_parts, axis=1))

    dec, glast = [], []
    for rs in rows:
        g5 = gc_5[rs]
        gc_row = jnp.sum(jnp.where(eye5, g5, 0.0), axis=0, keepdims=True)
        dec.append(jnp.exp(jnp.where(incl, g5 - gc_row, NEG_BIG)))
        ge = gc_e[rs]
        glast.append(jnp.concatenate(
            [jnp.broadcast_to(ge[(s + 1) * n_tok - 1:(s + 1) * n_tok, :], (n_tok, D_MODEL))
             for s in range(CHUNK // n_tok)], axis=0))

    row4 = lax.broadcasted_iota(jnp.int32, (CHUNK, INV_GROUP), 0)
    col4 = lax.broadcasted_iota(jnp.int32, (CHUNK, INV_GROUP), 1) % CHUNK
    base_blk = (row4 // INV_BASE) == (col4 // INV_BASE)
    prob = [(c, grp) for c in range(n_chunks) for grp in range(N_HEADS * CHUNK // INV_GROUP)]
    neg_l, t_inv, pw = {}, {}, {}
    for c, grp in prob:
        gs = slice(grp * INV_GROUP, (grp + 1) * INV_GROUP)
        neg_l[c, grp] = jnp.where(strict[:, :INV_GROUP],
                                  -(beta_5[rows[c], gs] * dec[c][:, gs] * kk[c][:, gs]), 0.0)
    for key in prob:
        m0 = jnp.where(base_blk, neg_l[key], 0.0)
        t_inv[key] = eye5[:, :INV_GROUP].astype(_F32) + m0
        pw[key] = _dot(m0, _lane_block_diag(m0, CHUNK))
    for key in prob:
        r = _dot(jnp.concatenate([pw[key], t_inv[key]], axis=0), _lane_block_diag(pw[key], CHUNK))
        pw[key] = r[:CHUNK]
        t_inv[key] = t_inv[key] + r[CHUNK:]
    for key in prob:
        t_inv[key] = t_inv[key] + _dot(t_inv[key], _lane_block_diag(pw[key], CHUNK))
    blk = INV_BASE
    while blk < n_tok:
        below = ((row4 // (2 * blk)) == (col4 // (2 * blk))) & ((row4 // blk) != (col4 // blk))
        for key in prob:
            pw[key] = _dot(jnp.where(below, neg_l[key], 0.0), _lane_block_diag(t_inv[key], CHUNK))
        for key in prob:
            t_inv[key] = t_inv[key] + _dot(t_inv[key], _lane_block_diag(pw[key], CHUNK))
        blk *= 2
    s = t_inv

    out = []
    for c, rs in enumerate(rows):
        gam = jnp.exp(gc_e[rs])
        be = beta_e[rs]
        bv = be * v[rs]
        gbk = be * gam * k[rs]
        wv_parts, wk_parts = [], []
        for p, sl in enumerate(pair):
            t_pair = s[c, p]
            sol = _dot(t_pair, _pair_block_diag(jnp.concatenate([bv[:, sl], gbk[:, sl]], axis=1)))
            wv_parts.append(sol[:, :2 * HEAD])
            wk_parts.append(sol[:, 2 * HEAD:])
        out.append(dict(w_v=jnp.concatenate(wv_parts, axis=1), w_k=jnp.concatenate(wk_parts, axis=1),
                        qg=q[rs] * gam, kt=k[rs] * jnp.exp(glast[c] - gc_e[rs]),
                        qkd=qk[c] * dec[c], gl=jnp.exp(glast[c])))
    return out


def _intra_output(qkd, u):
    outs = []
    for p in range(N_PAIRS):
        up = u[:, p * 2 * HEAD:(p + 1) * 2 * HEAD]
        outs.append(_dot(qkd[:, p * HEAD:(p + 1) * HEAD], _pair_block_diag(up)))
    return jnp.concatenate(outs, axis=1)


def _state_to_rows(state_ref, ext_ref, row0, n_rows, n_seq):
    for s in range(n_seq):
        for r in range(n_rows):
            ext_ref[s, row0 + r:row0 + r + 1, :] = state_ref[r, s:s + 1, :]


def _rows_to_state(ext_ref, row0, n_rows, state_ref, seq0, n_seq):
    for s in range(n_seq):
        for r in range(n_rows):
            state_ref[r, seq0 + s:seq0 + s + 1, :] = ext_ref[s, row0 + r:row0 + r + 1, :]


def _modulated(x3, mod3):
    shift, scale = mod3[:, :, 0:D_MODEL], mod3[:, :, D_MODEL:2 * D_MODEL]
    return (x3 * (1.0 + scale) + shift).reshape(x3.shape[0] * x3.shape[1], D_MODEL).astype(_BF)


def _front(sec, pos, refs, extp_ref, extc_ref, n_seq, n_tok):
    rows = n_seq * n_tok
    extc_ref[:, CONV_PAD:CONV_PAD + n_tok, :] = sec("qkv").reshape(n_seq, n_tok, 3 * D_MODEL)

    seg_c = CONV_PAD + n_tok
    convw_ref = refs["convw"]
    e = extc_ref[...].reshape(n_seq * seg_c, 3 * D_MODEL)
    acc = e * convw_ref[0:1, :]
    for i in range(1, CONV_W):
        acc = pltpu.roll(acc, 1, axis=0) + e * convw_ref[i:i + 1, :]
    qkv = _silu(acc.reshape(n_seq, seg_c, 3 * D_MODEL)[:, CONV_PAD:, :].reshape(rows, 3 * D_MODEL))
    q = _per_head_rsqrt_scale(qkv[:, 0:D_MODEL], L2_EPS, False, HEAD ** -0.5)
    k = _per_head_rsqrt_scale(qkv[:, D_MODEL:2 * D_MODEL], L2_EPS, False, 1.0)
    v = qkv[:, 2 * D_MODEL:]

    extp_ref[:, POOL_PAD:POOL_PAD + n_tok, :] = sec("u_a").reshape(n_seq, n_tok, D_MODEL)
    z_a, ba, z_b, ga, gb = sec("z_a"), sec("ba"), sec("z_b"), sec("ga"), sec("gb")

    seg_p = POOL_PAD + n_tok
    pooled = []
    for gi, w in enumerate(POOL_WINDOWS):
        e = extp_ref[:, :, gi * POOL_GROUP:(gi + 1) * POOL_GROUP].reshape(n_seq * seg_p, POOL_GROUP)
        acc, span = e, 1
        while span < w:
            acc = acc + pltpu.roll(acc, span, axis=0)
            span *= 2
        take = lambda a: a.reshape(n_seq, seg_p, POOL_GROUP)[:, POOL_PAD:, :].reshape(rows, POOL_GROUP)
        inv_cnt = 1.0 / jnp.minimum(pos + 1, w).astype(_F32)
        pooled.append(take(acc) * inv_cnt - take(e))
    mixed = jnp.concatenate([_dot(pp, refs["poolw"][gi]) for gi, pp in enumerate(pooled)], axis=1)
    y_a = (mixed * refs["pscale"][...] * _silu(z_a)).astype(_BF)
    a_proj = _dot(y_a, refs["pa"][...])

    beta_e, beta_5, gc_e, gc_5 = _expand_heads(_gate_scalars(ba, refs["gatep"], n_tok), refs["e3"])
    return dict(q=q, k=k, v=v, beta_e=beta_e, beta_5=beta_5, gc_e=gc_e, gc_5=gc_5,
                a_proj=a_proj, z_b=z_b, ga=ga, gb=gb)


def _back(x, o, f, gate, refs):
    o = _per_head_rsqrt_scale(o, RMS_EPS, True, refs["hnw"][...])
    y_b = (o * _silu(f["z_b"])).astype(_BF)
    merged = _sigmoid(f["ga"]) * f["a_proj"] + _sigmoid(f["gb"]) * _dot(y_b, refs["pb"][...])
    sub = (1.0 + gate) * _dot(merged, refs["wout"][...])
    r = DEEPNORM_ALPHA * x + sub
    mu = jnp.mean(r, axis=-1, keepdims=True)
    rc = r - mu
    var = jnp.mean(rc * rc, axis=-1, keepdims=True)
    return rc * lax.rsqrt(var + LN_EPS) * refs["lng"][...] + refs["lnb"][...]


_WEIGHT_NAMES = ("w_t", "convw", "gatep", "hnw", "poolw", "pscale", "pa", "pb",
                 "wout", "lng", "lnb", "e3")
_N_PROJ_WEIGHTS = 1
_SAMPLE_WEIGHT_NAMES = _WEIGHT_NAMES[_N_PROJ_WEIGHTS:]


def _ada_kernel(c_ref, w_ref, b_ref, o_ref):
    o_ref[:, 0, :] = _dot(_silu(c_ref[...]), w_ref[...]) + b_ref[...]


def _ada_call(c_all, w_ada, b_ada):
    n = c_all.shape[0]
    return pl.pallas_call(
        _ada_kernel,
        grid=(3 * D_MODEL // ADA_COLS,),
        in_specs=[pl.BlockSpec((n, D_MODEL), lambda j: (0, 0)),
                  pl.BlockSpec((D_MODEL, ADA_COLS), lambda j: (0, j)),
                  pl.BlockSpec((1, ADA_COLS), lambda j: (0, j))],
        out_specs=pl.BlockSpec((n, 1, ADA_COLS), lambda j: (0, 0, j)),
        out_shape=jax.ShapeDtypeStruct((n, 1, 3 * D_MODEL), _F32),
        compiler_params=pltpu.CompilerParams(dimension_semantics=("arbitrary",)),
        name="adaln_mod",
    )(c_all, w_ada, b_ada)


def _prompt_kernel(x_ref, mod_ref, xs_ref, mods_ref, *rest, n_seq, n_tok, n_steps, n_groups):
    refs = dict(zip(_WEIGHT_NAMES, rest[:len(_WEIGHT_NAMES)]))
    (y_ref, pool_out_ref, conv_out_ref, delta_out_ref, projs_ref,
     extp_ref, extc_ref, sbd_ref, w_t_ref, stage_ref, stage_sem) = rest[len(_WEIGHT_NAMES):]
    t = pl.program_id(1)
    rows = n_seq * n_tok

    @pl.when((pl.program_id(0) == 0) & (t == 0))
    def _():
        n_slabs = w_t_ref.shape[0] // W_SLAB

        def slab_in(k):
            return pltpu.make_async_copy(refs["w_t"].at[pl.ds(k * W_SLAB, W_SLAB)], stage_ref.at[k % 2],
                                         stage_sem.at[k % 2])

        slab_in(0).start()
        slab_in(1).start()
        for k in range(n_slabs):
            slab_in(k).wait()
            w_t_ref[k * W_SLAB:(k + 1) * W_SLAB, :] = stage_ref[k % 2].astype(_BF)
            if k + 2 < n_slabs:
                slab_in(k + 2).start()

    @pl.when(t == 0)
    def _():
        extp_ref[:, 0:POOL_PAD, :] = jnp.zeros((n_seq, POOL_PAD, D_MODEL), _F32)
        extc_ref[:, 0:CONV_PAD, :] = jnp.zeros((n_seq, CONV_PAD, 3 * D_MODEL), _F32)
        sbd_ref[...] = jnp.zeros(sbd_ref.shape, _F32)

    x3 = x_ref[...]
    mod3 = mod_ref[...]
    pos = t * n_tok + lax.broadcasted_iota(jnp.int32, (rows, 1), 0) % n_tok
    h = jnp.concatenate([_modulated(x3, mod3), _modulated(xs_ref[...], mods_ref[...])], axis=0)
    w_rows = dict(SEC, ba=(OFF_MAIN_END, OFF_MAIN_END + HEAD), ga=(OFF_GATE, OFF_GATE + D_MODEL),
                  gb=(OFF_GATE + D_MODEL, OFF_GATE + 2 * D_MODEL))

    def sec(name):
        r0, r1 = w_rows[name]
        full = lax.dot_general(h, w_t_ref[r0:r1, :], (((1,), (1,)), ((), ())), preferred_element_type=_F32)
        projs_ref[:, SEC[name][0]:SEC[name][1]] = full[rows:]
        return full[:rows]

    f = _front(sec, pos, refs, extp_ref, extc_ref, n_seq, n_tok)
    chunks = _chunks_intra(f["q"], f["k"], f["v"], f["beta_e"], f["beta_5"], f["gc_e"], f["gc_5"], n_tok)

    pair = [slice(p * 2 * HEAD, (p + 1) * 2 * HEAD) for p in range(N_PAIRS)]
    row_bd = lax.broadcasted_iota(jnp.int32, (2 * HEAD, 2 * HEAD), 0) // HEAD
    col_bd = lax.broadcasted_iota(jnp.int32, (2 * HEAD, 2 * HEAD), 1) // HEAD
    same_head = row_bd == col_bd
    us, ois = [], []
    for c, ck in enumerate(chunks):
        u_parts, oi_parts = [], []
        for p, sl in enumerate(pair):
            r = _dot(jnp.concatenate([ck["w_k"][:, sl], ck["qg"][:, sl]], axis=0), sbd_ref[c * N_PAIRS + p])
            u_parts.append(ck["w_v"][:, sl] - r[:CHUNK])
            oi_parts.append(r[CHUNK:])
        us.append(jnp.concatenate(u_parts, axis=1))
        ois.append(jnp.concatenate(oi_parts, axis=1))
    o = jnp.concatenate([ois[c] + _intra_output(ck["qkd"], us[c]) for c, ck in enumerate(chunks)], axis=0)
    for c, ck in enumerate(chunks):
        for p, sl in enumerate(pair):
            upd = _dot(ck["kt"][:, sl].T, us[c][:, sl])
            i = c * N_PAIRS + p
            sbd_ref[i] = ck["gl"][0:1, sl] * sbd_ref[i] + jnp.where(same_head, upd, 0.0)

    gate = jnp.broadcast_to(mod3[:, :, 2 * D_MODEL:], (n_seq, n_tok, D_MODEL)).reshape(rows, D_MODEL)
    y = _back(x3.reshape(rows, D_MODEL), o, f, gate, refs)
    y_ref[...] = y.reshape(n_seq, n_tok, D_MODEL)

    @pl.when(t == n_steps - 1)
    def _():
        for s in range(n_seq):
            for hd in range(N_HEADS):
                o0 = (hd % 2) * HEAD
                delta_out_ref[s, hd] = sbd_ref[s * N_PAIRS + hd // 2, o0:o0 + HEAD, o0:o0 + HEAD]

    for grp in range(n_groups):
        @pl.when((t == n_steps - 1) & (pl.program_id(0) == grp))
        def _():
            _rows_to_state(extp_ref, POOL_PAD + n_tok - POOL_BUF, POOL_BUF, pool_out_ref, grp * n_seq, n_seq)
            _rows_to_state(extc_ref, CONV_PAD + n_tok - (CONV_W - 1), CONV_W - 1, conv_out_ref, grp * n_seq, n_seq)

    @pl.when(t < n_steps - 1)
    def _():
        extp_ref[:, 0:POOL_PAD, :] = extp_ref[:, n_tok:n_tok + POOL_PAD, :]
        extc_ref[:, 0:CONV_PAD, :] = extc_ref[:, n_tok:n_tok + CONV_PAD, :]


def _const_spec(shape):
    nd = len(shape)
    return pl.BlockSpec(shape, lambda *_: (0,) * nd, pipeline_mode=pl.Buffered(1))


def _weight_specs(weights):
    return [pl.BlockSpec(memory_space=pl.ANY)] + [_const_spec(w.shape) for w in weights[_N_PROJ_WEIGHTS:]]


def _prompt_call(x, mod, xs, weights, n_seq, n_tok):
    bsz, seq, _ = x.shape
    n_steps = seq // n_tok
    n_sample, s_tok, _ = xs.shape
    assert n_tok == CHUNK and bsz % n_seq == 0 and seq % n_tok == 0
    w_t = weights[0]
    assert w_t.shape[0] % W_SLAB == 0 and w_t.shape[0] // W_SLAB >= 2
    assert n_sample == SAMPLE_RIDE * (bsz // n_seq) * n_steps
    assert n_sample % n_seq == 0 and mod.shape[0] == n_sample + bsz
    mod_blk0 = n_sample // n_seq
    ride = lambda shape: pl.BlockSpec((SAMPLE_RIDE,) + shape, lambda g, t: (g * n_steps + t,) + (0,) * len(shape))
    kern = functools.partial(_prompt_kernel, n_seq=n_seq, n_tok=n_tok, n_steps=n_steps, n_groups=bsz // n_seq)
    seq_block = lambda shape: pl.BlockSpec((n_seq,) + shape, lambda g, t: (g,) + (0,) * len(shape))
    return pl.pallas_call(
        kern,
        grid=(bsz // n_seq, n_steps),
        in_specs=[pl.BlockSpec((n_seq, n_tok, D_MODEL), lambda g, t: (g, t, 0)),
                  pl.BlockSpec((n_seq, 1, 3 * D_MODEL), lambda g, t: (mod_blk0 + g, 0, 0)),
                  ride((s_tok, D_MODEL)), ride((1, 3 * D_MODEL))]
                 + _weight_specs(weights),
        out_specs=[pl.BlockSpec((n_seq, n_tok, D_MODEL), lambda g, t: (g, t, 0)),
                   pl.BlockSpec((POOL_BUF, bsz, D_MODEL), lambda g, t: (0, 0, 0)),
                   pl.BlockSpec((CONV_W - 1, bsz, 3 * D_MODEL), lambda g, t: (0, 0, 0)),
                   seq_block((N_HEADS, HEAD, HEAD)),
                   pl.BlockSpec((SAMPLE_RIDE * s_tok, SEC_WIDTH), lambda g, t: (g * n_steps + t, 0))],
        out_shape=[jax.ShapeDtypeStruct((bsz, seq, D_MODEL), _F32),
                   jax.ShapeDtypeStruct((POOL_BUF, bsz, D_MODEL), _F32),
                   jax.ShapeDtypeStruct((CONV_W - 1, bsz, 3 * D_MODEL), _F32),
                   jax.ShapeDtypeStruct((bsz, N_HEADS, HEAD, HEAD), _F32),
                   jax.ShapeDtypeStruct((n_sample * s_tok, SEC_WIDTH), _F32)],
        scratch_shapes=[pltpu.VMEM((n_seq, POOL_PAD + n_tok, D_MODEL), _F32),
                        pltpu.VMEM((n_seq, CONV_PAD + n_tok, 3 * D_MODEL), _F32),
                        pltpu.VMEM((n_seq * N_PAIRS, 2 * HEAD, 2 * HEAD), _F32),
                        pltpu.VMEM(w_t.shape, _BF), pltpu.VMEM((2, W_SLAB, D_MODEL), _F32),
                        pltpu.SemaphoreType.DMA((2,))],
        compiler_params=pltpu.CompilerParams(dimension_semantics=("arbitrary", "arbitrary"),
                                             vmem_limit_bytes=VMEM_LIMIT_BYTES),
        name="prompt_layer",
    )(x, mod, xs, mod, *weights)


def _sample_kernel(x_ref, mod_ref, projs_ref, spool_ref, sconv_ref, sdelta_hbm, *rest, n_seq, n_tok, n_steps, pos0):
    refs = dict(zip(_SAMPLE_WEIGHT_NAMES, rest[:len(_SAMPLE_WEIGHT_NAMES)]))
    (y_ref, pool_out_ref, conv_out_ref, delta_out_hbm,
     extp_ref, extc_ref, sin_ref, sout_ref, in_sem, out_sem) = rest[len(_SAMPLE_WEIGHT_NAMES):]
    i = pl.program_id(0)
    rows = n_seq * n_tok
    per_chunk = CHUNK // n_tok
    n_chunks = n_seq // per_chunk

    def state_in(step, c):
        first = (step * n_chunks + c) * per_chunk
        return pltpu.make_async_copy(sdelta_hbm.at[pl.ds(first, per_chunk)], sin_ref.at[c], in_sem.at[c])

    def state_out(step, c):
        first = (step * n_chunks + c) * per_chunk
        return pltpu.make_async_copy(sout_ref.at[c], delta_out_hbm.at[pl.ds(first, per_chunk)], out_sem.at[c])

    @pl.when(i == 0)
    def _():
        for c in range(n_chunks):
            state_in(0, c).start()

    _state_to_rows(spool_ref, extp_ref, POOL_PAD - POOL_BUF, POOL_BUF, n_seq)
    extp_ref[:, 0:POOL_PAD - POOL_BUF, :] = jnp.zeros((n_seq, POOL_PAD - POOL_BUF, D_MODEL), _F32)
    _state_to_rows(sconv_ref, extc_ref, CONV_PAD - (CONV_W - 1), CONV_W - 1, n_seq)
    extc_ref[:, 0:CONV_PAD - (CONV_W - 1), :] = jnp.zeros((n_seq, CONV_PAD - (CONV_W - 1), 3 * D_MODEL), _F32)

    x3 = x_ref[...]
    mod3 = mod_ref[...]
    pos = pos0 + lax.broadcasted_iota(jnp.int32, (rows, 1), 0) % n_tok
    f = _front(lambda name: projs_ref[:, SEC[name][0]:SEC[name][1]], pos, refs, extp_ref, extc_ref, n_seq, n_tok)
    _rows_to_state(extp_ref, POOL_PAD + n_tok - POOL_BUF, POOL_BUF, pool_out_ref, 0, n_seq)
    _rows_to_state(extc_ref, CONV_PAD + n_tok - (CONV_W - 1), CONV_W - 1, conv_out_ref, 0, n_seq)
    chunks = _chunks_intra(f["q"], f["k"], f["v"], f["beta_e"], f["beta_5"], f["gc_e"], f["gc_5"], n_tok)

    lane_seq = lax.broadcasted_iota(jnp.int32, (HEAD, CHUNK), 1) // n_tok
    os_ = []
    for c, ck in enumerate(chunks):
        state_in(i, c).wait()

        @pl.when(i > 0)
        def _():
            state_out(i - 1, c).wait()

        u_rows, oi_rows = [], []
        for sl_ in range(per_chunk):
            rs = slice(sl_ * n_tok, (sl_ + 1) * n_tok)
            u_h, oi_h = [], []
            for hd in range(N_HEADS):
                cs = slice(hd * HEAD, (hd + 1) * HEAD)
                r = _dot(jnp.concatenate([ck["w_k"][rs, cs], ck["qg"][rs, cs]], axis=0), sin_ref[c, sl_, hd])
                u_h.append(ck["w_v"][rs, cs] - r[:n_tok])
                oi_h.append(r[n_tok:])
            u_rows.append(jnp.concatenate(u_h, axis=1))
            oi_rows.append(jnp.concatenate(oi_h, axis=1))
        u = jnp.concatenate(u_rows, axis=0)
        os_.append(jnp.concatenate(oi_rows, axis=0) + _intra_output(ck["qkd"], u))
        for hd in range(N_HEADS):
            cs = slice(hd * HEAD, (hd + 1) * HEAD)
            kt_t = ck["kt"][:, cs].T
            lhs = jnp.concatenate([jnp.where(lane_seq == sl_, kt_t, 0.0) for sl_ in range(per_chunk)], axis=0)
            upd = _dot(lhs, u[:, cs])
            for sl_ in range(per_chunk):
                gl = ck["gl"][sl_ * n_tok:sl_ * n_tok + 1, cs]
                sout_ref[c, sl_, hd] = gl * sin_ref[c, sl_, hd] + upd[sl_ * HEAD:(sl_ + 1) * HEAD]
        state_out(i, c).start()

        @pl.when(i + 1 < n_steps)
        def _():
            state_in(i + 1, c).start()
    o = jnp.concatenate(os_, axis=0)

    gate = jnp.broadcast_to(mod3[:, :, 2 * D_MODEL:], (n_seq, n_tok, D_MODEL)).reshape(rows, D_MODEL)
    y = _back(x3.reshape(rows, D_MODEL), o, f, gate, refs)
    y_ref[...] = y.reshape(n_seq, n_tok, D_MODEL)

    @pl.when(i == n_steps - 1)
    def _():
        for c in range(n_chunks):
            state_out(i, c).wait()


def _sample_call(x, mod, projs, spool, sconv, sdelta, weights, pos0):
    bsz, n_tok, _ = x.shape
    n_seq = SAMPLE_SEQS
    assert (n_seq * n_tok) % CHUNK == 0 and bsz % n_seq == 0
    n_steps = bsz // n_seq
    n_chunks = n_seq * n_tok // CHUNK
    kern = functools.partial(_sample_kernel, n_seq=n_seq, n_tok=n_tok, n_steps=n_steps, pos0=pos0)
    seq_block = lambda shape: pl.BlockSpec((n_seq,) + shape, lambda i: (i,) + (0,) * len(shape))
    row_state = lambda n_rows, width: pl.BlockSpec((n_rows, n_seq, width), lambda i: (0, i, 0))
    slot = (n_chunks, CHUNK // n_tok, N_HEADS, HEAD, HEAD)
    return pl.pallas_call(
        kern,
        grid=(n_steps,),
        in_specs=[seq_block((n_tok, D_MODEL)), seq_block((1, 3 * D_MODEL)),
                  pl.BlockSpec((n_seq * n_tok, SEC_WIDTH), lambda i: (i, 0)),
                  row_state(POOL_BUF, D_MODEL), row_state(CONV_W - 1, 3 * D_MODEL),
                  pl.BlockSpec(memory_space=pl.ANY)] + [_const_spec(w.shape) for w in weights],
        out_specs=[seq_block((n_tok, D_MODEL)), row_state(POOL_BUF, D_MODEL),
                   row_state(CONV_W - 1, 3 * D_MODEL), pl.BlockSpec(memory_space=pl.ANY)],
        out_shape=[jax.ShapeDtypeStruct((bsz, n_tok, D_MODEL), _F32),
                   jax.ShapeDtypeStruct((POOL_BUF, bsz, D_MODEL), _F32),
                   jax.ShapeDtypeStruct((CONV_W - 1, bsz, 3 * D_MODEL), _F32),
                   jax.ShapeDtypeStruct((bsz, N_HEADS, HEAD, HEAD), _F32)],
        scratch_shapes=[pltpu.VMEM((n_seq, POOL_PAD + n_tok, D_MODEL), _F32),
                        pltpu.VMEM((n_seq, CONV_PAD + n_tok, 3 * D_MODEL), _F32),
                        pltpu.VMEM(slot, _F32), pltpu.VMEM(slot, _F32),
                        pltpu.SemaphoreType.DMA((n_chunks,)), pltpu.SemaphoreType.DMA((n_chunks,))],
        compiler_params=pltpu.CompilerParams(dimension_semantics=("arbitrary",),
                                             vmem_limit_bytes=VMEM_LIMIT_BYTES),
        name="sample_layer",
    )(x, mod, projs, spool, sconv, sdelta, *weights)


def _head_expansion_matrix():
    e = np.zeros((HEAD, 3072), np.float32)
    for piece in range(3):
        for hd in range(N_HEADS):
            rb, rg = piece * 2 * N_HEADS + hd, piece * 2 * N_HEADS + N_HEADS + hd
            e[rb, hd * HEAD:(hd + 1) * HEAD] = 1.0
            e[rb, 1024 + hd * CHUNK:1024 + (hd + 1) * CHUNK] = 1.0
            e[rg, 1536 + hd * HEAD:1536 + (hd + 1) * HEAD] = 1.0
            e[rg, 2560 + hd * CHUNK:2560 + (hd + 1) * CHUNK] = 1.0
    return jnp.asarray(e, _BF)


def _layer_weights(w_in, conv_w, a_log, dt_bias, head_norm_w, pool_w, pool_scale, p_a, p_b, w_out, ln_g, ln_b):
    gatep = jnp.pad(jnp.stack([a_log.reshape(N_HEADS), dt_bias.reshape(N_HEADS)]),
                    ((0, 0), (N_HEADS, HEAD - 2 * N_HEADS)))
    return (jnp.transpose(w_in), conv_w,
            gatep, head_norm_w.reshape(1, HEAD), pool_w.astype(_BF),
            pool_scale.reshape(1, D_MODEL), p_a.astype(_BF), p_b.astype(_BF), w_out.astype(_BF),
            ln_g.reshape(1, D_MODEL), ln_b.reshape(1, D_MODEL), _head_expansion_matrix())


def kernel(x_prompt, x_sample, state_pool, state_conv, state_delta, c_prompt, c_sample, w_ada, b_ada, w_in, conv_w, a_log, dt_bias, head_norm_w, pool_w, pool_scale, p_a, p_b, w_out, ln_g, ln_b):
    assert w_in.shape[0] == 1, "single-layer trunk"
    bp, bs = x_prompt.shape[0], x_sample.shape[0]
    drop = lambda a: a.reshape(a.shape[1:])
    mod = _ada_call(jnp.concatenate([c_sample, c_prompt], axis=0), drop(w_ada), b_ada)
    weights = _layer_weights(drop(w_in), drop(conv_w), a_log, dt_bias, head_norm_w, drop(pool_w), pool_scale,
                             drop(p_a), drop(p_b), drop(w_out), ln_g, ln_b)
    rows_major = lambda a: jnp.transpose(a, (1, 0, 2))
    y_p, pool_p, conv_p, delta_p, proj_s = _prompt_call(x_prompt, mod, x_sample, weights,
                                                        PROMPT_SEQS, PROMPT_TOKENS)
    y_s, pool_s, conv_s, delta_s = _sample_call(x_sample, mod, proj_s, rows_major(drop(state_pool)),
                                                rows_major(drop(state_conv)), drop(state_delta),
                                                weights[_N_PROJ_WEIGHTS:], PAST_LEN)
    lift = lambda a: a.reshape((1,) + a.shape)
    return (y_p, y_s, lift(rows_major(pool_p)), lift(rows_major(conv_p)), lift(delta_p),
            lift(rows_major(pool_s)), lift(rows_major(conv_s)), lift(delta_s))
```

```python
import functools

import numpy as np
import jax
import jax.numpy as jnp
from jax import lax
from jax.experimental import pallas as pl
from jax.experimental.pallas import tpu as pltpu

D_MODEL = 1024
N_HEADS = 8
HEAD = 128
N_PAIRS = N_HEADS // 2
POOL_WINDOWS = (2, 4, 8, 16)
POOL_GROUP = 256
POOL_BUF = 15
POOL_PAD = 16
CONV_W = 4
CONV_PAD = 8
CHUNK = 64
INV_GROUP = 128
INV_BASE = 8
PAST_LEN = 16384
DEEPNORM_ALPHA = 2.0 ** 0.25
LN_EPS = 1e-5
RMS_EPS = 1e-6
L2_EPS = 1e-6
NEG_BIG = -1e30

OFF_MAIN_END = 6 * D_MODEL
OFF_GATE = OFF_MAIN_END + 2 * N_HEADS

SEC = dict(u_a=(0, D_MODEL), z_a=(D_MODEL, 2 * D_MODEL), qkv=(2 * D_MODEL, 5 * D_MODEL),
           z_b=(5 * D_MODEL, 6 * D_MODEL), ba=(OFF_MAIN_END, OFF_MAIN_END + 128),
           ga=(OFF_MAIN_END + 128, OFF_MAIN_END + 128 + D_MODEL),
           gb=(OFF_MAIN_END + 128 + D_MODEL, OFF_MAIN_END + 128 + 2 * D_MODEL))
SEC_WIDTH = OFF_MAIN_END + 128 + 2 * D_MODEL

PROMPT_SEQS = 4
SAMPLE_RIDE = 2
PROMPT_TOKENS = CHUNK
SAMPLE_SEQS = 16
ADA_COLS = 1024
W_SLAB = 432
VMEM_LIMIT_BYTES =58 * 1024 * 1024

_BF = jnp.bfloat16
_F32 = jnp.float32


def _dot(a, b):
    return jnp.dot(a.astype(_BF), b.astype(_BF), preferred_element_type=_F32)


def _dot_nt(a, b):
    return lax.dot_general(a.astype(_BF), b.astype(_BF), (((1,), (1,)), ((), ())),
                           preferred_element_type=_F32)


def _sigmoid(x):
    return 0.5 + 0.5 * jnp.tanh(0.5 * x)


def _silu(x):
    hx = 0.5 * x
    return hx + hx * jnp.tanh(hx)


def _softplus(x):
    return jnp.maximum(x, 0.0) + jnp.log1p(jnp.exp(-jnp.abs(x)))


def _lane_block_diag(x, width):
    nblk = x.shape[1] // width
    lane = lax.broadcasted_iota(jnp.int32, x.shape, 1)
    return jnp.concatenate([jnp.where(lane // width == i, x, 0.0) for i in range(nblk)], axis=0)


def _pair_block_diag(x):
    lane = lax.broadcasted_iota(jnp.int32, x.shape, 1) % (2 * HEAD)
    return jnp.concatenate([jnp.where(lane < HEAD, x, 0.0), jnp.where(lane >= HEAD, x, 0.0)], axis=0)


def _segment_cumsum(x, seg):
    row = lax.broadcasted_iota(jnp.int32, x.shape, 0) % seg
    shift = 1
    while shift < seg:
        x = x + jnp.where(row >= shift, pltpu.roll(x, shift, axis=0), 0.0)
        shift *= 2
    return x


def _expand_heads(narrow, e3_ref):
    lane = lax.broadcasted_iota(jnp.int32, narrow.shape, 1)
    x = jnp.where(lane < 2 * N_HEADS, narrow, 0.0)
    hi = x.astype(_BF).astype(_F32)
    rem = x - hi
    mid = rem.astype(_BF).astype(_F32)
    lo = (rem - mid).astype(_BF).astype(_F32)
    packed = hi + pltpu.roll(mid, 2 * N_HEADS, axis=1) + pltpu.roll(lo, 4 * N_HEADS, axis=1)
    wide = jnp.dot(packed.astype(_BF), e3_ref[...], preferred_element_type=_F32)
    return wide[:, 0:1024], wide[:, 1024:1536], wide[:, 1536:2560], wide[:, 2560:3072]


def _per_head_rsqrt_scale(x, eps, mean, post):
    outs = []
    for h in range(N_HEADS):
        xh = x[:, h * HEAD:(h + 1) * HEAD]
        ss = jnp.sum(xh * xh, axis=-1, keepdims=True)
        if mean:
            ss = ss * (1.0 / HEAD)
        outs.append(xh * (lax.rsqrt(ss + eps) * post))
    return jnp.concatenate(outs, axis=1)


def _gate_scalars(ba, gatep_ref, seg):
    beta = _sigmoid(ba)
    g = -jnp.exp(gatep_ref[0:1, :]) * _softplus(ba + gatep_ref[1:2, :])
    gc = _segment_cumsum(g, seg)
    lane = lax.broadcasted_iota(jnp.int32, ba.shape, 1)
    return jnp.where(lane < N_HEADS, beta, gc)


def _chunk_masks(n_tok):
    row = lax.broadcasted_iota(jnp.int32, (CHUNK, 4 * HEAD), 0)
    col = lax.broadcasted_iota(jnp.int32, (CHUNK, 4 * HEAD), 1) % CHUNK
    same_seq = (row // n_tok) == (col // n_tok)
    return same_seq & (row >= col), same_seq & (row > col), row == col


def _chunks_intra(q, k, v, beta_e, beta_5, gc_e, gc_5, n_tok):
    assert n_tok >= INV_BASE and n_tok & (n_tok - 1) == 0 and CHUNK % n_tok == 0
    n_chunks = q.shape[0] // CHUNK
    incl, strict, eye5 = _chunk_masks(n_tok)
    rows = [slice(c * CHUNK, (c + 1) * CHUNK) for c in range(n_chunks)]
    pair = [slice(p * 2 * HEAD, (p + 1) * 2 * HEAD) for p in range(N_PAIRS)]

    kk, qk = [], []
    for rs in rows:
        kk_parts, qk_parts = [], []
        for sl in pair:
            kp, qp = k[rs, sl], q[rs, sl]
            g2 = _dot_nt(jnp.concatenate([kp, qp], axis=0), _pair_block_diag(kp))
            kk_parts.append(g2[:CHUNK])
            qk_parts.append(g2[CHUNK:])
        kk.append(jnp.concatenate(kk_parts, axis=1))
        qk.append(jnp.concatenate(qk_parts, axis=1))

    dec, glast = [], []
    for rs in rows:
        g5 = gc_5[rs]
        gc_row = jnp.sum(jnp.where(eye5, g5, 0.0), axis=0, keepdims=True)
        dec.append(jnp.exp(jnp.where(incl, g5 - gc_row, NEG_BIG)))
        ge = gc_e[rs]
        glast.append(jnp.concatenate(
            [jnp.broadcast_to(ge[(s + 1) * n_tok - 1:(s + 1) * n_tok, :], (n_tok, D_MODEL))
             for s in range(CHUNK // n_tok)], axis=0))

    row4 = lax.broadcasted_iota(jnp.int32, (CHUNK, INV_GROUP), 0)
    col4 = lax.broadcasted_iota(jnp.int32, (CHUNK, INV_GROUP), 1) % CHUNK
    base_blk = (row4 // INV_BASE) == (col4 // INV_BASE)
    prob = [(c, grp) for c in range(n_chunks) for grp in range(N_HEADS * CHUNK // INV_GROUP)]
    neg_l, t_inv, pw = {}, {}, {}
    for c, grp in prob:
        gs = slice(grp * INV_GROUP, (grp + 1) * INV_GROUP)
        neg_l[c, grp] = jnp.where(strict[:, :INV_GROUP],
                                  -(beta_5[rows[c], gs] * dec[c][:, gs] * kk[c][:, gs]), 0.0)
    for key in prob:
        m0 = jnp.where(base_blk, neg_l[key], 0.0)
        t_inv[key] = eye5[:, :INV_GROUP].astype(_F32) + m0
        pw[key] = _dot(m0, _lane_block_diag(m0, CHUNK))
    for key in prob:
        r = _dot(jnp.concatenate([pw[key], t_inv[key]], axis=0), _lane_block_diag(pw[key], CHUNK))
        pw[key] = r[:CHUNK]
        t_inv[key] = t_inv[key] + r[CHUNK:]
    for key in prob:
        t_inv[key] = t_inv[key] + _dot(t_inv[key], _lane_block_diag(pw[key], CHUNK))
    blk = INV_BASE
    while blk < n_tok:
        below = ((row4 // (2 * blk)) == (col4 // (2 * blk))) & ((row4 // blk) != (col4 // blk))
        for key in prob:
            pw[key] = _dot(jnp.where(below, neg_l[key], 0.0), _lane_block_diag(t_inv[key], CHUNK))
        for key in prob:
            t_inv[key] = t_inv[key] + _dot(t_inv[key], _lane_block_diag(pw[key], CHUNK))
        blk *= 2
    s = t_inv

    out = []
    for c, rs in enumerate(rows):
        gam = jnp.exp(gc_e[rs])
        be = beta_e[rs]
        bv = be * v[rs]
        gbk = be * gam * k[rs]
        wv_parts, wk_parts = [], []
        for p, sl in enumerate(pair):
            t_pair = s[c, p]
            sol = _dot(t_pair, _pair_block_diag(jnp.concatenate([bv[:, sl], gbk[:, sl]], axis=1)))
            wv_parts.append(sol[:, :2 * HEAD])
            wk_parts.append(sol[:, 2 * HEAD:])
        out.append(dict(w_v=jnp.concatenate(wv_parts, axis=1), w_k=jnp.concatenate(wk_parts, axis=1),
                        qg=q[rs] * gam, kt=k[rs] * jnp.exp(glast[c] - gc_e[rs]),
                        qkd=qk[c] * dec[c], gl=jnp.exp(glast[c])))
    return out


def _intra_output(qkd, u):
    outs = []
    for p in range(N_PAIRS):
        up = u[:, p * 2 * HEAD:(p + 1) * 2 * HEAD]
        outs.append(_dot(qkd[:, p * HEAD:(p + 1) * HEAD], _pair_block_diag(up)))
    return jnp.concatenate(outs, axis=1)


def _state_to_rows(state_ref, ext_ref, row0, n_rows, n_seq):
    for s in range(n_seq):
        for r in range(n_rows):
            ext_ref[s, row0 + r:row0 + r + 1, :] = state_ref[r, s:s + 1, :]


def _rows_to_state(ext_ref, row0, n_rows, state_ref, seq0, n_seq):
    for s in range(n_seq):
        for r in range(n_rows):
            state_ref[r, seq0 + s:seq0 + s + 1, :] = ext_ref[s, row0 + r:row0 + r + 1, :]


def _modulated(x3, mod3):
    shift, scale = mod3[:, :, 0:D_MODEL], mod3[:, :, D_MODEL:2 * D_MODEL]
    return (x3 * (1.0 + scale) + shift).reshape(x3.shape[0] * x3.shape[1], D_MODEL).astype(_BF)


def _front(sec, pos, refs, extp_ref, extc_ref, n_seq, n_tok):
    rows = n_seq * n_tok
    extc_ref[:, CONV_PAD:CONV_PAD + n_tok, :] = sec("qkv").reshape(n_seq, n_tok, 3 * D_MODEL)

    seg_c = CONV_PAD + n_tok
    convw_ref = refs["convw"]
    e = extc_ref[...].reshape(n_seq * seg_c, 3 * D_MODEL)
    acc = e * convw_ref[0:1, :]
    for i in range(1, CONV_W):
        acc = pltpu.roll(acc, 1, axis=0) + e * convw_ref[i:i + 1, :]
    qkv = _silu(acc.reshape(n_seq, seg_c, 3 * D_MODEL)[:, CONV_PAD:, :].reshape(rows, 3 * D_MODEL))
    q = _per_head_rsqrt_scale(qkv[:, 0:D_MODEL], L2_EPS, False, HEAD ** -0.5)
    k = _per_head_rsqrt_scale(qkv[:, D_MODEL:2 * D_MODEL], L2_EPS, False, 1.0)
    v = qkv[:, 2 * D_MODEL:]

    extp_ref[:, POOL_PAD:POOL_PAD + n_tok, :] = sec("u_a").reshape(n_seq, n_tok, D_MODEL)
    z_a, ba, z_b, ga, gb = sec("z_a"), sec("ba"), sec("z_b"), sec("ga"), sec("gb")

    seg_p = POOL_PAD + n_tok
    pooled = []
    for gi, w in enumerate(POOL_WINDOWS):
        e = extp_ref[:, :, gi * POOL_GROUP:(gi + 1) * POOL_GROUP].reshape(n_seq * seg_p, POOL_GROUP)
        acc, span = e, 1
        while span < w:
            acc = acc + pltpu.roll(acc, span, axis=0)
            span *= 2
        take = lambda a: a.reshape(n_seq, seg_p, POOL_GROUP)[:, POOL_PAD:, :].reshape(rows, POOL_GROUP)
        inv_cnt = 1.0 / jnp.minimum(pos + 1, w).astype(_F32)
        pooled.append(take(acc) * inv_cnt - take(e))
    mixed = jnp.concatenate([_dot(pp, refs["poolw"][gi]) for gi, pp in enumerate(pooled)], axis=1)
    y_a = (mixed * refs["pscale"][...] * _silu(z_a)).astype(_BF)
    a_proj = _dot(y_a, refs["pa"][...])

    beta_e, beta_5, gc_e, gc_5 = _expand_heads(_gate_scalars(ba, refs["gatep"], n_tok), refs["e3"])
    return dict(q=q, k=k, v=v, beta_e=beta_e, beta_5=beta_5, gc_e=gc_e, gc_5=gc_5,
                a_proj=a_proj, z_b=z_b, ga=ga, gb=gb)


def _back(x, o, f, gate, refs):
    o = _per_head_rsqrt_scale(o, RMS_EPS, True, refs["hnw"][...])
    y_b = (o * _silu(f["z_b"])).astype(_BF)
    merged = _sigmoid(f["ga"]) * f["a_proj"] + _sigmoid(f["gb"]) * _dot(y_b, refs["pb"][...])
    sub = (1.0 + gate) * _dot(merged, refs["wout"][...])
    r = DEEPNORM_ALPHA * x + sub
    mu = jnp.mean(r, axis=-1, keepdims=True)
    rc = r - mu
    var = jnp.mean(rc * rc, axis=-1, keepdims=True)
    return rc * lax.rsqrt(var + LN_EPS) * refs["lng"][...] + refs["lnb"][...]


_WEIGHT_NAMES = ("w_t", "convw", "gatep", "hnw", "poolw", "pscale", "pa", "pb",
                 "wout", "lng", "lnb", "e3")
_N_PROJ_WEIGHTS = 1
_SAMPLE_WEIGHT_NAMES = _WEIGHT_NAMES[_N_PROJ_WEIGHTS:]


def _ada_kernel(c0_ref, c1_ref, w_ref, b_ref, o_ref):
    c = jnp.concatenate([c0_ref[...], c1_ref[...]], axis=0)
    o_ref[:, 0, :] = _dot(_silu(c), w_ref[...]) + b_ref[...]


def _ada_call(c0, c1, w_ada, b_ada):
    n = c0.shape[0] + c1.shape[0]
    return pl.pallas_call(
        _ada_kernel,
        grid=(3 * D_MODEL // ADA_COLS,),
        in_specs=[pl.BlockSpec(c0.shape, lambda j: (0, 0)), pl.BlockSpec(c1.shape, lambda j: (0, 0)),
                  pl.BlockSpec((D_MODEL, ADA_COLS), lambda j: (0, j)),
                  pl.BlockSpec((1, ADA_COLS), lambda j: (0, j))],
        out_specs=pl.BlockSpec((n, 1, ADA_COLS), lambda j: (0, 0, j)),
        out_shape=jax.ShapeDtypeStruct((n, 1, 3 * D_MODEL), _F32),
        compiler_params=pltpu.CompilerParams(dimension_semantics=("arbitrary",)),
        name="adaln_mod",
    )(c0, c1, w_ada, b_ada)


def _prompt_kernel(x_ref, mod_ref, xs_ref, mods_ref, *rest, n_seq, n_tok, n_steps, n_groups):
    refs = dict(zip(_WEIGHT_NAMES, rest[:len(_WEIGHT_NAMES)]))
    (y_ref, pool_out_ref, conv_out_ref, delta_out_ref, projs_ref,
     extp_ref, extc_ref, sbd_ref, w_t_ref, stage_ref, stage_sem) = rest[len(_WEIGHT_NAMES):]
    t = pl.program_id(1)
    rows = n_seq * n_tok

    @pl.when((pl.program_id(0) == 0) & (t == 0))
    def _():
        n_slabs = w_t_ref.shape[0] // W_SLAB

        def slab_in(k):
            return pltpu.make_async_copy(refs["w_t"].at[pl.ds(k * W_SLAB, W_SLAB)], stage_ref.at[k % 2],
                                         stage_sem.at[k % 2])

        slab_in(0).start()
        slab_in(1).start()
        for k in range(n_slabs):
            slab_in(k).wait()
            w_t_ref[k * W_SLAB:(k + 1) * W_SLAB, :] = stage_ref[k % 2].astype(_BF)
            if k + 2 < n_slabs:
                slab_in(k + 2).start()

    @pl.when(t == 0)
    def _():
        extp_ref[:, 0:POOL_PAD, :] = jnp.zeros((n_seq, POOL_PAD, D_MODEL), _F32)
        extc_ref[:, 0:CONV_PAD, :] = jnp.zeros((n_seq, CONV_PAD, 3 * D_MODEL), _F32)
        sbd_ref[...] = jnp.zeros(sbd_ref.shape, _F32)

    x3 = x_ref[...]
    mod3 = mod_ref[...]
    pos = t * n_tok + lax.broadcasted_iota(jnp.int32, (rows, 1), 0) % n_tok
    h = jnp.concatenate([_modulated(x3, mod3), _modulated(xs_ref[...], mods_ref[...])], axis=0)
    w_rows = dict(SEC, ba=(OFF_MAIN_END, OFF_MAIN_END + HEAD), ga=(OFF_GATE, OFF_GATE + D_MODEL),
                  gb=(OFF_GATE + D_MODEL, OFF_GATE + 2 * D_MODEL))

    def sec(name):
        r0, r1 = w_rows[name]
        full = lax.dot_general(h, w_t_ref[r0:r1, :], (((1,), (1,)), ((), ())), preferred_element_type=_F32)
        projs_ref[:, SEC[name][0]:SEC[name][1]] = full[rows:]
        return full[:rows]

    f = _front(sec, pos, refs, extp_ref, extc_ref, n_seq, n_tok)
    chunks = _chunks_intra(f["q"], f["k"], f["v"], f["beta_e"], f["beta_5"], f["gc_e"], f["gc_5"], n_tok)

    pair = [slice(p * 2 * HEAD, (p + 1) * 2 * HEAD) for p in range(N_PAIRS)]
    row_bd = lax.broadcasted_iota(jnp.int32, (2 * HEAD, 2 * HEAD), 0) // HEAD
    col_bd = lax.broadcasted_iota(jnp.int32, (2 * HEAD, 2 * HEAD), 1) // HEAD
    same_head = row_bd == col_bd
    us, ois = [], []
    for c, ck in enumerate(chunks):
        u_parts, oi_parts = [], []
        for p, sl in enumerate(pair):
            r = _dot(jnp.concatenate([ck["w_k"][:, sl], ck["qg"][:, sl]], axis=0), sbd_ref[c * N_PAIRS + p])
            u_parts.append(ck["w_v"][:, sl] - r[:CHUNK])
            oi_parts.append(r[CHUNK:])
        us.append(jnp.concatenate(u_parts, axis=1))
        ois.append(jnp.concatenate(oi_parts, axis=1))
    o = jnp.concatenate([ois[c] + _intra_output(ck["qkd"], us[c]) for c, ck in enumerate(chunks)], axis=0)
    for c, ck in enumerate(chunks):
        for p, sl in enumerate(pair):
            upd = _dot(ck["kt"][:, sl].T, us[c][:, sl])
            i = c * N_PAIRS + p
            sbd_ref[i] = ck["gl"][0:1, sl] * sbd_ref[i] + jnp.where(same_head, upd, 0.0)

    gate = jnp.broadcast_to(mod3[:, :, 2 * D_MODEL:], (n_seq, n_tok, D_MODEL)).reshape(rows, D_MODEL)
    y = _back(x3.reshape(rows, D_MODEL), o, f, gate, refs)
    y_ref[...] = y.reshape(n_seq, n_tok, D_MODEL)

    @pl.when(t == n_steps - 1)
    def _():
        for s in range(n_seq):
            for hd in range(N_HEADS):
                o0 = (hd % 2) * HEAD
                delta_out_ref[s, hd] = sbd_ref[s * N_PAIRS + hd // 2, o0:o0 + HEAD, o0:o0 + HEAD]

    for grp in range(n_groups):
        @pl.when((t == n_steps - 1) & (pl.program_id(0) == grp))
        def _():
            _rows_to_state(extp_ref, POOL_PAD + n_tok - POOL_BUF, POOL_BUF, pool_out_ref, grp * n_seq, n_seq)
            _rows_to_state(extc_ref, CONV_PAD + n_tok - (CONV_W - 1), CONV_W - 1, conv_out_ref, grp * n_seq, n_seq)

    @pl.when(t < n_steps - 1)
    def _():
        extp_ref[:, 0:POOL_PAD, :] = extp_ref[:, n_tok:n_tok + POOL_PAD, :]
        extc_ref[:, 0:CONV_PAD, :] = extc_ref[:, n_tok:n_tok + CONV_PAD, :]


def _const_spec(shape):
    nd = len(shape)
    return pl.BlockSpec(shape, lambda *_: (0,) * nd, pipeline_mode=pl.Buffered(1))


def _weight_specs(weights):
    return [pl.BlockSpec(memory_space=pl.ANY)] + [_const_spec(w.shape) for w in weights[_N_PROJ_WEIGHTS:]]


def _prompt_call(x, mod, xs, weights, n_seq, n_tok):
    bsz, seq, _ = x.shape
    n_steps = seq // n_tok
    n_sample, s_tok, _ = xs.shape
    assert n_tok == CHUNK and bsz % n_seq == 0 and seq % n_tok == 0
    w_t = weights[0]
    assert w_t.shape[0] % W_SLAB == 0 and w_t.shape[0] // W_SLAB >= 2
    assert n_sample == SAMPLE_RIDE * (bsz // n_seq) * n_steps
    assert n_sample % n_seq == 0 and mod.shape[0] == n_sample + bsz
    mod_blk0 = n_sample // n_seq
    ride = lambda shape: pl.BlockSpec((SAMPLE_RIDE,) + shape, lambda g, t: (g * n_steps + t,) + (0,) * len(shape))
    kern = functools.partial(_prompt_kernel, n_seq=n_seq, n_tok=n_tok, n_steps=n_steps, n_groups=bsz // n_seq)
    seq_block = lambda shape: pl.BlockSpec((n_seq,) + shape, lambda g, t: (g,) + (0,) * len(shape))
    return pl.pallas_call(
        kern,
        grid=(bsz // n_seq, n_steps),
        in_specs=[pl.BlockSpec((n_seq, n_tok, D_MODEL), lambda g, t: (g, t, 0)),
                  pl.BlockSpec((n_seq, 1, 3 * D_MODEL), lambda g, t: (mod_blk0 + g, 0, 0)),
                  ride((s_tok, D_MODEL)), ride((1, 3 * D_MODEL))]
                 + _weight_specs(weights),
        out_specs=[pl.BlockSpec((n_seq, n_tok, D_MODEL), lambda g, t: (g, t, 0)),
                   pl.BlockSpec((POOL_BUF, bsz, D_MODEL), lambda g, t: (0, 0, 0)),
                   pl.BlockSpec((CONV_W - 1, bsz, 3 * D_MODEL), lambda g, t: (0, 0, 0)),
                   seq_block((N_HEADS, HEAD, HEAD)),
                   pl.BlockSpec((SAMPLE_RIDE * s_tok, SEC_WIDTH), lambda g, t: (g * n_steps + t, 0))],
        out_shape=[jax.ShapeDtypeStruct((bsz, seq, D_MODEL), _F32),
                   jax.ShapeDtypeStruct((POOL_BUF, bsz, D_MODEL), _F32),
                   jax.ShapeDtypeStruct((CONV_W - 1, bsz, 3 * D_MODEL), _F32),
                   jax.ShapeDtypeStruct((bsz, N_HEADS, HEAD, HEAD), _F32),
                   jax.ShapeDtypeStruct((n_sample * s_tok, SEC_WIDTH), _F32)],
        scratch_shapes=[pltpu.VMEM((n_seq, POOL_PAD + n_tok, D_MODEL), _F32),
                        pltpu.VMEM((n_seq, CONV_PAD + n_tok, 3 * D_MODEL), _F32),
                        pltpu.VMEM((n_seq * N_PAIRS, 2 * HEAD, 2 * HEAD), _F32),
                        pltpu.VMEM(w_t.shape, _BF), pltpu.VMEM((2, W_SLAB, D_MODEL), _F32),
                        pltpu.SemaphoreType.DMA((2,))],
        compiler_params=pltpu.CompilerParams(dimension_semantics=("arbitrary", "arbitrary"),
                                             vmem_limit_bytes=VMEM_LIMIT_BYTES),
        name="prompt_layer",
    )(x, mod, xs, mod, *weights)


def _sample_kernel(x_ref, mod_ref, projs_ref, spool_ref, sconv_ref, sdelta_hbm, *rest, n_seq, n_tok, n_steps, pos0):
    refs = dict(zip(_SAMPLE_WEIGHT_NAMES, rest[:len(_SAMPLE_WEIGHT_NAMES)]))
    (y_ref, pool_out_ref, conv_out_ref, delta_out_hbm,
     extp_ref, extc_ref, sin_ref, sout_ref, in_sem, out_sem) = rest[len(_SAMPLE_WEIGHT_NAMES):]
    i = pl.program_id(0)
    rows = n_seq * n_tok
    per_chunk = CHUNK // n_tok
    n_chunks = n_seq // per_chunk

    def state_in(step, c):
        first = (step * n_chunks + c) * per_chunk
        return pltpu.make_async_copy(sdelta_hbm.at[pl.ds(first, per_chunk)], sin_ref.at[c], in_sem.at[c])

    def state_out(step, c):
        first = (step * n_chunks + c) * per_chunk
        return pltpu.make_async_copy(sout_ref.at[c], delta_out_hbm.at[pl.ds(first, per_chunk)], out_sem.at[c])

    @pl.when(i == 0)
    def _():
        for c in range(n_chunks):
            state_in(0, c).start()

    _state_to_rows(spool_ref, extp_ref, POOL_PAD - POOL_BUF, POOL_BUF, n_seq)
    extp_ref[:, 0:POOL_PAD - POOL_BUF, :] = jnp.zeros((n_seq, POOL_PAD - POOL_BUF, D_MODEL), _F32)
    _state_to_rows(sconv_ref, extc_ref, CONV_PAD - (CONV_W - 1), CONV_W - 1, n_seq)
    extc_ref[:, 0:CONV_PAD - (CONV_W - 1), :] = jnp.zeros((n_seq, CONV_PAD - (CONV_W - 1), 3 * D_MODEL), _F32)

    x3 = x_ref[...]
    mod3 = mod_ref[...]
    pos = pos0 + lax.broadcasted_iota(jnp.int32, (rows, 1), 0) % n_tok
    f = _front(lambda name: projs_ref[:, SEC[name][0]:SEC[name][1]], pos, refs, extp_ref, extc_ref, n_seq, n_tok)
    _rows_to_state(extp_ref, POOL_PAD + n_tok - POOL_BUF, POOL_BUF, pool_out_ref, 0, n_seq)
    _rows_to_state(extc_ref, CONV_PAD + n_tok - (CONV_W - 1), CONV_W - 1, conv_out_ref, 0, n_seq)
    chunks = _chunks_intra(f["q"], f["k"], f["v"], f["beta_e"], f["beta_5"], f["gc_e"], f["gc_5"], n_tok)

    lane_seq = lax.broadcasted_iota(jnp.int32, (HEAD, CHUNK), 1) // n_tok
    os_ = []
    for c, ck in enumerate(chunks):
        state_in(i, c).wait()

        @pl.when(i > 0)
        def _():
            state_out(i - 1, c).wait()

        u_rows, oi_rows = [], []
        for sl_ in range(per_chunk):
            rs = slice(sl_ * n_tok, (sl_ + 1) * n_tok)
            u_h, oi_h = [], []
            for hd in range(N_HEADS):
                cs = slice(hd * HEAD, (hd + 1) * HEAD)
                r = _dot(jnp.concatenate([ck["w_k"][rs, cs], ck["qg"][rs, cs]], axis=0), sin_ref[c, sl_, hd])
                u_h.append(ck["w_v"][rs, cs] - r[:n_tok])
                oi_h.append(r[n_tok:])
            u_rows.append(jnp.concatenate(u_h, axis=1))
            oi_rows.append(jnp.concatenate(oi_h, axis=1))
        u = jnp.concatenate(u_rows, axis=0)
        os_.append(jnp.concatenate(oi_rows, axis=0) + _intra_output(ck["qkd"], u))
        for hd in range(N_HEADS):
            cs = slice(hd * HEAD, (hd + 1) * HEAD)
            kt_t = ck["kt"][:, cs].T
            lhs = jnp.concatenate([jnp.where(lane_seq == sl_, kt_t, 0.0) for sl_ in range(per_chunk)], axis=0)
            upd = _dot(lhs, u[:, cs])
            for sl_ in range(per_chunk):
                gl = ck["gl"][sl_ * n_tok:sl_ * n_tok + 1, cs]
                sout_ref[c, sl_, hd] = gl * sin_ref[c, sl_, hd] + upd[sl_ * HEAD:(sl_ + 1) * HEAD]
        state_out(i, c).start()

        @pl.when(i + 1 < n_steps)
        def _():
            state_in(i + 1, c).start()
    o = jnp.concatenate(os_, axis=0)

    gate = jnp.broadcast_to(mod3[:, :, 2 * D_MODEL:], (n_seq, n_tok, D_MODEL)).reshape(rows, D_MODEL)
    y = _back(x3.reshape(rows, D_MODEL), o, f, gate, refs)
    y_ref[...] = y.reshape(n_seq, n_tok, D_MODEL)

    @pl.when(i == n_steps - 1)
    def _():
        for c in range(n_chunks):
            state_out(i, c).wait()


def _sample_call(x, mod, projs, spool, sconv, sdelta, weights, pos0):
    bsz, n_tok, _ = x.shape
    n_seq = SAMPLE_SEQS
    assert (n_seq * n_tok) % CHUNK == 0 and bsz % n_seq == 0
    n_steps = bsz // n_seq
    n_chunks = n_seq * n_tok // CHUNK
    kern = functools.partial(_sample_kernel, n_seq=n_seq, n_tok=n_tok, n_steps=n_steps, pos0=pos0)
    seq_block = lambda shape: pl.BlockSpec((n_seq,) + shape, lambda i: (i,) + (0,) * len(shape))
    row_state = lambda n_rows, width: pl.BlockSpec((n_rows, n_seq, width), lambda i: (0, i, 0))
    slot = (n_chunks, CHUNK // n_tok, N_HEADS, HEAD, HEAD)
    return pl.pallas_call(
        kern,
        grid=(n_steps,),
        in_specs=[seq_block((n_tok, D_MODEL)), seq_block((1, 3 * D_MODEL)),
                  pl.BlockSpec((n_seq * n_tok, SEC_WIDTH), lambda i: (i, 0)),
                  row_state(POOL_BUF, D_MODEL), row_state(CONV_W - 1, 3 * D_MODEL),
                  pl.BlockSpec(memory_space=pl.ANY)] + [_const_spec(w.shape) for w in weights],
        out_specs=[seq_block((n_tok, D_MODEL)), row_state(POOL_BUF, D_MODEL),
                   row_state(CONV_W - 1, 3 * D_MODEL), pl.BlockSpec(memory_space=pl.ANY)],
        out_shape=[jax.ShapeDtypeStruct((bsz, n_tok, D_MODEL), _F32),
                   jax.ShapeDtypeStruct((POOL_BUF, bsz, D_MODEL), _F32),
                   jax.ShapeDtypeStruct((CONV_W - 1, bsz, 3 * D_MODEL), _F32),
                   jax.ShapeDtypeStruct((bsz, N_HEADS, HEAD, HEAD), _F32)],
        scratch_shapes=[pltpu.VMEM((n_seq, POOL_PAD + n_tok, D_MODEL), _F32),
                        pltpu.VMEM((n_seq, CONV_PAD + n_tok, 3 * D_MODEL), _F32),
                        pltpu.VMEM(slot, _F32), pltpu.VMEM(slot, _F32),
                        pltpu.SemaphoreType.DMA((n_chunks,)), pltpu.SemaphoreType.DMA((n_chunks,))],
        compiler_params=pltpu.CompilerParams(dimension_semantics=("arbitrary",),
                                             vmem_limit_bytes=VMEM_LIMIT_BYTES),
        name="sample_layer",
    )(x, mod, projs, spool, sconv, sdelta, *weights)


def _head_expansion_matrix():
    e = np.zeros((HEAD, 3072), np.float32)
    for piece in range(3):
        for hd in range(N_HEADS):
            rb, rg = piece * 2 * N_HEADS + hd, piece * 2 * N_HEADS + N_HEADS + hd
            e[rb, hd * HEAD:(hd + 1) * HEAD] = 1.0
            e[rb, 1024 + hd * CHUNK:1024 + (hd + 1) * CHUNK] = 1.0
            e[rg, 1536 + hd * HEAD:1536 + (hd + 1) * HEAD] = 1.0
            e[rg, 2560 + hd * CHUNK:2560 + (hd + 1) * CHUNK] = 1.0
    return jnp.asarray(e, _BF)


def _layer_weights(w_in, conv_w, a_log, dt_bias, head_norm_w, pool_w, pool_scale, p_a, p_b, w_out, ln_g, ln_b):
    gatep = jnp.pad(jnp.stack([a_log.reshape(N_HEADS), dt_bias.reshape(N_HEADS)]),
                    ((0, 0), (N_HEADS, HEAD - 2 * N_HEADS)))
    return (jnp.transpose(w_in), conv_w,
            gatep, head_norm_w.reshape(1, HEAD), pool_w.astype(_BF),
            pool_scale.reshape(1, D_MODEL), p_a.astype(_BF), p_b.astype(_BF), w_out.astype(_BF),
            ln_g.reshape(1, D_MODEL), ln_b.reshape(1, D_MODEL), _head_expansion_matrix())


def kernel(x_prompt, x_sample, state_pool, state_conv, state_delta, c_prompt, c_sample, w_ada, b_ada, w_in, conv_w, a_log, dt_bias, head_norm_w, pool_w, pool_scale, p_a, p_b, w_out, ln_g, ln_b):
    assert w_in.shape[0] == 1, "single-layer trunk"
    drop = lambda a: a.reshape(a.shape[1:])
    mod = _ada_call(c_sample, c_prompt, drop(w_ada), b_ada)
    weights = _layer_weights(drop(w_in), drop(conv_w), a_log, dt_bias, head_norm_w, drop(pool_w), pool_scale,
                             drop(p_a), drop(p_b), drop(w_out), ln_g, ln_b)
    rows_major = lambda a: jnp.transpose(a, (1, 0, 2))
    y_p, pool_p, conv_p, delta_p, proj_s = _prompt_call(x_prompt, mod, x_sample, weights,
                                                        PROMPT_SEQS, PROMPT_TOKENS)
    y_s, pool_s, conv_s, delta_s = _sample_call(x_sample, mod, proj_s, rows_major(drop(state_pool)),
                                                rows_major(drop(state_conv)), drop(state_delta),
                                                weights[_N_PROJ_WEIGHTS:], PAST_LEN)
    lift = lambda a: a.reshape((1,) + a.shape)
    return (y_p, y_s, lift(rows_major(pool_p)), lift(rows_major(conv_p)), lift(delta_p),
            lift(rows_major(pool_s)), lift(rows_major(conv_s)), lift(delta_s))
```
